```python
import math
import jax, jax.numpy as jnp
from jax import lax
import numpy as np

D_MODEL = 1024
BATCH = 2
SEQ = 8192
DEPTH = 2
DEC_BATCH = 128
DEC_SEQ = 8
PAST_LEN = 2048
PAGE_SIZE = 128

N_MIXERS = 4
GROUP_W = D_MODEL // N_MIXERS
A_HEADS = 4
A_HALF = GROUP_W // (2 * A_HEADS)
A_VDIM = 2 * A_HALF
ROPE_THETA = 10000.0
Q_BLOCK = 128
NEG_BIG = -1e30
B_GROUPS = 4
B_GDIM = GROUP_W // B_GROUPS
SG_CHUNK = 128
CONV_W = 31
D_HEADS = 4
D_KDIM = GROUP_W // D_HEADS
D_VDIM = GROUP_W // D_HEADS
HGRN_CHUNK = 16
F_FLOOR = 1e-30
ALPHA = (2.0 * DEPTH) ** 0.25
BETA = (8.0 * DEPTH) ** -0.25
EPS = 1e-5
IN_SIZES = (GROUP_W, GROUP_W, GROUP_W, GROUP_W,
            GROUP_W, GROUP_W, GROUP_W,
            2 * GROUP_W, GROUP_W,
            GROUP_W, GROUP_W, GROUP_W, GROUP_W)
D_IN = sum(IN_SIZES)
SPLITS = [int(s) for s in np.cumsum(IN_SIZES)[:-1]]

kernel_name = 'hymba_diffattn_gmlp_conformer_hgrn2_step'

F32 = jnp.float32


def _layer_norm(x, g, b):
    xf = x.astype(F32)
    xc = xf - xf.mean(-1, keepdims=True)
    var = (xc * xc).mean(-1, keepdims=True)
    return xc * lax.rsqrt(var + EPS) * g.astype(F32) + b.astype(F32)


def _rms_norm(x, g):
    xf = x.astype(F32)
    return xf * lax.rsqrt((xf * xf).mean(-1, keepdims=True) + EPS) * g.astype(F32)


def _rope(x, pos):
    d = x.shape[-1]
    half = d // 2
    inv = ROPE_THETA ** (-jnp.arange(half, dtype=F32) * 2.0 / d)
    ang = pos.astype(F32)[:, None] * inv[None, :]
    cos = jnp.cos(ang)[:, None, :]
    sin = jnp.sin(ang)[:, None, :]
    xf = x.astype(F32)
    x1, x2 = xf[..., :half], xf[..., half:]
    return jnp.concatenate([x1 * cos - x2 * sin, x2 * cos + x1 * sin], -1)


def _diff_attend(q1, q2, k1, k2, v, qpos, kpos, lam):
    scale = q1.shape[-1] ** -0.5
    visible = kpos[None, :] <= qpos[:, None]

    def probs(q, k):
        s = jnp.einsum('bqhd,bkhd->bhqk', q.astype(F32), k.astype(F32)) * scale
        return jax.nn.softmax(jnp.where(visible, s, NEG_BIG), axis=-1)

    p = probs(q1, k1) - lam * probs(q2, k2)
    return jnp.einsum('bhqk,bkhd->bqhd', p, v.astype(F32))


def _diff_attn_blocks(q1, q2, k1, k2, v, lam):
    B, T, H, _ = q1.shape
    nb = T // Q_BLOCK
    kpos = jnp.arange(T)

    def to_blocks(a):
        return a.reshape(B, nb, Q_BLOCK, H, a.shape[-1]).swapaxes(0, 1)

    def one(args):
        qa, qb, qp = args
        return _diff_attend(qa, qb, k1, k2, v, qp, kpos, lam)

    out = lax.map(one, (to_blocks(q1), to_blocks(q2), kpos.reshape(nb, Q_BLOCK)))
    return out.swapaxes(0, 1).reshape(B, T, H, v.shape[-1])


def _spatial_gate(u, v, ws, bs, g, b):
    Bn, T, W = v.shape
    vn = _layer_norm(v, g, b)
    Tp = -(-T // SG_CHUNK) * SG_CHUNK
    vc = jnp.pad(vn, ((0, 0), (0, Tp - T), (0, 0))).reshape(Bn, Tp // SG_CHUNK, SG_CHUNK, B_GROUPS, B_GDIM)
    wm = jnp.tril(ws.astype(F32))
    mixed = jnp.einsum('gts,bcsgd->bctgd', wm, vc) + bs.astype(F32).T[None, None, :, :, None]
    mixed = mixed.reshape(Bn, Tp, W)[:, :T]
    return u.astype(F32) * mixed, vn


def _conformer_conv(a, buf, cw, cb, ng, nb, wpw):
    glu = a[..., :GROUP_W] * jax.nn.sigmoid(a[..., GROUP_W:])
    hc = jnp.concatenate([buf.astype(glu.dtype), glu], 1)
    y = lax.conv_general_dilated(hc, cw[:, None, :].astype(hc.dtype), window_strides=(1,), padding='VALID',
                                 dimension_numbers=('NWC', 'WIO', 'NWC'), feature_group_count=GROUP_W)
    y = jax.nn.silu(_layer_norm(y + cb, ng, nb))
    return jnp.einsum('btc,cd->btd', y, wpw), hc[:, -(CONV_W - 1):]


def _hgrn2(q, log_f, k, v, S0):
    Bn, T, H, _ = q.shape
    dv = v.shape[-1]
    C = HGRN_CHUNK
    Tp = -(-T // C) * C

    def chunks(a):
        a = jnp.pad(a, ((0, 0), (0, Tp - T), (0, 0), (0, 0)))
        return a.reshape(Bn, Tp // C, C, H, a.shape[-1]).transpose(1, 0, 3, 2, 4)

    causal = jnp.tril(jnp.ones((C, C), bool))[:, :, None]

    def step(S, inp):
        qc, lfc, kc, vc = inp
        cum = jnp.cumsum(lfc, axis=2)
        o_inter = jnp.einsum('bhtk,bhkv->bhtv', qc * jnp.exp(cum), S)
        diff = jnp.where(causal, cum[:, :, :, None, :] - cum[:, :, None, :, :], 0.0)
        decay = jnp.where(causal, jnp.exp(diff), 0.0)
        att = jnp.einsum('bhtk,bhtsk,bhsk->bhts', qc, decay, kc)
        o = o_inter + jnp.einsum('bhts,bhsv->bhtv', att, vc)
        last = cum[:, :, -1:, :]
        S_new = jnp.exp(last[:, :, 0, :])[..., None] * S + jnp.einsum('bhsk,bhsv->bhkv', kc * jnp.exp(last - cum), vc)
        return S_new, o

    S_fin, o = lax.scan(step, S0, (chunks(q), chunks(log_f), chunks(k), chunks(v)))
    o = o.transpose(1, 0, 3, 2, 4).reshape(Bn, Tp, H, dv)[:, :T]
    return o, S_fin


def _mixer_layer(l, x, c, pos, kv_past, buf, S0, lb, wts):
    (w_ada, b_ada, w_in, lam_qk, attn_norm_g, sg_norm_g, sg_norm_b, w_s, b_s, conv_w, conv_b,
     conv_norm_g, conv_norm_b, w_pw, hgrn_norm_g, w_out, ln_g, ln_b) = wts
    dt = x.dtype
    Bn, T, _ = x.shape
    mod = jax.nn.silu(c) @ w_ada[l] + b_ada[l]
    shift, scale, gate = jnp.split(mod, 3, axis=-1)
    h = x * (1 + scale[:, None, :]) + shift[:, None, :]
    z = h @ w_in[l]
    aq, ak, av, ag, bu, bv, bg, cin, cg, dq, df, di, dg = jnp.split(z, SPLITS, axis=-1)

    aq = aq.reshape(Bn, T, A_HEADS, 2 * A_HALF)
    ak = ak.reshape(Bn, T, A_HEADS, 2 * A_HALF)
    q1 = _rope(aq[..., :A_HALF], pos)
    q2 = _rope(aq[..., A_HALF:], pos)
    k_rows = jnp.concatenate([_rope(ak[..., :A_HALF], pos), _rope(ak[..., A_HALF:], pos)], -1).astype(dt)
    v_rows = av.reshape(Bn, T, A_HEADS, A_VDIM)
    lp = lam_qk[l].astype(F32)
    lam_init = 0.8 - 0.6 * math.exp(-0.3 * l)
    lam = jnp.exp(jnp.sum(lp[0] * lp[1])) - jnp.exp(jnp.sum(lp[2] * lp[3])) + lam_init
    if kv_past is None:
        o_a = _diff_attn_blocks(q1, q2, k_rows[..., :A_HALF], k_rows[..., A_HALF:], v_rows, lam)
    else:
        k_all = jnp.concatenate([kv_past[0].astype(dt), k_rows], 1)
        v_all = jnp.concatenate([kv_past[1].astype(dt), v_rows], 1)
        kpos = jnp.arange(k_all.shape[1])
        o_a = _diff_attend(q1, q2, k_all[..., :A_HALF], k_all[..., A_HALF:], v_all, pos, kpos, lam)
    o_a = (_rms_norm(o_a, attn_norm_g[l]) * (1.0 - lam_init)).reshape(Bn, T, GROUP_W)
    o_a = o_a * jax.nn.silu(ag.astype(F32))

    o_b, v_state = _spatial_gate(jax.nn.gelu(bu, approximate=False), jax.nn.gelu(bv, approximate=False),
                                 w_s[l], b_s[l], sg_norm_g[l], sg_norm_b[l])
    o_b = o_b * jax.nn.silu(bg.astype(F32))

    o_c, new_buf = _conformer_conv(cin, buf, conv_w[l], conv_b[l], conv_norm_g[l], conv_norm_b[l], w_pw[l])
    o_c = o_c * jax.nn.silu(cg.astype(F32))

    dff = df.astype(F32)
    f = lb + (1.0 - lb) * jax.nn.sigmoid(dff)
    log_f = jnp.log(jnp.maximum(f, F_FLOOR)).reshape(Bn, T, D_HEADS, D_KDIM)
    qd = jax.nn.silu(dq.astype(F32)).reshape(Bn, T, D_HEADS, D_KDIM)
    kd = (1.0 - f).reshape(Bn, T, D_HEADS, D_KDIM)
    vd = di.astype(F32).reshape(Bn, T, D_HEADS, D_VDIM)
    o_d, S_new = _hgrn2(qd, log_f, kd, vd, S0.astype(F32))
    o_d = _rms_norm(o_d, hgrn_norm_g[l]).reshape(Bn, T, GROUP_W) * jax.nn.silu(dg.astype(F32))

    mixed = jnp.concatenate([o_a, o_b, o_c, o_d], -1).astype(dt) @ w_out[l]
    y = _layer_norm(ALPHA * x + gate[:, None, :] * mixed, ln_g[l], ln_b[l]).astype(dt)
    return y, k_rows, v_rows.astype(dt), v_state.astype(dt), new_buf.astype(dt), S_new.astype(dt)


def setup_inputs(seed: int = 0) -> dict:
    key = jax.random.key(seed)
    ks = jax.random.split(key, 28)
    n_pages = PAST_LEN // PAGE_SIZE
    n_used = DEC_BATCH * n_pages
    n_pool = n_used + max(1, n_used // 4)

    def nrm(k, shape, s):
        return s * jax.random.normal(k, shape, F32)

    page_table = jax.random.permutation(ks[0], n_pool)[:n_used].reshape(DEC_BATCH, n_pages).astype(jnp.int32)
    return {
        'x_prompt': nrm(ks[1], (BATCH, SEQ, D_MODEL), 1.0),
        'x_sample': nrm(ks[2], (DEC_BATCH, DEC_SEQ, D_MODEL), 1.0),
        'cache_k': nrm(ks[3], (DEPTH, n_pool, PAGE_SIZE, A_HEADS, 2 * A_HALF), 1.0),
        'cache_v': nrm(ks[4], (DEPTH, n_pool, PAGE_SIZE, A_HEADS, A_VDIM), 1.0),
        'state_conv': nrm(ks[5], (DEPTH, DEC_BATCH, CONV_W - 1, GROUP_W), 0.5),
        'state_hgrn': nrm(ks[6], (DEPTH, DEC_BATCH, D_HEADS, D_KDIM, D_VDIM), 0.3),
        'page_table': page_table,
        'c_prompt': nrm(ks[7], (BATCH, D_MODEL), 1.0),
        'c_sample': nrm(ks[8], (DEC_BATCH, D_MODEL), 1.0),
        'w_ada': nrm(ks[9], (DEPTH, D_MODEL, 3 * D_MODEL), 0.5 * D_MODEL ** -0.5),
        'b_ada': nrm(ks[10], (DEPTH, 3 * D_MODEL), 0.01),
        'w_in': nrm(ks[11], (DEPTH, D_MODEL, D_IN), D_MODEL ** -0.5),
        'lam_qk': nrm(ks[12], (DEPTH, 4, A_HALF), 0.1),
        'attn_norm_g': 1.0 + nrm(ks[13], (DEPTH, A_VDIM), 0.02),
        'sg_norm_g': 1.0 + nrm(ks[14], (DEPTH, GROUP_W), 0.02),
        'sg_norm_b': nrm(ks[15], (DEPTH, GROUP_W), 0.01),
        'w_s': nrm(ks[16], (DEPTH, B_GROUPS, SG_CHUNK, SG_CHUNK), SG_CHUNK ** -0.5),
        'b_s': 1.0 + nrm(ks[17], (DEPTH, B_GROUPS, SG_CHUNK), 0.02),
        'conv_w': nrm(ks[18], (DEPTH, CONV_W, GROUP_W), CONV_W ** -0.5),
        'conv_b': nrm(ks[19], (DEPTH, GROUP_W), 0.01),
        'conv_norm_g': 1.0 + nrm(ks[20], (DEPTH, GROUP_W), 0.02),
        'conv_norm_b': nrm(ks[21], (DEPTH, GROUP_W), 0.01),
        'w_pw': nrm(ks[22], (DEPTH, GROUP_W, GROUP_W), GROUP_W ** -0.5),
        'lower_bounds': nrm(ks[23], (DEPTH, GROUP_W), 0.5),
        'hgrn_norm_g': 1.0 + nrm(ks[24], (DEPTH, D_VDIM), 0.02),
        'w_out': nrm(ks[25], (DEPTH, D_MODEL, D_MODEL), BETA * D_MODEL ** -0.5),
        'ln_g': 1.0 + nrm(ks[26], (DEPTH, D_MODEL), 0.02),
        'ln_b': nrm(ks[27], (DEPTH, D_MODEL), 0.01),
    }


def reference(x_prompt, x_sample, cache_k, cache_v, state_conv, state_hgrn, page_table, c_prompt, c_sample,
              w_ada, b_ada, w_in, lam_qk, attn_norm_g, sg_norm_g, sg_norm_b, w_s, b_s, conv_w, conv_b,
              conv_norm_g, conv_norm_b, w_pw, lower_bounds, hgrn_norm_g, w_out, ln_g, ln_b):
    wts = (w_ada, b_ada, w_in, lam_qk, attn_norm_g, sg_norm_g, sg_norm_b, w_s, b_s, conv_w, conv_b,
           conv_norm_g, conv_norm_b, w_pw, hgrn_norm_g, w_out, ln_g, ln_b)
    lb_soft = jax.nn.softmax(lower_bounds.astype(F32), axis=0)
    lbs = jnp.cumsum(lb_soft, axis=0) - lb_soft[0:1]

    Bp, T = x_prompt.shape[:2]
    Bs, Ts = x_sample.shape[:2]
    past_len = page_table.shape[1] * cache_k.shape[2]
    pos_p = jnp.arange(T)
    pos_s = past_len + jnp.arange(Ts)
    buf0 = jnp.zeros((Bp, CONV_W - 1, GROUP_W), x_prompt.dtype)
    S0 = jnp.zeros((Bp, D_HEADS, D_KDIM, D_VDIM), F32)

    xp, xs = x_prompt, x_sample
    kp_l, vp_l, ks_l, vs_l, ch_l, cbp_l, cbs_l, sp_l, ss_l = [], [], [], [], [], [], [], [], []
    for l in range(DEPTH):
        xp, kp, vp, _, cbp, Sp = _mixer_layer(l, xp, c_prompt, pos_p, None, buf0, S0, lbs[l], wts)
        k_past = cache_k[l][page_table].reshape(Bs, past_len, A_HEADS, 2 * A_HALF)
        v_past = cache_v[l][page_table].reshape(Bs, past_len, A_HEADS, A_VDIM)
        xs, kss, vss, chs, cbs, Ss = _mixer_layer(l, xs, c_sample, pos_s, (k_past, v_past),
                                                 state_conv[l], state_hgrn[l], lbs[l], wts)
        kp_l.append(kp); vp_l.append(vp); cbp_l.append(cbp); sp_l.append(Sp)
        ks_l.append(kss); vs_l.append(vss); ch_l.append(chs); cbs_l.append(cbs); ss_l.append(Ss)

    return (xp, xs, jnp.stack(kp_l), jnp.stack(vp_l), jnp.stack(ks_l), jnp.stack(vs_l), jnp.stack(ch_l),
            jnp.stack(cbp_l), jnp.stack(cbs_l), jnp.stack(sp_l), jnp.stack(ss_l))
```

```python
import functools
import math

import jax
import jax.numpy as jnp
import numpy as np
from jax import lax
from jax.experimental import pallas as pl
from jax.experimental.pallas import tpu as pltpu

F32 = jnp.float32
BF16 = jnp.bfloat16

D_MODEL = 1024
GROUP_W = 256
A_HEADS = 4
A_HALF = 32
A_VDIM = 64
ROPE_THETA = 10000.0
SG_CHUNK = 128
CONV_W = 31
D_HEADS = 4
D_KDIM = 64
F_FLOOR = 1e-30
EPS = 1e-5
NEG_BIG = -1e30
D_IN = 14 * GROUP_W
SQRT_HALF = 0.7071067811865476
Q_SCALE = (A_HALF ** -0.5) * math.log2(math.e)

ROWS = 128
VMEM_LIMIT = 56 * 1024 * 1024

G_CUM, G_SUF, G_CUM8, G_SUF8 = 7, 8, 9, 10

_NT = (((1,), (1,)), ((), ()))
_TN = (((0,), (0,)), ((), ()))


def _segment_matrices():
    t = np.arange(ROWS)[:, None]
    s = np.arange(ROWS)[None, :]
    mats = []
    for lvl in range(7):
        mid = ((t >> (lvl + 1)) << (lvl + 1)) + (1 << lvl)
        second = ((t >> lvl) & 1) == 1
        mats.append(np.where(second, (s >= mid) & (s <= t), (s > t) & (s < mid)))
    mats.append(s <= t)
    mats.append(s > t)
    same8 = (t >> 3) == (s >> 3)
    mats.append(same8 & (s <= t))
    mats.append(same8 & (s > t))
    return jnp.asarray(np.stack(mats).astype(np.float32), dtype=BF16)


def _head_ones():
    h = np.arange(GROUP_W) // 64
    return jnp.asarray((h[:, None] == h[None, :]).astype(np.float32), dtype=BF16)


def _silu(x):
    return x * jax.nn.sigmoid(x)


def _gelu(x):
    return 0.5 * x * (1.0 + lax.erf(x * SQRT_HALF))


def _layer_norm(x, g, b):
    xc = x - jnp.mean(x, axis=-1, keepdims=True)
    var = jnp.mean(xc * xc, axis=-1, keepdims=True)
    return xc * lax.rsqrt(var + EPS) * g + b


def _head_rms(x, e_ref, g):
    ms = jnp.dot((x * x).astype(BF16), e_ref[...], preferred_element_type=F32) * (1.0 / 64.0)
    return x * lax.rsqrt(ms + EPS) * g


def _rope(x, cos, sin):
    outs = []
    for half in range(2):
        xh = x[:, 128 * half:128 * half + 128]
        lane = lax.broadcasted_iota(jnp.int32, xh.shape, 1)
        partner = jnp.where((lane & 16) == 0, pltpu.roll(xh, 112, 1), pltpu.roll(xh, 16, 1))
        outs.append(xh * cos + partner * sin)
    return jnp.concatenate(outs, axis=1)


def _lower_bound(lb_ref, layer):
    lb = lb_ref[...]
    e = jnp.exp(lb - jnp.max(lb, axis=0, keepdims=True))
    soft = e / jnp.sum(e, axis=0, keepdims=True)
    acc = jnp.zeros((1, GROUP_W), F32)
    for i in range(1, layer + 1):
        acc = acc + soft[i:i + 1]
    return acc


def _lam(lam_ref):
    lp = lam_ref[0]
    a = jnp.sum(lp[0:1] * lp[1:2], axis=-1, keepdims=True)
    b = jnp.sum(lp[2:3] * lp[3:4], axis=-1, keepdims=True)
    return jnp.exp(a) - jnp.exp(b)


def _lam_init(layer):
    return 0.8 - 0.6 * math.exp(-0.3 * layer)


def _alpha(depth):
    return (2.0 * depth) ** 0.25


def _sg_mix(vn, ws_ref, bias):
    group = lax.broadcasted_iota(jnp.int32, (1, GROUP_W), 1) >> 6
    r = lax.broadcasted_iota(jnp.int32, (ROWS, ROWS), 0)
    c = lax.broadcasted_iota(jnp.int32, (ROWS, ROWS), 1)
    acc = bias
    for g in range(4):
        wm = jnp.where(r >= c, ws_ref[g], 0.0).astype(BF16)
        vm = jnp.where(group == g, vn, 0.0).astype(BF16)
        acc = acc + jnp.dot(wm, vm, preferred_element_type=F32)
    return acc


def _hgrn_gates(dq, df, di, lb):
    f = lb + (1.0 - lb) * jax.nn.sigmoid(df)
    lf = jnp.log(jnp.maximum(f, F_FLOOR))
    hi = lf.astype(BF16)
    lo = (lf - hi.astype(F32)).astype(BF16)
    return _silu(dq), 1.0 - f, di, jnp.concatenate([hi, lo], axis=1)


def _segsum(g, hilo):
    r = jnp.dot(g, hilo, preferred_element_type=F32)
    return r[:, :GROUP_W] + r[:, GROUP_W:]


def _hgrn_intra(qd, kk, vd, hilo, g_ref, e_ref, levels):
    row = lax.broadcasted_iota(jnp.int32, (ROWS, 1), 0)
    head = lax.broadcasted_iota(jnp.int32, (1, GROUP_W), 1) >> 6
    t_idx = lax.broadcasted_iota(jnp.int32, (4 * ROWS, ROWS), 0) & (ROWS - 1)
    s_idx = lax.broadcasted_iota(jnp.int32, (4 * ROWS, ROWS), 1)
    att = jnp.zeros((4 * ROWS, ROWS), F32)
    for lvl in levels:
        e = jnp.exp(_segsum(g_ref[lvl], hilo))
        second = ((row >> lvl) & 1) == 1
        qe = jnp.where(second, qd * e, 0.0)
        b = jnp.where(second, 0.0, kk * e).astype(BF16)
        a = jnp.concatenate([jnp.where(head == h, qe, 0.0) for h in range(4)], axis=0).astype(BF16)
        s = lax.dot_general(a, b, _NT, preferred_element_type=F32)
        att = att + jnp.where((t_idx >> (lvl + 1)) == (s_idx >> (lvl + 1)), s, 0.0)
    att_cat = jnp.concatenate([att[h * ROWS:(h + 1) * ROWS] for h in range(4)], axis=1).astype(BF16)
    v_stack = jnp.concatenate([jnp.where(head == h, vd, 0.0) for h in range(4)], axis=0).astype(BF16)
    o = jnp.dot(att_cat, v_stack, preferred_element_type=F32)
    diag = jnp.dot((qd * kk).astype(BF16), e_ref[...], preferred_element_type=F32)
    return o + diag * vd


def _ada_kernel(c_ref, w_ref, b_ref, o_ref):
    c = c_ref[...]
    a = _silu(c).astype(BF16)
    o_ref[0] = jnp.dot(a, w_ref[0].astype(BF16), preferred_element_type=F32) + b_ref[0]


def _ada(c_all, w_ada, b_ada):
    depth = w_ada.shape[0]
    rows = c_all.shape[0]
    return pl.pallas_call(
        _ada_kernel,
        out_shape=jax.ShapeDtypeStruct((depth, rows, 3 * D_MODEL), F32),
        grid=(depth, 3),
        in_specs=[
            pl.BlockSpec((rows, D_MODEL), lambda l, n: (0, 0)),
            pl.BlockSpec((1, D_MODEL, D_MODEL), lambda l, n: (l, 0, n)),
            pl.BlockSpec((1, 1, D_MODEL), lambda l, n: (l, 0, n)),
        ],
        out_specs=pl.BlockSpec((1, rows, D_MODEL), lambda l, n: (l, 0, n)),
        compiler_params=pltpu.CompilerParams(dimension_semantics=("arbitrary", "arbitrary"),
                                             vmem_limit_bytes=VMEM_LIMIT),
        name="adaln",
    )(c_all, w_ada, b_ada.reshape(depth, 1, 3 * D_MODEL))


def _prompt_pre_kernel(layer, tb,
                       x_ref, mod_ref, win_ref, cos_ref, sin_ref, sgg_ref, sgb_ref, ws_ref, sgbias_ref,
                       cw_ref, cb_ref, cng_ref, cnb_ref, wpw_ref, lb_ref, hng_ref, g_ref, e_ref,
                       krow_ref, vrow_ref, qt_ref, kb_ref, vt_ref, ga_ref, obcd_ref, cst_ref, hst_ref,
                       hc_ref, s_ref):
    i = pl.program_id(1)

    @pl.when(i == 0)
    def _():
        hc_ref[0:32, :] = jnp.zeros((32, GROUP_W), F32)
        s_ref[...] = jnp.zeros(s_ref.shape, F32)

    mod = mod_ref[0]
    shift = mod[:, 0:D_MODEL]
    scale = mod[:, D_MODEL:2 * D_MODEL]
    h = (x_ref[0] * (1.0 + scale) + shift).astype(BF16)
    z = jnp.dot(h, win_ref[0], preferred_element_type=F32)

    cos = cos_ref[...]
    sin = sin_ref[...]
    q = _rope(z[:, 0:256], cos, sin)
    k = _rope(z[:, 256:512], cos, sin)
    v = z[:, 512:768]
    krow_ref[0] = k
    vrow_ref[0] = v
    kb_ref[0] = k.astype(BF16)
    qt_ref[0] = (q * Q_SCALE).T.astype(BF16)
    vt_ref[0] = v.T.astype(BF16)
    ga_ref[0] = _silu(z[:, 768:1024])

    a = z[:, 1792:2304]
    glu = a[:, :GROUP_W] * jax.nn.sigmoid(a[:, GROUP_W:])
    hc_ref[32:32 + tb, :] = glu
    y = jnp.zeros((tb, GROUP_W), F32)
    for j in range(CONV_W):
        y = y + hc_ref[pl.ds(2 + j, tb), :] * cw_ref[0, j:j + 1, :]
    cst_ref[0] = hc_ref[pl.ds(tb + 2, CONV_W - 1), :]
    hc_ref[0:32, :] = hc_ref[pl.ds(tb, 32), :]
    yn = _silu(_layer_norm(y + cb_ref[0], cng_ref[0], cnb_ref[0]))
    o_c = jnp.dot(yn.astype(BF16), wpw_ref[0], preferred_element_type=F32) * _silu(z[:, 2304:2560])

    lb = _lower_bound(lb_ref, layer)
    o_b_parts, o_d_parts = [], []
    for c in range(tb // ROWS):
        zc = z[c * ROWS:(c + 1) * ROWS]
        vn = _layer_norm(_gelu(zc[:, 1280:1536]), sgg_ref[0], sgb_ref[0])
        mixed = _sg_mix(vn, ws_ref.at[0], sgbias_ref[0])
        o_b_parts.append(_gelu(zc[:, 1024:1280]) * mixed * _silu(zc[:, 1536:1792]))
        qd, kk, vd, hilo = _hgrn_gates(zc[:, 2560:2816], zc[:, 2816:3072], zc[:, 3072:3328], lb)
        o = _hgrn_intra(qd, kk, vd, hilo, g_ref, e_ref, range(7))
        aq = qd * jnp.exp(_segsum(g_ref[G_CUM], hilo))
        bk = kk * jnp.exp(_segsum(g_ref[G_SUF], hilo))
        ones = jnp.ones((ROWS, D_KDIM), BF16)
        inter = []
        for hh in range(D_HEADS):
            lo_, hi_ = 64 * hh, 64 * hh + 64
            st = s_ref[hh]
            inter.append(jnp.dot(aq[:, lo_:hi_].astype(BF16), st.astype(BF16), preferred_element_type=F32))
            dec = (lax.dot_general(hilo[:, lo_:hi_], ones, _TN, preferred_element_type=F32)
                   + lax.dot_general(hilo[:, GROUP_W + lo_:GROUP_W + hi_], ones, _TN, preferred_element_type=F32))
            upd = lax.dot_general(bk[:, lo_:hi_].astype(BF16), vd[:, lo_:hi_].astype(BF16), _TN,
                                  preferred_element_type=F32)
            s_ref[hh] = jnp.exp(dec) * st + upd
        o = o + jnp.concatenate(inter, axis=1)
        o_d_parts.append(_head_rms(o, e_ref, hng_ref[0]) * _silu(zc[:, 3328:3584]))
    o_b = jnp.concatenate(o_b_parts, axis=0)
    o_d = jnp.concatenate(o_d_parts, axis=0)
    obcd_ref[0] = jnp.concatenate([o_b, o_c, o_d], axis=1).astype(BF16)
    hst_ref[0] = s_ref[...]


def _prompt_pre(layer, x, mod_p, win_b, cos, sin, sgg, sgb, ws, sgbias, cw, cb, cng, cnb, wpw_b, lbnd, hng,
                gmat, eones):
    b, t, _ = x.shape
    tb = min(256, t)
    kernel = functools.partial(_prompt_pre_kernel, layer, tb)
    lsel3 = lambda bb, i: (layer, 0, 0)
    const2 = lambda bb, i: (0, 0)
    const3 = lambda bb, i: (0, 0, 0)
    row_blk = lambda w: pl.BlockSpec((1, tb, w), lambda bb, i: (bb, i, 0))
    col_blk = pl.BlockSpec((1, GROUP_W, tb), lambda bb, i: (bb, 0, i))
    out_shape = (
        jax.ShapeDtypeStruct((b, t, GROUP_W), F32),
        jax.ShapeDtypeStruct((b, t, GROUP_W), F32),
        jax.ShapeDtypeStruct((b, GROUP_W, t), BF16),
        jax.ShapeDtypeStruct((b, t, GROUP_W), BF16),
        jax.ShapeDtypeStruct((b, GROUP_W, t), BF16),
        jax.ShapeDtypeStruct((b, t, GROUP_W), F32),
        jax.ShapeDtypeStruct((b, t, 3 * GROUP_W), BF16),
        jax.ShapeDtypeStruct((b, CONV_W - 1, GROUP_W), F32),
        jax.ShapeDtypeStruct((b, D_HEADS, D_KDIM, D_KDIM), F32),
    )
    return pl.pallas_call(
        kernel,
        out_shape=out_shape,
        grid=(b, t // tb),
        in_specs=[
            row_blk(D_MODEL),
            pl.BlockSpec((1, 1, 3 * D_MODEL), lambda bb, i: (bb, 0, 0)),
            pl.BlockSpec((1, D_MODEL, D_IN), lsel3),
            pl.BlockSpec((tb, 128), lambda bb, i: (i, 0)),
            pl.BlockSpec((tb, 128), lambda bb, i: (i, 0)),
            pl.BlockSpec((1, 1, GROUP_W), lsel3),
            pl.BlockSpec((1, 1, GROUP_W), lsel3),
            pl.BlockSpec((1, 4, ROWS, ROWS), lambda bb, i: (layer, 0, 0, 0)),
            pl.BlockSpec((1, ROWS, GROUP_W), lsel3),
            pl.BlockSpec((1, CONV_W, GROUP_W), lsel3),
            pl.BlockSpec((1, 1, GROUP_W), lsel3),
            pl.BlockSpec((1, 1, GROUP_W), lsel3),
            pl.BlockSpec((1, 1, GROUP_W), lsel3),
            pl.BlockSpec((1, GROUP_W, GROUP_W), lsel3),
            pl.BlockSpec(lbnd.shape, const2),
            pl.BlockSpec((1, 1, GROUP_W), lsel3),
            pl.BlockSpec(gmat.shape, const3),
            pl.BlockSpec(eones.shape, const2),
        ],
        out_specs=(
            row_blk(GROUP_W), row_blk(GROUP_W), col_blk, row_blk(GROUP_W), col_blk, row_blk(GROUP_W),
            row_blk(3 * GROUP_W),
            pl.BlockSpec((1, CONV_W - 1, GROUP_W), lambda bb, i: (bb, 0, 0)),
            pl.BlockSpec((1, D_HEADS, D_KDIM, D_KDIM), lambda bb, i: (bb, 0, 0, 0)),
        ),
        scratch_shapes=[pltpu.VMEM((32 + tb, GROUP_W), F32), pltpu.VMEM((D_HEADS, D_KDIM, D_KDIM), F32)],
        compiler_params=pltpu.CompilerParams(dimension_semantics=("arbitrary", "arbitrary"),
                                             vmem_limit_bytes=VMEM_LIMIT),
        name=f"prompt_pre_l{layer}",
    )(x, mod_p, win_b, cos, sin, sgg, sgb, ws, sgbias, cw, cb, cng, cnb, wpw_b, lbnd, hng, gmat, eones)


def _prompt_attn_kernel(layer, depth, tq,
                        qt_ref, kb_ref, vt_ref, ga_ref, obcd_ref, x_ref, mod_ref, wout_ref, lam_ref, ang_ref,
                        lng_ref, lnb_ref, y_ref, acc_ref, m_ref, l_ref):
    i = pl.program_id(1)
    qt = qt_ref[0]
    rowg = lax.broadcasted_iota(jnp.int32, (GROUP_W, 1), 0) >> 5
    qm = [jnp.where(rowg == j, qt, jnp.zeros_like(qt)) for j in range(8)]
    m_ref[...] = jnp.full(m_ref.shape, NEG_BIG, F32)
    l_ref[...] = jnp.zeros(l_ref.shape, F32)
    acc_ref[...] = jnp.zeros(acc_ref.shape, F32)

    def tile(kt, masked):
        off = pl.multiple_of(kt * tq, tq)
        kk = kb_ref[0, pl.ds(off, tq), :]
        if masked:
            key = lax.broadcasted_iota(jnp.int32, (tq, tq), 0)
            qry = lax.broadcasted_iota(jnp.int32, (tq, tq), 1)
            visible = key <= qry
        for j in range(8):
            s = jnp.dot(kk, qm[j], preferred_element_type=F32)
            if masked:
                s = jnp.where(visible, s, NEG_BIG)
            m_old = m_ref[j]
            m_new = jnp.maximum(m_old, jnp.max(s, axis=0, keepdims=True))
            alpha = jnp.exp2(m_old - m_new)
            p = jnp.exp2(s - m_new)
            l_ref[j] = alpha * l_ref[j] + jnp.sum(p, axis=0, keepdims=True)
            hh = j // 2
            vv = vt_ref[0, 64 * hh:64 * hh + 64, pl.ds(off, tq)]
            acc_ref[j] = alpha * acc_ref[j] + jnp.dot(vv, p.astype(BF16), preferred_element_type=F32)
            m_ref[j] = m_new

    def body(kt, carry):
        tile(kt, False)
        return carry

    lax.fori_loop(0, i, body, 0)
    tile(i, True)

    lam = _lam(lam_ref) + _lam_init(layer)
    heads = []
    for hh in range(A_HEADS):
        d = acc_ref[2 * hh] / l_ref[2 * hh] - lam * (acc_ref[2 * hh + 1] / l_ref[2 * hh + 1])
        ms = jnp.mean(d * d, axis=0, keepdims=True)
        heads.append(d * lax.rsqrt(ms + EPS))
    o_a = jnp.concatenate(heads, axis=0).T
    o_a = o_a * (ang_ref[0] * (1.0 - _lam_init(layer))) * ga_ref[0]
    mixed = (jnp.dot(o_a.astype(BF16), wout_ref[0, 0:GROUP_W, :], preferred_element_type=F32)
             + jnp.dot(obcd_ref[0], wout_ref[0, GROUP_W:, :], preferred_element_type=F32))
    gate = mod_ref[0][:, 2 * D_MODEL:]
    y_ref[0] = _layer_norm(_alpha(depth) * x_ref[0] + gate * mixed, lng_ref[0], lnb_ref[0])


def _prompt_attn(layer, depth, qt, kb, vt, ga, obcd, x, mod_p, wout_b, lam_qk, ang, lng, lnb):
    b, t, _ = x.shape
    tq = min(256, t)
    kernel = functools.partial(_prompt_attn_kernel, layer, depth, tq)
    lsel3 = lambda bb, i: (layer, 0, 0)
    row_blk = lambda w: pl.BlockSpec((1, tq, w), lambda bb, i: (bb, i, 0))
    return pl.pallas_call(
        kernel,
        out_shape=jax.ShapeDtypeStruct((b, t, D_MODEL), F32),
        grid=(b, t // tq),
        in_specs=[
            pl.BlockSpec((1, GROUP_W, tq), lambda bb, i: (bb, 0, i)),
            pl.BlockSpec((1, t, GROUP_W), lambda bb, i: (bb, 0, 0)),
            pl.BlockSpec((1, GROUP_W, t), lambda bb, i: (bb, 0, 0)),
            row_blk(GROUP_W),
            row_blk(3 * GROUP_W),
            row_blk(D_MODEL),
            pl.BlockSpec((1, 1, 3 * D_MODEL), lambda bb, i: (bb, 0, 0)),
            pl.BlockSpec((1, D_MODEL, D_MODEL), lsel3),
            pl.BlockSpec((1, 4, A_HALF), lsel3),
            pl.BlockSpec((1, 1, GROUP_W), lsel3),
            pl.BlockSpec((1, 1, D_MODEL), lsel3),
            pl.BlockSpec((1, 1, D_MODEL), lsel3),
        ],
        out_specs=row_blk(D_MODEL),
        scratch_shapes=[pltpu.VMEM((8, A_VDIM, tq), F32), pltpu.VMEM((8, 1, tq), F32), pltpu.VMEM((8, 1, tq), F32)],
        compiler_params=pltpu.CompilerParams(dimension_semantics=("arbitrary", "arbitrary"),
                                             vmem_limit_bytes=VMEM_LIMIT),
        name=f"prompt_attn_l{layer}",
    )(qt, kb, vt, ga, obcd, x, mod_p, wout_b, lam_qk, ang, lng, lnb)


def _sample_pre_kernel(layer, ts,
                       x_ref, mod_ref, win_ref, cos_ref, sin_ref, sgg_ref, sgb_ref, ws_ref, sgbias_ref,
                       cw_ref, cb_ref, cng_ref, cnb_ref, wpw_ref, lb_ref, hng_ref, g_ref, e_ref, cst_ref, hst_ref,
                       krow_ref, vrow_ref, q_ref, ga_ref, obcd_ref, chv_ref, ncst_ref, nhst_ref,
                       hc_ref):
    nseq = ROWS // ts
    mod = mod_ref[...]
    shift = mod[:, :, 0:D_MODEL]
    scale = mod[:, :, D_MODEL:2 * D_MODEL]
    h = (x_ref[...] * (1.0 + scale) + shift).reshape(ROWS, D_MODEL).astype(BF16)
    z = jnp.dot(h, win_ref[0], preferred_element_type=F32)

    cos = cos_ref[...]
    sin = sin_ref[...]
    k = _rope(z[:, 256:512], cos, sin)
    krow_ref[...] = k
    vrow_ref[...] = z[:, 512:768]
    q_ref[...] = _rope(z[:, 0:256], cos, sin) * Q_SCALE
    ga_ref[...] = _silu(z[:, 768:1024])

    vn = _layer_norm(_gelu(z[:, 1280:1536]), sgg_ref[0], sgb_ref[0])
    chv_ref[...] = vn
    o_b = _gelu(z[:, 1024:1280]) * _sg_mix(vn, ws_ref.at[0], sgbias_ref[0]) * _silu(z[:, 1536:1792])

    a = z[:, 1792:2304]
    glu = a[:, :GROUP_W] * jax.nn.sigmoid(a[:, GROUP_W:])
    hc_ref[:, 0:CONV_W - 1, :] = cst_ref[0]
    hc_ref[:, CONV_W - 1:CONV_W - 1 + ts, :] = glu.reshape(nseq, ts, GROUP_W)
    y = jnp.zeros((nseq, ts, GROUP_W), F32)
    for j in range(CONV_W):
        y = y + hc_ref[:, j:j + ts, :] * cw_ref[0, j:j + 1, :]
    ncst_ref[0] = hc_ref[:, ts:ts + CONV_W - 1, :]
    yn = _silu(_layer_norm(y.reshape(ROWS, GROUP_W) + cb_ref[0], cng_ref[0], cnb_ref[0]))
    o_c = jnp.dot(yn.astype(BF16), wpw_ref[0], preferred_element_type=F32) * _silu(z[:, 2304:2560])

    lb = _lower_bound(lb_ref, layer)
    qd, kk, vd, hilo = _hgrn_gates(z[:, 2560:2816], z[:, 2816:3072], z[:, 3072:3328], lb)
    o = _hgrn_intra(qd, kk, vd, hilo, g_ref, e_ref, range(3))
    aq = (qd * jnp.exp(_segsum(g_ref[G_CUM8], hilo))).reshape(nseq, ts, GROUP_W)
    bk = (kk * jnp.exp(_segsum(g_ref[G_SUF8], hilo))).reshape(nseq, ts, GROUP_W)
    v3 = vd.reshape(nseq, ts, GROUP_W)
    hilo3 = hilo.astype(F32).reshape(nseq, ts, 2 * GROUP_W)
    ones = jnp.ones((nseq, ts, D_KDIM), BF16)
    inter = []
    for hh in range(D_HEADS):
        lo_, hi_ = 64 * hh, 64 * hh + 64
        st = hst_ref[0, :, hh]
        inter.append(jnp.einsum('bqk,bkv->bqv', aq[:, :, lo_:hi_].astype(BF16), st.astype(BF16),
                                preferred_element_type=F32))
        dec = (jnp.einsum('bsk,bsv->bkv', hilo3[:, :, lo_:hi_].astype(BF16), ones, preferred_element_type=F32)
               + jnp.einsum('bsk,bsv->bkv', hilo3[:, :, GROUP_W + lo_:GROUP_W + hi_].astype(BF16), ones,
                            preferred_element_type=F32))
        upd = jnp.einsum('bsk,bsv->bkv', bk[:, :, lo_:hi_].astype(BF16), v3[:, :, lo_:hi_].astype(BF16),
                         preferred_element_type=F32)
        nhst_ref[0, :, hh] = jnp.exp(dec) * st + upd
    o = o + jnp.concatenate(inter, axis=2).reshape(ROWS, GROUP_W)
    o_d = _head_rms(o, e_ref, hng_ref[0]) * _silu(z[:, 3328:3584])
    obcd_ref[...] = jnp.concatenate([o_b, o_c, o_d], axis=1).astype(BF16)


def _sample_pre(layer, x, mod_s, win_b, cos, sin, sgg, sgb, ws_blk, sgbias, cw, cb, cng, cnb, wpw_b, lbnd, hng,
                gmat, eones, state_conv, state_hgrn):
    bs, ts, _ = x.shape
    nseq = ROWS // ts
    nblk = bs // nseq
    n = bs * ts
    kernel = functools.partial(_sample_pre_kernel, layer, ts)
    lsel3 = lambda i: (layer, 0, 0)
    const2 = lambda i: (0, 0)
    const3 = lambda i: (0, 0, 0)
    row_blk = lambda w: pl.BlockSpec((ROWS, w), lambda i: (i, 0))
    out_shape = (
        jax.ShapeDtypeStruct((n, GROUP_W), F32),
        jax.ShapeDtypeStruct((n, GROUP_W), F32),
        jax.ShapeDtypeStruct((n, GROUP_W), F32),
        jax.ShapeDtypeStruct((n, GROUP_W), F32),
        jax.ShapeDtypeStruct((n, 3 * GROUP_W), BF16),
        jax.ShapeDtypeStruct((n, GROUP_W), F32),
        jax.ShapeDtypeStruct((1, bs, CONV_W - 1, GROUP_W), F32),
        jax.ShapeDtypeStruct((1, bs, D_HEADS, D_KDIM, D_KDIM), F32),
    )
    return pl.pallas_call(
        kernel,
        out_shape=out_shape,
        grid=(nblk,),
        in_specs=[
            pl.BlockSpec((nseq, ts, D_MODEL), lambda i: (i, 0, 0)),
            pl.BlockSpec((nseq, 1, 3 * D_MODEL), lambda i: (i, 0, 0)),
            pl.BlockSpec((1, D_MODEL, D_IN), lsel3),
            pl.BlockSpec((ROWS, 128), const2),
            pl.BlockSpec((ROWS, 128), const2),
            pl.BlockSpec((1, 1, GROUP_W), lsel3),
            pl.BlockSpec((1, 1, GROUP_W), lsel3),
            pl.BlockSpec((1, 4, ROWS, ROWS), lambda i: (layer, 0, 0, 0)),
            pl.BlockSpec((1, ROWS, GROUP_W), lsel3),
            pl.BlockSpec((1, CONV_W, GROUP_W), lsel3),
            pl.BlockSpec((1, 1, GROUP_W), lsel3),
            pl.BlockSpec((1, 1, GROUP_W), lsel3),
            pl.BlockSpec((1, 1, GROUP_W), lsel3),
            pl.BlockSpec((1, GROUP_W, GROUP_W), lsel3),
            pl.BlockSpec(lbnd.shape, const2),
            pl.BlockSpec((1, 1, GROUP_W), lsel3),
            pl.BlockSpec(gmat.shape, const3),
            pl.BlockSpec(eones.shape, const2),
            pl.BlockSpec((1, nseq, CONV_W - 1, GROUP_W), lambda i: (layer, i, 0, 0)),
            pl.BlockSpec((1, nseq, D_HEADS, D_KDIM, D_KDIM), lambda i: (layer, i, 0, 0, 0)),
        ],
        out_specs=(
            row_blk(GROUP_W), row_blk(GROUP_W), row_blk(GROUP_W), row_blk(GROUP_W), row_blk(3 * GROUP_W),
            row_blk(GROUP_W),
            pl.BlockSpec((1, nseq, CONV_W - 1, GROUP_W), lambda i: (0, i, 0, 0)),
            pl.BlockSpec((1, nseq, D_HEADS, D_KDIM, D_KDIM), lambda i: (0, i, 0, 0, 0)),
        ),
        scratch_shapes=[pltpu.VMEM((nseq, 40, GROUP_W), F32)],
        compiler_params=pltpu.CompilerParams(dimension_semantics=("arbitrary",), vmem_limit_bytes=VMEM_LIMIT),
        name=f"sample_pre_l{layer}",
    )(x, mod_s, win_b, cos, sin, sgg, sgb, ws_blk, sgbias, cw, cb, cng, cnb, wpw_b, lbnd, hng, gmat, eones,
      state_conv, state_hgrn)


def _sample_attn_kernel(layer, n_pages, ts, pt_ref, *refs):
    k_pages = refs[0:n_pages]
    v_pages = refs[n_pages:2 * n_pages]
    q_ref, kn_ref, vn_ref, ga_ref, lam_ref, ang_ref, e_ref, o_ref = refs[2 * n_pages:]
    q = q_ref[0]
    grp = lax.broadcasted_iota(jnp.int32, (1, GROUP_W), 1) >> 5
    qexp = jnp.concatenate([jnp.where(grp == j, q, 0.0) for j in range(8)], axis=0).astype(BF16)
    k_all = jnp.concatenate([r[...] for r in k_pages], axis=0).astype(BF16)
    v_all = jnp.concatenate([r[...] for r in v_pages], axis=0).astype(BF16)
    s_past = lax.dot_general(qexp, k_all, _NT, preferred_element_type=F32)
    s_new = lax.dot_general(qexp, kn_ref[0].astype(BF16), _NT, preferred_element_type=F32)
    t_q = lax.broadcasted_iota(jnp.int32, (8 * ts, ts), 0) & (ts - 1)
    t_k = lax.broadcasted_iota(jnp.int32, (8 * ts, ts), 1)
    vis = t_k <= t_q
    s_new = jnp.where(vis, s_new, NEG_BIG)
    m = jnp.maximum(jnp.max(s_past, axis=-1, keepdims=True), jnp.max(s_new, axis=-1, keepdims=True))
    p_past = jnp.exp2(s_past - m)
    p_new = jnp.where(vis, jnp.exp2(s_new - m), 0.0)
    l = jnp.sum(p_past, axis=-1, keepdims=True) + jnp.sum(p_new, axis=-1, keepdims=True)
    o = (jnp.dot(p_past.astype(BF16), v_all, preferred_element_type=F32)
         + jnp.dot(p_new.astype(BF16), vn_ref[0].astype(BF16), preferred_element_type=F32)) / l
    lam = _lam(lam_ref) + _lam_init(layer)
    head = lax.broadcasted_iota(jnp.int32, (1, GROUP_W), 1) >> 6
    o_a = jnp.zeros((ts, GROUP_W), F32)
    for hh in range(A_HEADS):
        d = o[2 * hh * ts:(2 * hh + 1) * ts] - lam * o[(2 * hh + 1) * ts:(2 * hh + 2) * ts]
        o_a = o_a + jnp.where(head == hh, d, 0.0)
    o_a = _head_rms(o_a, e_ref, ang_ref[0] * (1.0 - _lam_init(layer))) * ga_ref[0]
    o_ref[0] = o_a.astype(BF16)


def _sample_attn(layer, page_table, cache_k, cache_v, q_s, k_new, v_new, ga_s, lam_qk, ang, eones, ts):
    bs, n_pages = page_table.shape
    page = cache_k.shape[2]
    kernel = functools.partial(_sample_attn_kernel, layer, n_pages, ts)

    def page_spec(j):
        return pl.BlockSpec((None, None, page, GROUP_W), lambda b, pt: (layer, pt[b, j], 0, 0))

    seq_blk = pl.BlockSpec((1, ts, GROUP_W), lambda b, pt: (b, 0, 0))
    lsel3 = lambda b, pt: (layer, 0, 0)
    grid_spec = pltpu.PrefetchScalarGridSpec(
        num_scalar_prefetch=1,
        grid=(bs,),
        in_specs=([page_spec(j) for j in range(n_pages)] + [page_spec(j) for j in range(n_pages)]
                  + [seq_blk, seq_blk, seq_blk, seq_blk,
                     pl.BlockSpec((1, 4, A_HALF), lsel3),
                     pl.BlockSpec((1, 1, GROUP_W), lsel3),
                     pl.BlockSpec(eones.shape, lambda b, pt: (0, 0))]),
        out_specs=seq_blk,
    )
    shp3 = (bs, ts, GROUP_W)
    return pl.pallas_call(
        kernel,
        out_shape=jax.ShapeDtypeStruct(shp3, BF16),
        grid_spec=grid_spec,
        compiler_params=pltpu.CompilerParams(dimension_semantics=("arbitrary",), vmem_limit_bytes=VMEM_LIMIT),
        name=f"sample_attn_l{layer}",
    )(page_table, *([cache_k] * n_pages), *([cache_v] * n_pages),
      q_s.reshape(shp3), k_new.reshape(shp3), v_new.reshape(shp3), ga_s.reshape(shp3), lam_qk, ang, eones)


def _sample_out_kernel(depth, ts, oa_ref, obcd_ref, x_ref, mod_ref, wout_ref, lng_ref, lnb_ref, y_ref):
    nseq = ROWS // ts
    mixed = (jnp.dot(oa_ref[...], wout_ref[0, 0:GROUP_W, :], preferred_element_type=F32)
             + jnp.dot(obcd_ref[...], wout_ref[0, GROUP_W:, :], preferred_element_type=F32))
    gate = mod_ref[...][:, :, 2 * D_MODEL:]
    y = _alpha(depth) * x_ref[...] + gate * mixed.reshape(nseq, ts, D_MODEL)
    y_ref[...] = _layer_norm(y, lng_ref[0], lnb_ref[0])


def _sample_out(layer, depth, oa, obcd, x, mod_s, wout_b, lng, lnb):
    bs, ts, _ = x.shape
    nseq = ROWS // ts
    lsel3 = lambda i: (layer, 0, 0)
    return pl.pallas_call(
        functools.partial(_sample_out_kernel, depth, ts),
        out_shape=jax.ShapeDtypeStruct(x.shape, F32),
        grid=(bs // nseq,),
        in_specs=[
            pl.BlockSpec((ROWS, GROUP_W), lambda i: (i, 0)),
            pl.BlockSpec((ROWS, 3 * GROUP_W), lambda i: (i, 0)),
            pl.BlockSpec((nseq, ts, D_MODEL), lambda i: (i, 0, 0)),
            pl.BlockSpec((nseq, 1, 3 * D_MODEL), lambda i: (i, 0, 0)),
            pl.BlockSpec((1, D_MODEL, D_MODEL), lsel3),
            pl.BlockSpec((1, 1, D_MODEL), lsel3),
            pl.BlockSpec((1, 1, D_MODEL), lsel3),
        ],
        out_specs=pl.BlockSpec((nseq, ts, D_MODEL), lambda i: (i, 0, 0)),
        compiler_params=pltpu.CompilerParams(dimension_semantics=("arbitrary",), vmem_limit_bytes=VMEM_LIMIT),
        name=f"sample_out_l{layer}",
    )(oa, obcd, x, mod_s, wout_b, lng, lnb)


def _rope_tables(pos):
    half = A_HALF // 2
    inv = ROPE_THETA ** (-jnp.arange(half, dtype=F32) * 2.0 / A_HALF)
    ang = pos.astype(F32)[:, None] * inv[None, :]
    cos = jnp.cos(ang)
    sin = jnp.sin(ang)
    return jnp.tile(jnp.concatenate([cos, cos], -1), (1, 4)), jnp.tile(jnp.concatenate([-sin, sin], -1), (1, 4))


def kernel(x_prompt, x_sample, cache_k, cache_v, state_conv, state_hgrn, page_table, c_prompt, c_sample, w_ada, b_ada, w_in, lam_qk, attn_norm_g, sg_norm_g, sg_norm_b, w_s, b_s, conv_w, conv_b, conv_norm_g, conv_norm_b, w_pw, lower_bounds, hgrn_norm_g, w_out, ln_g, ln_b):
    depth = w_in.shape[0]
    bp, t, _ = x_prompt.shape
    bs, ts, _ = x_sample.shape
    n_pool, page = cache_k.shape[1], cache_k.shape[2]
    past_len = page_table.shape[1] * page
    assert ts == 8 and ROWS % ts == 0 and bs % (ROWS // ts) == 0 and t % ROWS == 0
    nseq = ROWS // ts

    gmat = _segment_matrices()
    eones = _head_ones()
    win_b = w_in.astype(BF16)
    wout_b = w_out.astype(BF16)
    wpw_b = w_pw.astype(BF16)
    row3 = lambda a: a.reshape(depth, 1, a.shape[-1])
    sgg, sgb, cb, cng, cnb = row3(sg_norm_g), row3(sg_norm_b), row3(conv_b), row3(conv_norm_g), row3(conv_norm_b)
    lng, lnb = row3(ln_g), row3(ln_b)
    ang = row3(jnp.tile(attn_norm_g, (1, A_HEADS)))
    hng = row3(jnp.tile(hgrn_norm_g, (1, D_HEADS)))
    sgbias_p = jnp.repeat(jnp.swapaxes(b_s, 1, 2), GROUP_W // 4, axis=2)
    sgbias_s = jnp.tile(sgbias_p[:, :ts], (1, nseq, 1))
    eye = jnp.eye(nseq, dtype=F32)
    ws_blk = jnp.einsum('ab,lgts->lgatbs', eye, w_s[:, :, :ts, :ts]).reshape(depth, 4, ROWS, ROWS)
    cos_p, sin_p = _rope_tables(jnp.arange(t))
    cos_s, sin_s = _rope_tables(past_len + jnp.arange(ts))
    cos_s, sin_s = jnp.tile(cos_s, (nseq, 1)), jnp.tile(sin_s, (nseq, 1))
    ck = cache_k.reshape(depth, n_pool, page, GROUP_W)
    cv = cache_v.reshape(depth, n_pool, page, GROUP_W)

    rows = bp + bs
    pad = (-rows) % 8
    c_all = jnp.concatenate([c_prompt, c_sample, jnp.zeros((pad, D_MODEL), F32)], axis=0)
    mod = _ada(c_all, w_ada, b_ada)

    xp, xs = x_prompt, x_sample
    outs = [[] for _ in range(9)]
    for l in range(depth):
        mod_p = mod[l, :bp].reshape(bp, 1, 3 * D_MODEL)
        mod_s = mod[l, bp:bp + bs].reshape(bs, 1, 3 * D_MODEL)
        krow, vrow, qt, kb, vt, ga, obcd, cst_p, hst_p = _prompt_pre(
            l, xp, mod_p, win_b, cos_p, sin_p, sgg, sgb, w_s, sgbias_p, conv_w, cb, cng, cnb, wpw_b, lower_bounds,
            hng, gmat, eones)
        xp = _prompt_attn(l, depth, qt, kb, vt, ga, obcd, xp, mod_p, wout_b, lam_qk, ang, lng, lnb)

        krow_s, vrow_s, q_s, ga_s, obcd_s, chv_s, cst_s, hst_s = _sample_pre(
            l, xs, mod_s, win_b, cos_s, sin_s, sgg, sgb, ws_blk, sgbias_s, conv_w, cb, cng, cnb, wpw_b,
            lower_bounds, hng, gmat, eones, state_conv, state_hgrn)
        oa_s = _sample_attn(l, page_table, ck, cv, q_s, krow_s, vrow_s, ga_s, lam_qk, ang, eones, ts)
        xs = _sample_out(l, depth, oa_s.reshape(bs * ts, GROUP_W), obcd_s, xs, mod_s, wout_b, lng, lnb)

        outs[0].append(krow.reshape(bp, t, A_HEADS, 2 * A_HALF))
        outs[1].append(vrow.reshape(bp, t, A_HEADS, A_VDIM))
        outs[2].append(krow_s.reshape(bs, ts, A_HEADS, 2 * A_HALF))
        outs[3].append(vrow_s.reshape(bs, ts, A_HEADS, A_VDIM))
        outs[4].append(chv_s.reshape(bs, ts, GROUP_W))
        outs[5].append(cst_p)
        outs[6].append(cst_s[0])
        outs[7].append(hst_p)
        outs[8].append(hst_s[0])
    return (xp, xs) + tuple(jnp.stack(o) for o in outs)
```

```python
import functools
import math

import jax
import jax.numpy as jnp
import numpy as np
from jax import lax
from jax.experimental import pallas as pl
from jax.experimental.pallas import tpu as pltpu

F32 = jnp.float32
BF16 = jnp.bfloat16

D_MODEL = 1024
GROUP_W = 256
A_HEADS = 4
A_HALF = 32
A_VDIM = 64
ROPE_THETA = 10000.0
SG_CHUNK = 128
CONV_W = 31
D_HEADS = 4
D_KDIM = 64
F_FLOOR = 1e-30
EPS = 1e-5
NEG_BIG = -1e30
D_IN = 14 * GROUP_W
SQRT_HALF = 0.7071067811865476
Q_SCALE = (A_HALF ** -0.5) * math.log2(math.e)

ROWS = 128
VMEM_LIMIT = 56 * 1024 * 1024

G_CUM, G_SUF, G_CUM8, G_SUF8 = 7, 8, 9, 10

_NT = (((1,), (1,)), ((), ()))
_TN = (((0,), (0,)), ((), ()))


def _segment_matrices():
    t = np.arange(ROWS)[:, None]
    s = np.arange(ROWS)[None, :]
    mats = []
    for lvl in range(7):
        mid = ((t >> (lvl + 1)) << (lvl + 1)) + (1 << lvl)
        second = ((t >> lvl) & 1) == 1
        mats.append(np.where(second, (s >= mid) & (s <= t), (s > t) & (s < mid)))
    mats.append(s <= t)
    mats.append(s > t)
    same8 = (t >> 3) == (s >> 3)
    mats.append(same8 & (s <= t))
    mats.append(same8 & (s > t))
    return jnp.asarray(np.stack(mats).astype(np.float32), dtype=BF16)


def _head_ones():
    h = np.arange(GROUP_W) // 64
    return jnp.asarray((h[:, None] == h[None, :]).astype(np.float32), dtype=BF16)


def _silu(x):
    return x * jax.nn.sigmoid(x)


def _gelu(x):
    return 0.5 * x * (1.0 + lax.erf(x * SQRT_HALF))


def _layer_norm(x, g, b):
    xc = x - jnp.mean(x, axis=-1, keepdims=True)
    var = jnp.mean(xc * xc, axis=-1, keepdims=True)
    return xc * lax.rsqrt(var + EPS) * g + b


def _head_rms(x, e_ref, g):
    ms = jnp.dot((x * x).astype(BF16), e_ref[...], preferred_element_type=F32) * (1.0 / 64.0)
    return x * lax.rsqrt(ms + EPS) * g


def _rope(x, cos, sin):
    outs = []
    for half in range(2):
        xh = x[:, 128 * half:128 * half + 128]
        lane = lax.broadcasted_iota(jnp.int32, xh.shape, 1)
        partner = jnp.where((lane & 16) == 0, pltpu.roll(xh, 112, 1), pltpu.roll(xh, 16, 1))
        outs.append(xh * cos + partner * sin)
    return jnp.concatenate(outs, axis=1)


def _lower_bound(lb_ref, layer):
    lb = lb_ref[...]
    e = jnp.exp(lb - jnp.max(lb, axis=0, keepdims=True))
    soft = e / jnp.sum(e, axis=0, keepdims=True)
    acc = jnp.zeros((1, GROUP_W), F32)
    for i in range(1, layer + 1):
        acc = acc + soft[i:i + 1]
    return acc


def _lam(lam_ref):
    lp = lam_ref[0]
    a = jnp.sum(lp[0:1] * lp[1:2], axis=-1, keepdims=True)
    b = jnp.sum(lp[2:3] * lp[3:4], axis=-1, keepdims=True)
    return jnp.exp(a) - jnp.exp(b)


def _lam_init(layer):
    return 0.8 - 0.6 * math.exp(-0.3 * layer)


def _alpha(depth):
    return (2.0 * depth) ** 0.25


def _sg_mix(vn, ws_ref, bias):
    group = lax.broadcasted_iota(jnp.int32, (1, GROUP_W), 1) >> 6
    r = lax.broadcasted_iota(jnp.int32, (ROWS, ROWS), 0)
    c = lax.broadcasted_iota(jnp.int32, (ROWS, ROWS), 1)
    acc = bias
    for g in range(4):
        wm = jnp.where(r >= c, ws_ref[g], 0.0).astype(BF16)
        vm = jnp.where(group == g, vn, 0.0).astype(BF16)
        acc = acc + jnp.dot(wm, vm, preferred_element_type=F32)
    return acc


def _hgrn_gates(dq, df, di, lb):
    f = lb + (1.0 - lb) * jax.nn.sigmoid(df)
    lf = jnp.log(jnp.maximum(f, F_FLOOR))
    hi = lf.astype(BF16)
    lo = (lf - hi.astype(F32)).astype(BF16)
    return _silu(dq), 1.0 - f, di, jnp.concatenate([hi, lo], axis=1)


def _segsum(g, hilo):
    r = jnp.dot(g, hilo, preferred_element_type=F32)
    return r[:, :GROUP_W] + r[:, GROUP_W:]


def _hgrn_intra(qd, kk, vd, hilo, g_ref, e_ref, levels):
    row = lax.broadcasted_iota(jnp.int32, (ROWS, 1), 0)
    head = lax.broadcasted_iota(jnp.int32, (1, GROUP_W), 1) >> 6
    t_idx = lax.broadcasted_iota(jnp.int32, (4 * ROWS, ROWS), 0) & (ROWS - 1)
    s_idx = lax.broadcasted_iota(jnp.int32, (4 * ROWS, ROWS), 1)
    att = jnp.zeros((4 * ROWS, ROWS), F32)
    for lvl in levels:
        e = jnp.exp(_segsum(g_ref[lvl], hilo))
        second = ((row >> lvl) & 1) == 1
        qe = jnp.where(second, qd * e, 0.0)
        b = jnp.where(second, 0.0, kk * e).astype(BF16)
        a = jnp.concatenate([jnp.where(head == h, qe, 0.0) for h in range(4)], axis=0).astype(BF16)
        s = lax.dot_general(a, b, _NT, preferred_element_type=F32)
        att = att + jnp.where((t_idx >> (lvl + 1)) == (s_idx >> (lvl + 1)), s, 0.0)
    att_cat = jnp.concatenate([att[h * ROWS:(h + 1) * ROWS] for h in range(4)], axis=1).astype(BF16)
    v_stack = jnp.concatenate([jnp.where(head == h, vd, 0.0) for h in range(4)], axis=0).astype(BF16)
    o = jnp.dot(att_cat, v_stack, preferred_element_type=F32)
    diag = jnp.dot((qd * kk).astype(BF16), e_ref[...], preferred_element_type=F32)
    return o + diag * vd


def _ada_kernel(c_ref, w_ref, b_ref, o_ref):
    c = c_ref[...]
    a = _silu(c).astype(BF16)
    o_ref[0] = jnp.dot(a, w_ref[0].astype(BF16), preferred_element_type=F32) + b_ref[0]


def _ada(c_all, w_ada, b_ada):
    depth = w_ada.shape[0]
    rows = c_all.shape[0]
    return pl.pallas_call(
        _ada_kernel,
        out_shape=jax.ShapeDtypeStruct((depth, rows, 3 * D_MODEL), F32),
        grid=(depth, 3),
        in_specs=[
            pl.BlockSpec((rows, D_MODEL), lambda l, n: (0, 0)),
            pl.BlockSpec((1, D_MODEL, D_MODEL), lambda l, n: (l, 0, n)),
            pl.BlockSpec((1, 1, D_MODEL), lambda l, n: (l, 0, n)),
        ],
        out_specs=pl.BlockSpec((1, rows, D_MODEL), lambda l, n: (l, 0, n)),
        compiler_params=pltpu.CompilerParams(dimension_semantics=("arbitrary", "arbitrary"),
                                             vmem_limit_bytes=VMEM_LIMIT),
        name="adaln",
    )(c_all, w_ada, b_ada.reshape(depth, 1, 3 * D_MODEL))


def _prompt_pre_kernel(layer, tb,
                       x_ref, mod_ref, win_ref, cos_ref, sin_ref, sgg_ref, sgb_ref, ws_ref, sgbias_ref,
                       cw_ref, cb_ref, cng_ref, cnb_ref, wpw_ref, lb_ref, hng_ref, g_ref, e_ref,
                       krow_ref, vrow_ref, qt_ref, kb_ref, vt_ref, ga_ref, obcd_ref, cst_ref, hst_ref,
                       hc_ref, s_ref):
    i = pl.program_id(1)

    @pl.when(i == 0)
    def _():
        hc_ref[0:32, :] = jnp.zeros((32, GROUP_W), F32)
        s_ref[...] = jnp.zeros(s_ref.shape, F32)

    mod = mod_ref[0]
    shift = mod[:, 0:D_MODEL]
    scale = mod[:, D_MODEL:2 * D_MODEL]
    h = (x_ref[0] * (1.0 + scale) + shift).astype(BF16)
    z = jnp.dot(h, win_ref[0], preferred_element_type=F32)

    cos = cos_ref[...]
    sin = sin_ref[...]
    q = _rope(z[:, 0:256], cos, sin)
    k = _rope(z[:, 256:512], cos, sin)
    v = z[:, 512:768]
    kt = k.T
    vt = v.T
    krow_ref[0] = kt
    vrow_ref[0] = vt
    kb_ref[0] = k.astype(BF16)
    qt_ref[0] = (q * Q_SCALE).T.astype(BF16)
    vt_ref[0] = vt.astype(BF16)
    ga_ref[0] = _silu(z[:, 768:1024])

    a = z[:, 1792:2304]
    glu = a[:, :GROUP_W] * jax.nn.sigmoid(a[:, GROUP_W:])
    hc_ref[32:32 + tb, :] = glu
    y = jnp.zeros((tb, GROUP_W), F32)
    for j in range(CONV_W):
        y = y + hc_ref[pl.ds(2 + j, tb), :] * cw_ref[0, j:j + 1, :]
    cst_ref[0] = hc_ref[pl.ds(tb + 2, CONV_W - 1), :]
    hc_ref[0:32, :] = hc_ref[pl.ds(tb, 32), :]
    yn = _silu(_layer_norm(y + cb_ref[0], cng_ref[0], cnb_ref[0]))
    o_c = jnp.dot(yn.astype(BF16), wpw_ref[0], preferred_element_type=F32) * _silu(z[:, 2304:2560])

    lb = _lower_bound(lb_ref, layer)
    o_b_parts, o_d_parts = [], []
    for c in range(tb // ROWS):
        zc = z[c * ROWS:(c + 1) * ROWS]
        vn = _layer_norm(_gelu(zc[:, 1280:1536]), sgg_ref[0], sgb_ref[0])
        mixed = _sg_mix(vn, ws_ref.at[0], sgbias_ref[0])
        o_b_parts.append(_gelu(zc[:, 1024:1280]) * mixed * _silu(zc[:, 1536:1792]))
        qd, kk, vd, hilo = _hgrn_gates(zc[:, 2560:2816], zc[:, 2816:3072], zc[:, 3072:3328], lb)
        o = _hgrn_intra(qd, kk, vd, hilo, g_ref, e_ref, range(7))
        aq = qd * jnp.exp(_segsum(g_ref[G_CUM], hilo))
        bk = kk * jnp.exp(_segsum(g_ref[G_SUF], hilo))
        ones = jnp.ones((ROWS, D_KDIM), BF16)
        inter = []
        for hh in range(D_HEADS):
            lo_, hi_ = 64 * hh, 64 * hh + 64
            st = s_ref[hh]
            inter.append(jnp.dot(aq[:, lo_:hi_].astype(BF16), st.astype(BF16), preferred_element_type=F32))
            dec = (lax.dot_general(hilo[:, lo_:hi_], ones, _TN, preferred_element_type=F32)
                   + lax.dot_general(hilo[:, GROUP_W + lo_:GROUP_W + hi_], ones, _TN, preferred_element_type=F32))
            upd = lax.dot_general(bk[:, lo_:hi_].astype(BF16), vd[:, lo_:hi_].astype(BF16), _TN,
                                  preferred_element_type=F32)
            s_ref[hh] = jnp.exp(dec) * st + upd
        o = o + jnp.concatenate(inter, axis=1)
        o_d_parts.append(_head_rms(o, e_ref, hng_ref[0]) * _silu(zc[:, 3328:3584]))
    o_b = jnp.concatenate(o_b_parts, axis=0)
    o_d = jnp.concatenate(o_d_parts, axis=0)
    obcd_ref[0] = jnp.concatenate([o_b, o_c, o_d], axis=1).astype(BF16)
    hst_ref[0] = s_ref[...]


def _prompt_pre(layer, x, mod_p, win_b, cos, sin, sgg, sgb, ws, sgbias, cw, cb, cng, cnb, wpw_b, lbnd, hng,
                gmat, eones):
    b, t, _ = x.shape
    tb = min(256, t)
    kernel = functools.partial(_prompt_pre_kernel, layer, tb)
    lsel3 = lambda bb, i: (layer, 0, 0)
    const2 = lambda bb, i: (0, 0)
    const3 = lambda bb, i: (0, 0, 0)
    row_blk = lambda w: pl.BlockSpec((1, tb, w), lambda bb, i: (bb, i, 0))
    col_blk = pl.BlockSpec((1, GROUP_W, tb), lambda bb, i: (bb, 0, i))
    out_shape = (
        jax.ShapeDtypeStruct((b, GROUP_W, t), F32),
        jax.ShapeDtypeStruct((b, GROUP_W, t), F32),
        jax.ShapeDtypeStruct((b, GROUP_W, t), BF16),
        jax.ShapeDtypeStruct((b, t, GROUP_W), BF16),
        jax.ShapeDtypeStruct((b, GROUP_W, t), BF16),
        jax.ShapeDtypeStruct((b, t, GROUP_W), F32),
        jax.ShapeDtypeStruct((b, t, 3 * GROUP_W), BF16),
        jax.ShapeDtypeStruct((b, CONV_W - 1, GROUP_W), F32),
        jax.ShapeDtypeStruct((b, D_HEADS, D_KDIM, D_KDIM), F32),
    )
    return pl.pallas_call(
        kernel,
        out_shape=out_shape,
        grid=(b, t // tb),
        in_specs=[
            row_blk(D_MODEL),
            pl.BlockSpec((1, 1, 3 * D_MODEL), lambda bb, i: (bb, 0, 0)),
            pl.BlockSpec((1, D_MODEL, D_IN), lsel3),
            pl.BlockSpec((tb, 128), lambda bb, i: (i, 0)),
            pl.BlockSpec((tb, 128), lambda bb, i: (i, 0)),
            pl.BlockSpec((1, 1, GROUP_W), lsel3),
            pl.BlockSpec((1, 1, GROUP_W), lsel3),
            pl.BlockSpec((1, 4, ROWS, ROWS), lambda bb, i: (layer, 0, 0, 0)),
            pl.BlockSpec((1, ROWS, GROUP_W), lsel3),
            pl.BlockSpec((1, CONV_W, GROUP_W), lsel3),
            pl.BlockSpec((1, 1, GROUP_W), lsel3),
            pl.BlockSpec((1, 1, GROUP_W), lsel3),
            pl.BlockSpec((1, 1, GROUP_W), lsel3),
            pl.BlockSpec((1, GROUP_W, GROUP_W), lsel3),
            pl.BlockSpec(lbnd.shape, const2),
            pl.BlockSpec((1, 1, GROUP_W), lsel3),
            pl.BlockSpec(gmat.shape, const3),
            pl.BlockSpec(eones.shape, const2),
        ],
        out_specs=(
            col_blk, col_blk, col_blk, row_blk(GROUP_W), col_blk, row_blk(GROUP_W),
            row_blk(3 * GROUP_W),
            pl.BlockSpec((1, CONV_W - 1, GROUP_W), lambda bb, i: (bb, 0, 0)),
            pl.BlockSpec((1, D_HEADS, D_KDIM, D_KDIM), lambda bb, i: (bb, 0, 0, 0)),
        ),
        scratch_shapes=[pltpu.VMEM((32 + tb, GROUP_W), F32), pltpu.VMEM((D_HEADS, D_KDIM, D_KDIM), F32)],
        compiler_params=pltpu.CompilerParams(dimension_semantics=("arbitrary", "arbitrary"),
                                             vmem_limit_bytes=VMEM_LIMIT),
        name=f"prompt_pre_l{layer}",
    )(x, mod_p, win_b, cos, sin, sgg, sgb, ws, sgbias, cw, cb, cng, cnb, wpw_b, lbnd, hng, gmat, eones)


def _prompt_attn_kernel(layer, depth, tq,
                        qt_ref, kb_ref, vt_ref, ga_ref, obcd_ref, x_ref, mod_ref, wout_ref, lam_ref, ang_ref,
                        lng_ref, lnb_ref, y_ref, acc_ref, m_ref, l_ref):
    i = pl.program_id(1)
    qt = qt_ref[0]
    rowg = lax.broadcasted_iota(jnp.int32, (GROUP_W, 1), 0) >> 5
    qm = [jnp.where(rowg == j, qt, jnp.zeros_like(qt)) for j in range(8)]
    m_ref[...] = jnp.full(m_ref.shape, NEG_BIG, F32)
    l_ref[...] = jnp.zeros(l_ref.shape, F32)
    acc_ref[...] = jnp.zeros(acc_ref.shape, F32)

    def tile(kt, masked):
        off = pl.multiple_of(kt * tq, tq)
        kk = kb_ref[0, pl.ds(off, tq), :]
        if masked:
            key = lax.broadcasted_iota(jnp.int32, (tq, tq), 0)
            qry = lax.broadcasted_iota(jnp.int32, (tq, tq), 1)
            visible = key <= qry
        for j in range(8):
            s = jnp.dot(kk, qm[j], preferred_element_type=F32)
            if masked:
                s = jnp.where(visible, s, NEG_BIG)
            m_old = m_ref[j]
            m_new = jnp.maximum(m_old, jnp.max(s, axis=0, keepdims=True))
            alpha = jnp.exp2(m_old - m_new)
            p = jnp.exp2(s - m_new)
            l_ref[j] = alpha * l_ref[j] + jnp.sum(p, axis=0, keepdims=True)
            hh = j // 2
            vv = vt_ref[0, 64 * hh:64 * hh + 64, pl.ds(off, tq)]
            acc_ref[j] = alpha * acc_ref[j] + jnp.dot(vv, p.astype(BF16), preferred_element_type=F32)
            m_ref[j] = m_new

    def body(kt, carry):
        tile(kt, False)
        return carry

    lax.fori_loop(0, i, body, 0)
    tile(i, True)

    lam = _lam(lam_ref) + _lam_init(layer)
    heads = []
    for hh in range(A_HEADS):
        d = acc_ref[2 * hh] / l_ref[2 * hh] - lam * (acc_ref[2 * hh + 1] / l_ref[2 * hh + 1])
        ms = jnp.mean(d * d, axis=0, keepdims=True)
        heads.append(d * lax.rsqrt(ms + EPS))
    o_a = jnp.concatenate(heads, axis=0).T
    o_a = o_a * (ang_ref[0] * (1.0 - _lam_init(layer))) * ga_ref[0]
    mixed = (jnp.dot(o_a.astype(BF16), wout_ref[0, 0:GROUP_W, :], preferred_element_type=F32)
             + jnp.dot(obcd_ref[0], wout_ref[0, GROUP_W:, :], preferred_element_type=F32))
    gate = mod_ref[0][:, 2 * D_MODEL:]
    y_ref[0] = _layer_norm(_alpha(depth) * x_ref[0] + gate * mixed, lng_ref[0], lnb_ref[0])


def _prompt_attn(layer, depth, qt, kb, vt, ga, obcd, x, mod_p, wout_b, lam_qk, ang, lng, lnb):
    b, t, _ = x.shape
    tq = min(256, t)
    kernel = functools.partial(_prompt_attn_kernel, layer, depth, tq)
    lsel3 = lambda bb, i: (layer, 0, 0)
    row_blk = lambda w: pl.BlockSpec((1, tq, w), lambda bb, i: (bb, i, 0))
    return pl.pallas_call(
        kernel,
        out_shape=jax.ShapeDtypeStruct((b, t, D_MODEL), F32),
        grid=(b, t // tq),
        in_specs=[
            pl.BlockSpec((1, GROUP_W, tq), lambda bb, i: (bb, 0, i)),
            pl.BlockSpec((1, t, GROUP_W), lambda bb, i: (bb, 0, 0)),
            pl.BlockSpec((1, GROUP_W, t), lambda bb, i: (bb, 0, 0)),
            row_blk(GROUP_W),
            row_blk(3 * GROUP_W),
            row_blk(D_MODEL),
            pl.BlockSpec((1, 1, 3 * D_MODEL), lambda bb, i: (bb, 0, 0)),
            pl.BlockSpec((1, D_MODEL, D_MODEL), lsel3),
            pl.BlockSpec((1, 4, A_HALF), lsel3),
            pl.BlockSpec((1, 1, GROUP_W), lsel3),
            pl.BlockSpec((1, 1, D_MODEL), lsel3),
            pl.BlockSpec((1, 1, D_MODEL), lsel3),
        ],
        out_specs=row_blk(D_MODEL),
        scratch_shapes=[pltpu.VMEM((8, A_VDIM, tq), F32), pltpu.VMEM((8, 1, tq), F32), pltpu.VMEM((8, 1, tq), F32)],
        compiler_params=pltpu.CompilerParams(dimension_semantics=("arbitrary", "arbitrary"),
                                             vmem_limit_bytes=VMEM_LIMIT),
        name=f"prompt_attn_l{layer}",
    )(qt, kb, vt, ga, obcd, x, mod_p, wout_b, lam_qk, ang, lng, lnb)


def _sample_pre_kernel(layer, ts,
                       x_ref, mod_ref, win_ref, cos_ref, sin_ref, sgg_ref, sgb_ref, ws_ref, sgbias_ref,
                       cw_ref, cb_ref, cng_ref, cnb_ref, wpw_ref, lb_ref, hng_ref, g_ref, e_ref, cst_ref, hst_ref,
                       krow_ref, vrow_ref, q_ref, ga_ref, obcd_ref, chv_ref, ncst_ref, nhst_ref,
                       hc_ref):
    nseq = ROWS // ts
    mod = mod_ref[...]
    shift = mod[:, :, 0:D_MODEL]
    scale = mod[:, :, D_MODEL:2 * D_MODEL]
    h = (x_ref[...] * (1.0 + scale) + shift).reshape(ROWS, D_MODEL).astype(BF16)
    z = jnp.dot(h, win_ref[0], preferred_element_type=F32)

    cos = cos_ref[...]
    sin = sin_ref[...]
    k = _rope(z[:, 256:512], cos, sin)
    krow_ref[...] = k
    vrow_ref[...] = z[:, 512:768]
    q_ref[...] = _rope(z[:, 0:256], cos, sin) * Q_SCALE
    ga_ref[...] = _silu(z[:, 768:1024])

    vn = _layer_norm(_gelu(z[:, 1280:1536]), sgg_ref[0], sgb_ref[0])
    chv_ref[...] = vn
    o_b = _gelu(z[:, 1024:1280]) * _sg_mix(vn, ws_ref.at[0], sgbias_ref[0]) * _silu(z[:, 1536:1792])

    a = z[:, 1792:2304]
    glu = a[:, :GROUP_W] * jax.nn.sigmoid(a[:, GROUP_W:])
    hc_ref[:, 0:CONV_W - 1, :] = cst_ref[0]
    hc_ref[:, CONV_W - 1:CONV_W - 1 + ts, :] = glu.reshape(nseq, ts, GROUP_W)
    y = jnp.zeros((nseq, ts, GROUP_W), F32)
    for j in range(CONV_W):
        y = y + hc_ref[:, j:j + ts, :] * cw_ref[0, j:j + 1, :]
    ncst_ref[0] = hc_ref[:, ts:ts + CONV_W - 1, :]
    yn = _silu(_layer_norm(y.reshape(ROWS, GROUP_W) + cb_ref[0], cng_ref[0], cnb_ref[0]))
    o_c = jnp.dot(yn.astype(BF16), wpw_ref[0], preferred_element_type=F32) * _silu(z[:, 2304:2560])

    lb = _lower_bound(lb_ref, layer)
    qd, kk, vd, hilo = _hgrn_gates(z[:, 2560:2816], z[:, 2816:3072], z[:, 3072:3328], lb)
    o = _hgrn_intra(qd, kk, vd, hilo, g_ref, e_ref, range(3))
    aq = (qd * jnp.exp(_segsum(g_ref[G_CUM8], hilo))).reshape(nseq, ts, GROUP_W)
    bk = (kk * jnp.exp(_segsum(g_ref[G_SUF8], hilo))).reshape(nseq, ts, GROUP_W)
    v3 = vd.reshape(nseq, ts, GROUP_W)
    hilo3 = hilo.astype(F32).reshape(nseq, ts, 2 * GROUP_W)
    ones = jnp.ones((nseq, ts, D_KDIM), BF16)
    inter = []
    for hh in range(D_HEADS):
        lo_, hi_ = 64 * hh, 64 * hh + 64
        st = hst_ref[0, :, hh]
        inter.append(jnp.einsum('bqk,bkv->bqv', aq[:, :, lo_:hi_].astype(BF16), st.astype(BF16),
                                preferred_element_type=F32))
        dec = (jnp.einsum('bsk,bsv->bkv', hilo3[:, :, lo_:hi_].astype(BF16), ones, preferred_element_type=F32)
               + jnp.einsum('bsk,bsv->bkv', hilo3[:, :, GROUP_W + lo_:GROUP_W + hi_].astype(BF16), ones,
                            preferred_element_type=F32))
        upd = jnp.einsum('bsk,bsv->bkv', bk[:, :, lo_:hi_].astype(BF16), v3[:, :, lo_:hi_].astype(BF16),
                         preferred_element_type=F32)
        nhst_ref[0, :, hh] = jnp.exp(dec) * st + upd
    o = o + jnp.concatenate(inter, axis=2).reshape(ROWS, GROUP_W)
    o_d = _head_rms(o, e_ref, hng_ref[0]) * _silu(z[:, 3328:3584])
    obcd_ref[...] = jnp.concatenate([o_b, o_c, o_d], axis=1).astype(BF16)


def _sample_pre(layer, x, mod_s, win_b, cos, sin, sgg, sgb, ws_blk, sgbias, cw, cb, cng, cnb, wpw_b, lbnd, hng,
                gmat, eones, state_conv, state_hgrn):
    bs, ts, _ = x.shape
    nseq = ROWS // ts
    nblk = bs // nseq
    n = bs * ts
    kernel = functools.partial(_sample_pre_kernel, layer, ts)
    lsel3 = lambda i: (layer, 0, 0)
    const2 = lambda i: (0, 0)
    const3 = lambda i: (0, 0, 0)
    row_blk = lambda w: pl.BlockSpec((ROWS, w), lambda i: (i, 0))
    out_shape = (
        jax.ShapeDtypeStruct((n, GROUP_W), F32),
        jax.ShapeDtypeStruct((n, GROUP_W), F32),
        jax.ShapeDtypeStruct((n, GROUP_W), F32),
        jax.ShapeDtypeStruct((n, GROUP_W), F32),
        jax.ShapeDtypeStruct((n, 3 * GROUP_W), BF16),
        jax.ShapeDtypeStruct((n, GROUP_W), F32),
        jax.ShapeDtypeStruct((1, bs, CONV_W - 1, GROUP_W), F32),
        jax.ShapeDtypeStruct((1, bs, D_HEADS, D_KDIM, D_KDIM), F32),
    )
    return pl.pallas_call(
        kernel,
        out_shape=out_shape,
        grid=(nblk,),
        in_specs=[
            pl.BlockSpec((nseq, ts, D_MODEL), lambda i: (i, 0, 0)),
            pl.BlockSpec((nseq, 1, 3 * D_MODEL), lambda i: (i, 0, 0)),
            pl.BlockSpec((1, D_MODEL, D_IN), lsel3),
            pl.BlockSpec((ROWS, 128), const2),
            pl.BlockSpec((ROWS, 128), const2),
            pl.BlockSpec((1, 1, GROUP_W), lsel3),
            pl.BlockSpec((1, 1, GROUP_W), lsel3),
            pl.BlockSpec((1, 4, ROWS, ROWS), lambda i: (layer, 0, 0, 0)),
            pl.BlockSpec((1, ROWS, GROUP_W), lsel3),
            pl.BlockSpec((1, CONV_W, GROUP_W), lsel3),
            pl.BlockSpec((1, 1, GROUP_W), lsel3),
            pl.BlockSpec((1, 1, GROUP_W), lsel3),
            pl.BlockSpec((1, 1, GROUP_W), lsel3),
            pl.BlockSpec((1, GROUP_W, GROUP_W), lsel3),
            pl.BlockSpec(lbnd.shape, const2),
            pl.BlockSpec((1, 1, GROUP_W), lsel3),
            pl.BlockSpec(gmat.shape, const3),
            pl.BlockSpec(eones.shape, const2),
            pl.BlockSpec((1, nseq, CONV_W - 1, GROUP_W), lambda i: (layer, i, 0, 0)),
            pl.BlockSpec((1, nseq, D_HEADS, D_KDIM, D_KDIM), lambda i: (layer, i, 0, 0, 0)),
        ],
        out_specs=(
            row_blk(GROUP_W), row_blk(GROUP_W), row_blk(GROUP_W), row_blk(GROUP_W), row_blk(3 * GROUP_W),
            row_blk(GROUP_W),
            pl.BlockSpec((1, nseq, CONV_W - 1, GROUP_W), lambda i: (0, i, 0, 0)),
            pl.BlockSpec((1, nseq, D_HEADS, D_KDIM, D_KDIM), lambda i: (0, i, 0, 0, 0)),
        ),
        scratch_shapes=[pltpu.VMEM((nseq, 40, GROUP_W), F32)],
        compiler_params=pltpu.CompilerParams(dimension_semantics=("arbitrary",), vmem_limit_bytes=VMEM_LIMIT),
        name=f"sample_pre_l{layer}",
    )(x, mod_s, win_b, cos, sin, sgg, sgb, ws_blk, sgbias, cw, cb, cng, cnb, wpw_b, lbnd, hng, gmat, eones,
      state_conv, state_hgrn)


def _sample_attn_kernel(layer, n_pages, ts, pt_ref, *refs):
    k_pages = refs[0:n_pages]
    v_pages = refs[n_pages:2 * n_pages]
    q_ref, kn_ref, vn_ref, ga_ref, lam_ref, ang_ref, e_ref, o_ref = refs[2 * n_pages:]
    q = q_ref[0]
    grp = lax.broadcasted_iota(jnp.int32, (1, GROUP_W), 1) >> 5
    qexp = jnp.concatenate([jnp.where(grp == j, q, 0.0) for j in range(8)], axis=0).astype(BF16)
    kt_all = jnp.concatenate([r[...] for r in k_pages], axis=1).astype(BF16)
    vt_all = jnp.concatenate([r[...] for r in v_pages], axis=1).astype(BF16)
    s_past = jnp.dot(qexp, kt_all, preferred_element_type=F32)
    s_new = lax.dot_general(qexp, kn_ref[0].astype(BF16), _NT, preferred_element_type=F32)
    t_q = lax.broadcasted_iota(jnp.int32, (8 * ts, ts), 0) & (ts - 1)
    t_k = lax.broadcasted_iota(jnp.int32, (8 * ts, ts), 1)
    vis = t_k <= t_q
    s_new = jnp.where(vis, s_new, NEG_BIG)
    m = jnp.maximum(jnp.max(s_past, axis=-1, keepdims=True), jnp.max(s_new, axis=-1, keepdims=True))
    p_past = jnp.exp2(s_past - m)
    p_new = jnp.where(vis, jnp.exp2(s_new - m), 0.0)
    l = jnp.sum(p_past, axis=-1, keepdims=True) + jnp.sum(p_new, axis=-1, keepdims=True)
    o = (lax.dot_general(p_past.astype(BF16), vt_all, _NT, preferred_element_type=F32)
         + jnp.dot(p_new.astype(BF16), vn_ref[0].astype(BF16), preferred_element_type=F32)) / l
    lam = _lam(lam_ref) + _lam_init(layer)
    head = lax.broadcasted_iota(jnp.int32, (1, GROUP_W), 1) >> 6
    o_a = jnp.zeros((ts, GROUP_W), F32)
    for hh in range(A_HEADS):
        d = o[2 * hh * ts:(2 * hh + 1) * ts] - lam * o[(2 * hh + 1) * ts:(2 * hh + 2) * ts]
        o_a = o_a + jnp.where(head == hh, d, 0.0)
    o_a = _head_rms(o_a, e_ref, ang_ref[0] * (1.0 - _lam_init(layer))) * ga_ref[0]
    o_ref[0] = o_a.astype(BF16)


def _sample_attn(layer, page_table, cache_k, cache_v, q_s, k_new, v_new, ga_s, lam_qk, ang, eones, ts):
    bs, n_pages = page_table.shape
    page = cache_k.shape[3]
    kernel = functools.partial(_sample_attn_kernel, layer, n_pages, ts)

    def page_spec(j):
        return pl.BlockSpec((None, None, GROUP_W, page), lambda b, pt: (layer, pt[b, j], 0, 0))

    seq_blk = pl.BlockSpec((1, ts, GROUP_W), lambda b, pt: (b, 0, 0))
    lsel3 = lambda b, pt: (layer, 0, 0)
    grid_spec = pltpu.PrefetchScalarGridSpec(
        num_scalar_prefetch=1,
        grid=(bs,),
        in_specs=([page_spec(j) for j in range(n_pages)] + [page_spec(j) for j in range(n_pages)]
                  + [seq_blk, seq_blk, seq_blk, seq_blk,
                     pl.BlockSpec((1, 4, A_HALF), lsel3),
                     pl.BlockSpec((1, 1, GROUP_W), lsel3),
                     pl.BlockSpec(eones.shape, lambda b, pt: (0, 0))]),
        out_specs=seq_blk,
    )
    shp3 = (bs, ts, GROUP_W)
    return pl.pallas_call(
        kernel,
        out_shape=jax.ShapeDtypeStruct(shp3, BF16),
        grid_spec=grid_spec,
        compiler_params=pltpu.CompilerParams(dimension_semantics=("arbitrary",), vmem_limit_bytes=VMEM_LIMIT),
        name=f"sample_attn_l{layer}",
    )(page_table, *([cache_k] * n_pages), *([cache_v] * n_pages),
      q_s.reshape(shp3), k_new.reshape(shp3), v_new.reshape(shp3), ga_s.reshape(shp3), lam_qk, ang, eones)


def _sample_out_kernel(depth, ts, oa_ref, obcd_ref, x_ref, mod_ref, wout_ref, lng_ref, lnb_ref, y_ref):
    nseq = ROWS // ts
    mixed = (jnp.dot(oa_ref[...], wout_ref[0, 0:GROUP_W, :], preferred_element_type=F32)
             + jnp.dot(obcd_ref[...], wout_ref[0, GROUP_W:, :], preferred_element_type=F32))
    gate = mod_ref[...][:, :, 2 * D_MODEL:]
    y = _alpha(depth) * x_ref[...] + gate * mixed.reshape(nseq, ts, D_MODEL)
    y_ref[...] = _layer_norm(y, lng_ref[0], lnb_ref[0])


def _sample_out(layer, depth, oa, obcd, x, mod_s, wout_b, lng, lnb):
    bs, ts, _ = x.shape
    nseq = ROWS // ts
    lsel3 = lambda i: (layer, 0, 0)
    return pl.pallas_call(
        functools.partial(_sample_out_kernel, depth, ts),
        out_shape=jax.ShapeDtypeStruct(x.shape, F32),
        grid=(bs // nseq,),
        in_specs=[
            pl.BlockSpec((ROWS, GROUP_W), lambda i: (i, 0)),
            pl.BlockSpec((ROWS, 3 * GROUP_W), lambda i: (i, 0)),
            pl.BlockSpec((nseq, ts, D_MODEL), lambda i: (i, 0, 0)),
            pl.BlockSpec((nseq, 1, 3 * D_MODEL), lambda i: (i, 0, 0)),
            pl.BlockSpec((1, D_MODEL, D_MODEL), lsel3),
            pl.BlockSpec((1, 1, D_MODEL), lsel3),
            pl.BlockSpec((1, 1, D_MODEL), lsel3),
        ],
        out_specs=pl.BlockSpec((nseq, ts, D_MODEL), lambda i: (i, 0, 0)),
        compiler_params=pltpu.CompilerParams(dimension_semantics=("arbitrary",), vmem_limit_bytes=VMEM_LIMIT),
        name=f"sample_out_l{layer}",
    )(oa, obcd, x, mod_s, wout_b, lng, lnb)


def _rope_tables(pos):
    half = A_HALF // 2
    inv = ROPE_THETA ** (-jnp.arange(half, dtype=F32) * 2.0 / A_HALF)
    ang = pos.astype(F32)[:, None] * inv[None, :]
    cos = jnp.cos(ang)
    sin = jnp.sin(ang)
    return jnp.tile(jnp.concatenate([cos, cos], -1), (1, 4)), jnp.tile(jnp.concatenate([-sin, sin], -1), (1, 4))


def kernel(x_prompt, x_sample, cache_k, cache_v, state_conv, state_hgrn, page_table, c_prompt, c_sample, w_ada, b_ada, w_in, lam_qk, attn_norm_g, sg_norm_g, sg_norm_b, w_s, b_s, conv_w, conv_b, conv_norm_g, conv_norm_b, w_pw, lower_bounds, hgrn_norm_g, w_out, ln_g, ln_b):
    depth = w_in.shape[0]
    bp, t, _ = x_prompt.shape
    bs, ts, _ = x_sample.shape
    n_pool, page = cache_k.shape[1], cache_k.shape[2]
    past_len = page_table.shape[1] * page
    assert ts == 8 and ROWS % ts == 0 and bs % (ROWS // ts) == 0 and t % ROWS == 0
    nseq = ROWS // ts

    gmat = _segment_matrices()
    eones = _head_ones()
    win_b = w_in.astype(BF16)
    wout_b = w_out.astype(BF16)
    wpw_b = w_pw.astype(BF16)
    row3 = lambda a: a.reshape(depth, 1, a.shape[-1])
    sgg, sgb, cb, cng, cnb = row3(sg_norm_g), row3(sg_norm_b), row3(conv_b), row3(conv_norm_g), row3(conv_norm_b)
    lng, lnb = row3(ln_g), row3(ln_b)
    ang = row3(jnp.tile(attn_norm_g, (1, A_HEADS)))
    hng = row3(jnp.tile(hgrn_norm_g, (1, D_HEADS)))
    sgbias_p = jnp.repeat(jnp.swapaxes(b_s, 1, 2), GROUP_W // 4, axis=2)
    sgbias_s = jnp.tile(sgbias_p[:, :ts], (1, nseq, 1))
    eye = jnp.eye(nseq, dtype=F32)
    ws_blk = jnp.einsum('ab,lgts->lgatbs', eye, w_s[:, :, :ts, :ts]).reshape(depth, 4, ROWS, ROWS)
    cos_p, sin_p = _rope_tables(jnp.arange(t))
    cos_s, sin_s = _rope_tables(past_len + jnp.arange(ts))
    cos_s, sin_s = jnp.tile(cos_s, (nseq, 1)), jnp.tile(sin_s, (nseq, 1))
    ck = jnp.transpose(cache_k, (0, 1, 3, 4, 2)).reshape(depth, n_pool, GROUP_W, page)
    cv = jnp.transpose(cache_v, (0, 1, 3, 4, 2)).reshape(depth, n_pool, GROUP_W, page)

    rows = bp + bs
    pad = (-rows) % 8
    c_all = jnp.concatenate([c_prompt, c_sample, jnp.zeros((pad, D_MODEL), F32)], axis=0)
    mod = _ada(c_all, w_ada, b_ada)

    xp, xs = x_prompt, x_sample
    outs = [[] for _ in range(9)]
    for l in range(depth):
        mod_p = mod[l, :bp].reshape(bp, 1, 3 * D_MODEL)
        mod_s = mod[l, bp:bp + bs].reshape(bs, 1, 3 * D_MODEL)
        krow, vrow, qt, kb, vt, ga, obcd, cst_p, hst_p = _prompt_pre(
            l, xp, mod_p, win_b, cos_p, sin_p, sgg, sgb, w_s, sgbias_p, conv_w, cb, cng, cnb, wpw_b, lower_bounds,
            hng, gmat, eones)
        xp = _prompt_attn(l, depth, qt, kb, vt, ga, obcd, xp, mod_p, wout_b, lam_qk, ang, lng, lnb)

        krow_s, vrow_s, q_s, ga_s, obcd_s, chv_s, cst_s, hst_s = _sample_pre(
            l, xs, mod_s, win_b, cos_s, sin_s, sgg, sgb, ws_blk, sgbias_s, conv_w, cb, cng, cnb, wpw_b,
            lower_bounds, hng, gmat, eones, state_conv, state_hgrn)
        oa_s = _sample_attn(l, page_table, ck, cv, q_s, krow_s, vrow_s, ga_s, lam_qk, ang, eones, ts)
        xs = _sample_out(l, depth, oa_s.reshape(bs * ts, GROUP_W), obcd_s, xs, mod_s, wout_b, lng, lnb)

        outs[0].append(jnp.transpose(krow.reshape(bp, A_HEADS, 2 * A_HALF, t), (0, 3, 1, 2)))
        outs[1].append(jnp.transpose(vrow.reshape(bp, A_HEADS, A_VDIM, t), (0, 3, 1, 2)))
        outs[2].append(krow_s.reshape(bs, ts, A_HEADS, 2 * A_HALF))
        outs[3].append(vrow_s.reshape(bs, ts, A_HEADS, A_VDIM))
        outs[4].append(chv_s.reshape(bs, ts, GROUP_W))
        outs[5].append(cst_p)
        outs[6].append(cst_s[0])
        outs[7].append(hst_p)
        outs[8].append(hst_s[0])
    return (xp, xs) + tuple(jnp.stack(o) for o in outs)
```

```python
import functools
import math

import jax
import jax.numpy as jnp
import numpy as np
from jax import lax
from jax.experimental import pallas as pl
from jax.experimental.pallas import tpu as pltpu

F32 = jnp.float32
BF16 = jnp.bfloat16

D_MODEL = 1024
GROUP_W = 256
A_HEADS = 4
A_HALF = 32
A_VDIM = 64
ROPE_THETA = 10000.0
SG_CHUNK = 128
CONV_W = 31
D_HEADS = 4
D_KDIM = 64
F_FLOOR = 1e-30
EPS = 1e-5
NEG_BIG = -1e30
D_IN = 14 * GROUP_W
SQRT_HALF = 0.7071067811865476
Q_SCALE = (A_HALF ** -0.5) * math.log2(math.e)

ROWS = 128
SEQ_PER_STEP = 1
VMEM_LIMIT = 56 * 1024 * 1024

G_CUM, G_SUF, G_CUM8, G_SUF8 = 7, 8, 9, 10

_NT = (((1,), (1,)), ((), ()))
_TN = (((0,), (0,)), ((), ()))


def _segment_matrices():
    t = np.arange(ROWS)[:, None]
    s = np.arange(ROWS)[None, :]
    mats = []
    for lvl in range(7):
        mid = ((t >> (lvl + 1)) << (lvl + 1)) + (1 << lvl)
        second = ((t >> lvl) & 1) == 1
        mats.append(np.where(second, (s >= mid) & (s <= t), (s > t) & (s < mid)))
    mats.append(s <= t)
    mats.append(s > t)
    same8 = (t >> 3) == (s >> 3)
    mats.append(same8 & (s <= t))
    mats.append(same8 & (s > t))
    return jnp.asarray(np.stack(mats).astype(np.float32), dtype=BF16)


def _head_ones():
    h = np.arange(GROUP_W) // 64
    return jnp.asarray((h[:, None] == h[None, :]).astype(np.float32), dtype=BF16)


def _silu(x):
    return x * jax.nn.sigmoid(x)


def _gelu(x):
    return 0.5 * x * (1.0 + lax.erf(x * SQRT_HALF))


def _layer_norm(x, g, b):
    xc = x - jnp.mean(x, axis=-1, keepdims=True)
    var = jnp.mean(xc * xc, axis=-1, keepdims=True)
    return xc * lax.rsqrt(var + EPS) * g + b


def _head_rms(x, e_ref, g):
    ms = jnp.dot((x * x).astype(BF16), e_ref[...], preferred_element_type=F32) * (1.0 / 64.0)
    return x * lax.rsqrt(ms + EPS) * g


def _rope(x, cos, sin):
    outs = []
    for half in range(2):
        xh = x[:, 128 * half:128 * half + 128]
        lane = lax.broadcasted_iota(jnp.int32, xh.shape, 1)
        partner = jnp.where((lane & 16) == 0, pltpu.roll(xh, 112, 1), pltpu.roll(xh, 16, 1))
        outs.append(xh * cos + partner * sin)
    return jnp.concatenate(outs, axis=1)


def _lower_bound(lb_ref, layer):
    lb = lb_ref[...]
    e = jnp.exp(lb - jnp.max(lb, axis=0, keepdims=True))
    soft = e / jnp.sum(e, axis=0, keepdims=True)
    acc = jnp.zeros((1, GROUP_W), F32)
    for i in range(1, layer + 1):
        acc = acc + soft[i:i + 1]
    return acc


def _lam(lam_ref):
    lp = lam_ref[0]
    a = jnp.sum(lp[0:1] * lp[1:2], axis=-1, keepdims=True)
    b = jnp.sum(lp[2:3] * lp[3:4], axis=-1, keepdims=True)
    return jnp.exp(a) - jnp.exp(b)


def _lam_init(layer):
    return 0.8 - 0.6 * math.exp(-0.3 * layer)


def _alpha(depth):
    return (2.0 * depth) ** 0.25


def _sg_mix(vn, ws_ref, bias):
    group = lax.broadcasted_iota(jnp.int32, (1, GROUP_W), 1) >> 6
    r = lax.broadcasted_iota(jnp.int32, (ROWS, ROWS), 0)
    c = lax.broadcasted_iota(jnp.int32, (ROWS, ROWS), 1)
    acc = bias
    for g in range(4):
        wm = jnp.where(r >= c, ws_ref[g], 0.0).astype(BF16)
        vm = jnp.where(group == g, vn, 0.0).astype(BF16)
        acc = acc + jnp.dot(wm, vm, preferred_element_type=F32)
    return acc


def _hgrn_gates(dq, df, di, lb):
    f = lb + (1.0 - lb) * jax.nn.sigmoid(df)
    lf = jnp.log(jnp.maximum(f, F_FLOOR))
    hi = lf.astype(BF16)
    lo = (lf - hi.astype(F32)).astype(BF16)
    return _silu(dq), 1.0 - f, di, jnp.concatenate([hi, lo], axis=1)


def _segsum(g, hilo):
    r = jnp.dot(g, hilo, preferred_element_type=F32)
    return r[:, :GROUP_W] + r[:, GROUP_W:]


def _hgrn_intra(qd, kk, vd, hilo, g_ref, e_ref, levels):
    row = lax.broadcasted_iota(jnp.int32, (ROWS, 1), 0)
    head = lax.broadcasted_iota(jnp.int32, (1, GROUP_W), 1) >> 6
    t_idx = lax.broadcasted_iota(jnp.int32, (4 * ROWS, ROWS), 0) & (ROWS - 1)
    s_idx = lax.broadcasted_iota(jnp.int32, (4 * ROWS, ROWS), 1)
    att = jnp.zeros((4 * ROWS, ROWS), F32)
    for lvl in levels:
        e = jnp.exp(_segsum(g_ref[lvl], hilo))
        second = ((row >> lvl) & 1) == 1
        qe = jnp.where(second, qd * e, 0.0)
        b = jnp.where(second, 0.0, kk * e).astype(BF16)
        a = jnp.concatenate([jnp.where(head == h, qe, 0.0) for h in range(4)], axis=0).astype(BF16)
        s = lax.dot_general(a, b, _NT, preferred_element_type=F32)
        att = att + jnp.where((t_idx >> (lvl + 1)) == (s_idx >> (lvl + 1)), s, 0.0)
    att_cat = jnp.concatenate([att[h * ROWS:(h + 1) * ROWS] for h in range(4)], axis=1).astype(BF16)
    v_stack = jnp.concatenate([jnp.where(head == h, vd, 0.0) for h in range(4)], axis=0).astype(BF16)
    o = jnp.dot(att_cat, v_stack, preferred_element_type=F32)
    diag = jnp.dot((qd * kk).astype(BF16), e_ref[...], preferred_element_type=F32)
    return o + diag * vd


def _ada_kernel(c_ref, w_ref, b_ref, o_ref):
    c = c_ref[...]
    a = _silu(c).astype(BF16)
    o_ref[0] = jnp.dot(a, w_ref[0].astype(BF16), preferred_element_type=F32) + b_ref[0]


def _ada(c_all, w_ada, b_ada):
    depth = w_ada.shape[0]
    rows = c_all.shape[0]
    return pl.pallas_call(
        _ada_kernel,
        out_shape=jax.ShapeDtypeStruct((depth, rows, 3 * D_MODEL), F32),
        grid=(depth, 3),
        in_specs=[
            pl.BlockSpec((rows, D_MODEL), lambda l, n: (0, 0)),
            pl.BlockSpec((1, D_MODEL, D_MODEL), lambda l, n: (l, 0, n)),
            pl.BlockSpec((1, 1, D_MODEL), lambda l, n: (l, 0, n)),
        ],
        out_specs=pl.BlockSpec((1, rows, D_MODEL), lambda l, n: (l, 0, n)),
        compiler_params=pltpu.CompilerParams(dimension_semantics=("arbitrary", "arbitrary"),
                                             vmem_limit_bytes=VMEM_LIMIT),
        name="adaln",
    )(c_all, w_ada, b_ada.reshape(depth, 1, 3 * D_MODEL))


def _prompt_pre_kernel(layer, tb,
                       x_ref, mod_ref, win_ref, cos_ref, sin_ref, sgg_ref, sgb_ref, ws_ref, sgbias_ref,
                       cw_ref, cb_ref, cng_ref, cnb_ref, wpw_ref, lb_ref, hng_ref, g_ref, e_ref,
                       krow_ref, vrow_ref, qt_ref, kb_ref, vt_ref, ga_ref, obcd_ref, cst_ref, hst_ref,
                       hc_ref, s_ref):
    i = pl.program_id(1)

    @pl.when(i == 0)
    def _():
        hc_ref[0:32, :] = jnp.zeros((32, GROUP_W), F32)
        s_ref[...] = jnp.zeros(s_ref.shape, F32)

    mod = mod_ref[0]
    shift = mod[:, 0:D_MODEL]
    scale = mod[:, D_MODEL:2 * D_MODEL]
    h = (x_ref[0] * (1.0 + scale) + shift).astype(BF16)
    z = jnp.dot(h, win_ref[0], preferred_element_type=F32)

    cos = cos_ref[...]
    sin = sin_ref[...]
    q = _rope(z[:, 0:256], cos, sin)
    k = _rope(z[:, 256:512], cos, sin)
    v = z[:, 512:768]
    kt = k.T
    vt = v.T
    krow_ref[0] = kt
    vrow_ref[0] = vt
    kb_ref[0] = k.astype(BF16)
    qt_ref[0] = (q * Q_SCALE).T.astype(BF16)
    vt_ref[0] = vt.astype(BF16)
    ga_ref[0] = _silu(z[:, 768:1024])

    a = z[:, 1792:2304]
    glu = a[:, :GROUP_W] * jax.nn.sigmoid(a[:, GROUP_W:])
    hc_ref[32:32 + tb, :] = glu
    y = jnp.zeros((tb, GROUP_W), F32)
    for j in range(CONV_W):
        y = y + hc_ref[pl.ds(2 + j, tb), :] * cw_ref[0, j:j + 1, :]
    cst_ref[0] = hc_ref[pl.ds(tb + 2, CONV_W - 1), :]
    hc_ref[0:32, :] = hc_ref[pl.ds(tb, 32), :]
    yn = _silu(_layer_norm(y + cb_ref[0], cng_ref[0], cnb_ref[0]))
    o_c = jnp.dot(yn.astype(BF16), wpw_ref[0], preferred_element_type=F32) * _silu(z[:, 2304:2560])

    lb = _lower_bound(lb_ref, layer)
    o_b_parts, o_d_parts = [], []
    for c in range(tb // ROWS):
        zc = z[c * ROWS:(c + 1) * ROWS]
        vn = _layer_norm(_gelu(zc[:, 1280:1536]), sgg_ref[0], sgb_ref[0])
        mixed = _sg_mix(vn, ws_ref.at[0], sgbias_ref[0])
        o_b_parts.append(_gelu(zc[:, 1024:1280]) * mixed * _silu(zc[:, 1536:1792]))
        qd, kk, vd, hilo = _hgrn_gates(zc[:, 2560:2816], zc[:, 2816:3072], zc[:, 3072:3328], lb)
        o = _hgrn_intra(qd, kk, vd, hilo, g_ref, e_ref, range(7))
        aq = qd * jnp.exp(_segsum(g_ref[G_CUM], hilo))
        bk = kk * jnp.exp(_segsum(g_ref[G_SUF], hilo))
        ones = jnp.ones((ROWS, D_KDIM), BF16)
        inter = []
        for hh in range(D_HEADS):
            lo_, hi_ = 64 * hh, 64 * hh + 64
            st = s_ref[hh]
            inter.append(jnp.dot(aq[:, lo_:hi_].astype(BF16), st.astype(BF16), preferred_element_type=F32))
            dec = (lax.dot_general(hilo[:, lo_:hi_], ones, _TN, preferred_element_type=F32)
                   + lax.dot_general(hilo[:, GROUP_W + lo_:GROUP_W + hi_], ones, _TN, preferred_element_type=F32))
            upd = lax.dot_general(bk[:, lo_:hi_].astype(BF16), vd[:, lo_:hi_].astype(BF16), _TN,
                                  preferred_element_type=F32)
            s_ref[hh] = jnp.exp(dec) * st + upd
        o = o + jnp.concatenate(inter, axis=1)
        o_d_parts.append(_head_rms(o, e_ref, hng_ref[0]) * _silu(zc[:, 3328:3584]))
    o_b = jnp.concatenate(o_b_parts, axis=0)
    o_d = jnp.concatenate(o_d_parts, axis=0)
    obcd_ref[0] = jnp.concatenate([o_b, o_c, o_d], axis=1).astype(BF16)
    hst_ref[0] = s_ref[...]


def _prompt_pre(layer, x, mod_p, win_b, cos, sin, sgg, sgb, ws, sgbias, cw, cb, cng, cnb, wpw_b, lbnd, hng,
                gmat, eones):
    b, t, _ = x.shape
    tb = min(256, t)
    kernel = functools.partial(_prompt_pre_kernel, layer, tb)
    lsel3 = lambda bb, i: (layer, 0, 0)
    const2 = lambda bb, i: (0, 0)
    const3 = lambda bb, i: (0, 0, 0)
    row_blk = lambda w: pl.BlockSpec((1, tb, w), lambda bb, i: (bb, i, 0))
    col_blk = pl.BlockSpec((1, GROUP_W, tb), lambda bb, i: (bb, 0, i))
    out_shape = (
        jax.ShapeDtypeStruct((b, GROUP_W, t), F32),
        jax.ShapeDtypeStruct((b, GROUP_W, t), F32),
        jax.ShapeDtypeStruct((b, GROUP_W, t), BF16),
        jax.ShapeDtypeStruct((b, t, GROUP_W), BF16),
        jax.ShapeDtypeStruct((b, GROUP_W, t), BF16),
        jax.ShapeDtypeStruct((b, t, GROUP_W), F32),
        jax.ShapeDtypeStruct((b, t, 3 * GROUP_W), BF16),
        jax.ShapeDtypeStruct((b, CONV_W - 1, GROUP_W), F32),
        jax.ShapeDtypeStruct((b, D_HEADS, D_KDIM, D_KDIM), F32),
    )
    return pl.pallas_call(
        kernel,
        out_shape=out_shape,
        grid=(b, t // tb),
        in_specs=[
            row_blk(D_MODEL),
            pl.BlockSpec((1, 1, 3 * D_MODEL), lambda bb, i: (bb, 0, 0)),
            pl.BlockSpec((1, D_MODEL, D_IN), lsel3),
            pl.BlockSpec((tb, 128), lambda bb, i: (i, 0)),
            pl.BlockSpec((tb, 128), lambda bb, i: (i, 0)),
            pl.BlockSpec((1, 1, GROUP_W), lsel3),
            pl.BlockSpec((1, 1, GROUP_W), lsel3),
            pl.BlockSpec((1, 4, ROWS, ROWS), lambda bb, i: (layer, 0, 0, 0)),
            pl.BlockSpec((1, ROWS, GROUP_W), lsel3),
            pl.BlockSpec((1, CONV_W, GROUP_W), lsel3),
            pl.BlockSpec((1, 1, GROUP_W), lsel3),
            pl.BlockSpec((1, 1, GROUP_W), lsel3),
            pl.BlockSpec((1, 1, GROUP_W), lsel3),
            pl.BlockSpec((1, GROUP_W, GROUP_W), lsel3),
            pl.BlockSpec(lbnd.shape, const2),
            pl.BlockSpec((1, 1, GROUP_W), lsel3),
            pl.BlockSpec(gmat.shape, const3),
            pl.BlockSpec(eones.shape, const2),
        ],
        out_specs=(
            col_blk, col_blk, col_blk, row_blk(GROUP_W), col_blk, row_blk(GROUP_W),
            row_blk(3 * GROUP_W),
            pl.BlockSpec((1, CONV_W - 1, GROUP_W), lambda bb, i: (bb, 0, 0)),
            pl.BlockSpec((1, D_HEADS, D_KDIM, D_KDIM), lambda bb, i: (bb, 0, 0, 0)),
        ),
        scratch_shapes=[pltpu.VMEM((32 + tb, GROUP_W), F32), pltpu.VMEM((D_HEADS, D_KDIM, D_KDIM), F32)],
        compiler_params=pltpu.CompilerParams(dimension_semantics=("arbitrary", "arbitrary"),
                                             vmem_limit_bytes=VMEM_LIMIT),
        name=f"prompt_pre_l{layer}",
    )(x, mod_p, win_b, cos, sin, sgg, sgb, ws, sgbias, cw, cb, cng, cnb, wpw_b, lbnd, hng, gmat, eones)


def _prompt_attn_kernel(layer, depth, tq,
                        qt_ref, kb_ref, vt_ref, ga_ref, obcd_ref, x_ref, mod_ref, wout_ref, lam_ref, ang_ref,
                        lng_ref, lnb_ref, y_ref, acc_ref, m_ref, qm_ref, s_ref, p_ref):
    i = pl.program_id(1)
    qt = qt_ref[0]
    rowg = lax.broadcasted_iota(jnp.int32, (GROUP_W, 1), 0) >> 5
    for j in range(8):
        qm_ref[j] = jnp.where(rowg == j, qt, jnp.zeros_like(qt))
    m_ref[...] = jnp.full(m_ref.shape, NEG_BIG, F32)
    acc_ref[...] = jnp.zeros(acc_ref.shape, F32)
    ones = jnp.ones((16, tq), BF16)

    def tile(kt, masked):
        off = pl.multiple_of(kt * tq, tq)
        kk = kb_ref[0, pl.ds(off, tq), :]
        if masked:
            key = lax.broadcasted_iota(jnp.int32, (tq, tq), 0)
            qry = lax.broadcasted_iota(jnp.int32, (tq, tq), 1)
            visible = key <= qry
        cmax = []
        for j in range(8):
            s = jnp.dot(kk, qm_ref[j], preferred_element_type=F32)
            if masked:
                s = jnp.where(visible, s, NEG_BIG)
            s_ref[j] = s
            cmax.append(jnp.max(s, axis=0, keepdims=True))
        alphas = []
        for j in range(8):
            m_old = m_ref[j]
            m_new = jnp.maximum(m_old, cmax[j])
            alphas.append(jnp.exp2(m_old - m_new))
            p_ref[j] = jnp.exp2(s_ref[j] - m_new).astype(BF16)
            m_ref[j] = m_new
        for hh in range(A_HEADS):
            vv = jnp.concatenate([vt_ref[0, 64 * hh:64 * hh + 64, pl.ds(off, tq)], ones], axis=0)
            for j in (2 * hh, 2 * hh + 1):
                acc_ref[j] = alphas[j] * acc_ref[j] + jnp.dot(vv, p_ref[j], preferred_element_type=F32)

    def body(kt, carry):
        tile(kt, False)
        return carry

    lax.fori_loop(0, i, body, 0)
    tile(i, True)

    lam = _lam(lam_ref) + _lam_init(layer)
    heads = []
    for hh in range(A_HEADS):
        a0 = acc_ref[2 * hh]
        a1 = acc_ref[2 * hh + 1]
        d = a0[0:A_VDIM] / a0[A_VDIM:A_VDIM + 1] - lam * (a1[0:A_VDIM] / a1[A_VDIM:A_VDIM + 1])
        ms = jnp.mean(d * d, axis=0, keepdims=True)
        heads.append(d * lax.rsqrt(ms + EPS))
    o_a = jnp.concatenate(heads, axis=0).T
    o_a = o_a * (ang_ref[0] * (1.0 - _lam_init(layer))) * ga_ref[0]
    mixed = (jnp.dot(o_a.astype(BF16), wout_ref[0, 0:GROUP_W, :], preferred_element_type=F32)
             + jnp.dot(obcd_ref[0], wout_ref[0, GROUP_W:, :], preferred_element_type=F32))
    gate = mod_ref[0][:, 2 * D_MODEL:]
    y_ref[0] = _layer_norm(_alpha(depth) * x_ref[0] + gate * mixed, lng_ref[0], lnb_ref[0])


def _prompt_attn(layer, depth, qt, kb, vt, ga, obcd, x, mod_p, wout_b, lam_qk, ang, lng, lnb):
    b, t, _ = x.shape
    tq = min(256, t)
    kernel = functools.partial(_prompt_attn_kernel, layer, depth, tq)
    lsel3 = lambda bb, i: (layer, 0, 0)
    row_blk = lambda w: pl.BlockSpec((1, tq, w), lambda bb, i: (bb, i, 0))
    return pl.pallas_call(
        kernel,
        out_shape=jax.ShapeDtypeStruct((b, t, D_MODEL), F32),
        grid=(b, t // tq),
        in_specs=[
            pl.BlockSpec((1, GROUP_W, tq), lambda bb, i: (bb, 0, i)),
            pl.BlockSpec((1, t, GROUP_W), lambda bb, i: (bb, 0, 0)),
            pl.BlockSpec((1, GROUP_W, t), lambda bb, i: (bb, 0, 0)),
            row_blk(GROUP_W),
            row_blk(3 * GROUP_W),
            row_blk(D_MODEL),
            pl.BlockSpec((1, 1, 3 * D_MODEL), lambda bb, i: (bb, 0, 0)),
            pl.BlockSpec((1, D_MODEL, D_MODEL), lsel3),
            pl.BlockSpec((1, 4, A_HALF), lsel3),
            pl.BlockSpec((1, 1, GROUP_W), lsel3),
            pl.BlockSpec((1, 1, D_MODEL), lsel3),
            pl.BlockSpec((1, 1, D_MODEL), lsel3),
        ],
        out_specs=row_blk(D_MODEL),
        scratch_shapes=[pltpu.VMEM((8, A_VDIM + 16, tq), F32), pltpu.VMEM((8, 1, tq), F32),
                        pltpu.VMEM((8, GROUP_W, tq), BF16), pltpu.VMEM((8, tq, tq), F32),
                        pltpu.VMEM((8, tq, tq), BF16)],
        compiler_params=pltpu.CompilerParams(dimension_semantics=("arbitrary", "arbitrary"),
                                             vmem_limit_bytes=VMEM_LIMIT),
        name=f"prompt_attn_l{layer}",
    )(qt, kb, vt, ga, obcd, x, mod_p, wout_b, lam_qk, ang, lng, lnb)


def _sample_pre_kernel(layer, ts,
                       x_ref, mod_ref, win_ref, cos_ref, sin_ref, sgg_ref, sgb_ref, ws_ref, sgbias_ref,
                       cw_ref, cb_ref, cng_ref, cnb_ref, wpw_ref, lb_ref, hng_ref, g_ref, e_ref, cst_ref, hst_ref,
                       krow_ref, vrow_ref, q_ref, ga_ref, obcd_ref, chv_ref, ncst_ref, nhst_ref,
                       hc_ref):
    nseq = ROWS // ts
    mod = mod_ref[...]
    shift = mod[:, :, 0:D_MODEL]
    scale = mod[:, :, D_MODEL:2 * D_MODEL]
    h = (x_ref[...] * (1.0 + scale) + shift).reshape(ROWS, D_MODEL).astype(BF16)
    z = jnp.dot(h, win_ref[0], preferred_element_type=F32)

    cos = cos_ref[...]
    sin = sin_ref[...]
    k = _rope(z[:, 256:512], cos, sin)
    krow_ref[...] = k
    vrow_ref[...] = z[:, 512:768]
    q_ref[...] = _rope(z[:, 0:256], cos, sin) * Q_SCALE
    ga_ref[...] = _silu(z[:, 768:1024])

    vn = _layer_norm(_gelu(z[:, 1280:1536]), sgg_ref[0], sgb_ref[0])
    chv_ref[...] = vn
    o_b = _gelu(z[:, 1024:1280]) * _sg_mix(vn, ws_ref.at[0], sgbias_ref[0]) * _silu(z[:, 1536:1792])

    a = z[:, 1792:2304]
    glu = a[:, :GROUP_W] * jax.nn.sigmoid(a[:, GROUP_W:])
    hc_ref[:, 0:CONV_W - 1, :] = cst_ref[0]
    hc_ref[:, CONV_W - 1:CONV_W - 1 + ts, :] = glu.reshape(nseq, ts, GROUP_W)
    y = jnp.zeros((nseq, ts, GROUP_W), F32)
    for j in range(CONV_W):
        y = y + hc_ref[:, j:j + ts, :] * cw_ref[0, j:j + 1, :]
    ncst_ref[0] = hc_ref[:, ts:ts + CONV_W - 1, :]
    yn = _silu(_layer_norm(y.reshape(ROWS, GROUP_W) + cb_ref[0], cng_ref[0], cnb_ref[0]))
    o_c = jnp.dot(yn.astype(BF16), wpw_ref[0], preferred_element_type=F32) * _silu(z[:, 2304:2560])

    lb = _lower_bound(lb_ref, layer)
    qd, kk, vd, hilo = _hgrn_gates(z[:, 2560:2816], z[:, 2816:3072], z[:, 3072:3328], lb)
    o = _hgrn_intra(qd, kk, vd, hilo, g_ref, e_ref, range(3))
    aq = (qd * jnp.exp(_segsum(g_ref[G_CUM8], hilo))).reshape(nseq, ts, GROUP_W)
    bk = (kk * jnp.exp(_segsum(g_ref[G_SUF8], hilo))).reshape(nseq, ts, GROUP_W)
    v3 = vd.reshape(nseq, ts, GROUP_W)
    hilo3 = hilo.astype(F32).reshape(nseq, ts, 2 * GROUP_W)
    ones = jnp.ones((nseq, ts, D_KDIM), BF16)
    inter = []
    for hh in range(D_HEADS):
        lo_, hi_ = 64 * hh, 64 * hh + 64
        st = hst_ref[0, :, hh]
        inter.append(jnp.einsum('bqk,bkv->bqv', aq[:, :, lo_:hi_].astype(BF16), st.astype(BF16),
                                preferred_element_type=F32))
        dec = (jnp.einsum('bsk,bsv->bkv', hilo3[:, :, lo_:hi_].astype(BF16), ones, preferred_element_type=F32)
               + jnp.einsum('bsk,bsv->bkv', hilo3[:, :, GROUP_W + lo_:GROUP_W + hi_].astype(BF16), ones,
                            preferred_element_type=F32))
        upd = jnp.einsum('bsk,bsv->bkv', bk[:, :, lo_:hi_].astype(BF16), v3[:, :, lo_:hi_].astype(BF16),
                         preferred_element_type=F32)
        nhst_ref[0, :, hh] = jnp.exp(dec) * st + upd
    o = o + jnp.concatenate(inter, axis=2).reshape(ROWS, GROUP_W)
    o_d = _head_rms(o, e_ref, hng_ref[0]) * _silu(z[:, 3328:3584])
    obcd_ref[...] = jnp.concatenate([o_b, o_c, o_d], axis=1).astype(BF16)


def _sample_pre(layer, x, mod_s, win_b, cos, sin, sgg, sgb, ws_blk, sgbias, cw, cb, cng, cnb, wpw_b, lbnd, hng,
                gmat, eones, state_conv, state_hgrn):
    bs, ts, _ = x.shape
    nseq = ROWS // ts
    nblk = bs // nseq
    n = bs * ts
    kernel = functools.partial(_sample_pre_kernel, layer, ts)
    lsel3 = lambda i: (layer, 0, 0)
    const2 = lambda i: (0, 0)
    const3 = lambda i: (0, 0, 0)
    row_blk = lambda w: pl.BlockSpec((ROWS, w), lambda i: (i, 0))
    out_shape = (
        jax.ShapeDtypeStruct((n, GROUP_W), F32),
        jax.ShapeDtypeStruct((n, GROUP_W), F32),
        jax.ShapeDtypeStruct((n, GROUP_W), F32),
        jax.ShapeDtypeStruct((n, GROUP_W), F32),
        jax.ShapeDtypeStruct((n, 3 * GROUP_W), BF16),
        jax.ShapeDtypeStruct((n, GROUP_W), F32),
        jax.ShapeDtypeStruct((1, bs, CONV_W - 1, GROUP_W), F32),
        jax.ShapeDtypeStruct((1, bs, D_HEADS, D_KDIM, D_KDIM), F32),
    )
    return pl.pallas_call(
        kernel,
        out_shape=out_shape,
        grid=(nblk,),
        in_specs=[
            pl.BlockSpec((nseq, ts, D_MODEL), lambda i: (i, 0, 0)),
            pl.BlockSpec((nseq, 1, 3 * D_MODEL), lambda i: (i, 0, 0)),
            pl.BlockSpec((1, D_MODEL, D_IN), lsel3),
            pl.BlockSpec((ROWS, 128), const2),
            pl.BlockSpec((ROWS, 128), const2),
            pl.BlockSpec((1, 1, GROUP_W), lsel3),
            pl.BlockSpec((1, 1, GROUP_W), lsel3),
            pl.BlockSpec((1, 4, ROWS, ROWS), lambda i: (layer, 0, 0, 0)),
            pl.BlockSpec((1, ROWS, GROUP_W), lsel3),
            pl.BlockSpec((1, CONV_W, GROUP_W), lsel3),
            pl.BlockSpec((1, 1, GROUP_W), lsel3),
            pl.BlockSpec((1, 1, GROUP_W), lsel3),
            pl.BlockSpec((1, 1, GROUP_W), lsel3),
            pl.BlockSpec((1, GROUP_W, GROUP_W), lsel3),
            pl.BlockSpec(lbnd.shape, const2),
            pl.BlockSpec((1, 1, GROUP_W), lsel3),
            pl.BlockSpec(gmat.shape, const3),
            pl.BlockSpec(eones.shape, const2),
            pl.BlockSpec((1, nseq, CONV_W - 1, GROUP_W), lambda i: (layer, i, 0, 0)),
            pl.BlockSpec((1, nseq, D_HEADS, D_KDIM, D_KDIM), lambda i: (layer, i, 0, 0, 0)),
        ],
        out_specs=(
            row_blk(GROUP_W), row_blk(GROUP_W), row_blk(GROUP_W), row_blk(GROUP_W), row_blk(3 * GROUP_W),
            row_blk(GROUP_W),
            pl.BlockSpec((1, nseq, CONV_W - 1, GROUP_W), lambda i: (0, i, 0, 0)),
            pl.BlockSpec((1, nseq, D_HEADS, D_KDIM, D_KDIM), lambda i: (0, i, 0, 0, 0)),
        ),
        scratch_shapes=[pltpu.VMEM((nseq, 40, GROUP_W), F32)],
        compiler_params=pltpu.CompilerParams(dimension_semantics=("arbitrary",), vmem_limit_bytes=VMEM_LIMIT),
        name=f"sample_pre_l{layer}",
    )(x, mod_s, win_b, cos, sin, sgg, sgb, ws_blk, sgbias, cw, cb, cng, cnb, wpw_b, lbnd, hng, gmat, eones,
      state_conv, state_hgrn)


def _sample_attn_kernel(layer, n_pages, nsq, ts, pt_ref, *refs):
    k_pages = refs[0:nsq * n_pages]
    v_pages = refs[nsq * n_pages:2 * nsq * n_pages]
    q_ref, kn_ref, vn_ref, ga_ref, lam_ref, ang_ref, e_ref, o_ref = refs[2 * nsq * n_pages:]
    grp = lax.broadcasted_iota(jnp.int32, (1, GROUP_W), 1) >> 5
    head = lax.broadcasted_iota(jnp.int32, (1, GROUP_W), 1) >> 6
    t_q = lax.broadcasted_iota(jnp.int32, (8 * ts, ts), 0) & (ts - 1)
    t_k = lax.broadcasted_iota(jnp.int32, (8 * ts, ts), 1)
    vis = t_k <= t_q
    lam = _lam(lam_ref) + _lam_init(layer)
    for sq in range(nsq):
        q = q_ref[sq]
        qexp = jnp.concatenate([jnp.where(grp == j, q, 0.0) for j in range(8)], axis=0).astype(BF16)
        kt_all = jnp.concatenate([r[...] for r in k_pages[sq * n_pages:(sq + 1) * n_pages]], axis=1).astype(BF16)
        vt_all = jnp.concatenate([r[...] for r in v_pages[sq * n_pages:(sq + 1) * n_pages]], axis=1).astype(BF16)
        s_past = jnp.dot(qexp, kt_all, preferred_element_type=F32)
        s_new = lax.dot_general(qexp, kn_ref[sq].astype(BF16), _NT, preferred_element_type=F32)
        s_new = jnp.where(vis, s_new, NEG_BIG)
        m = jnp.maximum(jnp.max(s_past, axis=-1, keepdims=True), jnp.max(s_new, axis=-1, keepdims=True))
        p_past = jnp.exp2(s_past - m)
        p_new = jnp.where(vis, jnp.exp2(s_new - m), 0.0)
        l = jnp.sum(p_past, axis=-1, keepdims=True) + jnp.sum(p_new, axis=-1, keepdims=True)
        o = (lax.dot_general(p_past.astype(BF16), vt_all, _NT, preferred_element_type=F32)
             + jnp.dot(p_new.astype(BF16), vn_ref[sq].astype(BF16), preferred_element_type=F32)) / l
        o_a = jnp.zeros((ts, GROUP_W), F32)
        for hh in range(A_HEADS):
            d = o[2 * hh * ts:(2 * hh + 1) * ts] - lam * o[(2 * hh + 1) * ts:(2 * hh + 2) * ts]
            o_a = o_a + jnp.where(head == hh, d, 0.0)
        o_a = _head_rms(o_a, e_ref, ang_ref[0] * (1.0 - _lam_init(layer))) * ga_ref[sq]
        o_ref[sq] = o_a.astype(BF16)


def _sample_attn(layer, page_table, cache_k, cache_v, q_s, k_new, v_new, ga_s, lam_qk, ang, eones, ts):
    bs, n_pages = page_table.shape
    page = cache_k.shape[3]
    nsq = SEQ_PER_STEP
    assert bs % nsq == 0
    kernel = functools.partial(_sample_attn_kernel, layer, n_pages, nsq, ts)

    def page_spec(sq, j):
        return pl.BlockSpec((None, None, GROUP_W, page), lambda g, pt: (layer, pt[g * nsq + sq, j], 0, 0))

    pages = [page_spec(sq, j) for sq in range(nsq) for j in range(n_pages)]
    seq_blk = pl.BlockSpec((nsq, ts, GROUP_W), lambda g, pt: (g, 0, 0))
    lsel3 = lambda g, pt: (layer, 0, 0)
    grid_spec = pltpu.PrefetchScalarGridSpec(
        num_scalar_prefetch=1,
        grid=(bs // nsq,),
        in_specs=(pages + pages
                  + [seq_blk, seq_blk, seq_blk, seq_blk,
                     pl.BlockSpec((1, 4, A_HALF), lsel3),
                     pl.BlockSpec((1, 1, GROUP_W), lsel3),
                     pl.BlockSpec(eones.shape, lambda b, pt: (0, 0))]),
        out_specs=seq_blk,
    )
    shp3 = (bs, ts, GROUP_W)
    return pl.pallas_call(
        kernel,
        out_shape=jax.ShapeDtypeStruct(shp3, BF16),
        grid_spec=grid_spec,
        compiler_params=pltpu.CompilerParams(dimension_semantics=("arbitrary",), vmem_limit_bytes=VMEM_LIMIT),
        name=f"sample_attn_l{layer}",
    )(page_table, *([cache_k] * (nsq * n_pages)), *([cache_v] * (nsq * n_pages)),
      q_s.reshape(shp3), k_new.reshape(shp3), v_new.reshape(shp3), ga_s.reshape(shp3), lam_qk, ang, eones)


def _sample_out_kernel(depth, ts, oa_ref, obcd_ref, x_ref, mod_ref, wout_ref, lng_ref, lnb_ref, y_ref):
    nseq = ROWS // ts
    mixed = (jnp.dot(oa_ref[...], wout_ref[0, 0:GROUP_W, :], preferred_element_type=F32)
             + jnp.dot(obcd_ref[...], wout_ref[0, GROUP_W:, :], preferred_element_type=F32))
    gate = mod_ref[...][:, :, 2 * D_MODEL:]
    y = _alpha(depth) * x_ref[...] + gate * mixed.reshape(nseq, ts, D_MODEL)
    y_ref[...] = _layer_norm(y, lng_ref[0], lnb_ref[0])


def _sample_out(layer, depth, oa, obcd, x, mod_s, wout_b, lng, lnb):
    bs, ts, _ = x.shape
    nseq = ROWS // ts
    lsel3 = lambda i: (layer, 0, 0)
    return pl.pallas_call(
        functools.partial(_sample_out_kernel, depth, ts),
        out_shape=jax.ShapeDtypeStruct(x.shape, F32),
        grid=(bs // nseq,),
        in_specs=[
            pl.BlockSpec((ROWS, GROUP_W), lambda i: (i, 0)),
            pl.BlockSpec((ROWS, 3 * GROUP_W), lambda i: (i, 0)),
            pl.BlockSpec((nseq, ts, D_MODEL), lambda i: (i, 0, 0)),
            pl.BlockSpec((nseq, 1, 3 * D_MODEL), lambda i: (i, 0, 0)),
            pl.BlockSpec((1, D_MODEL, D_MODEL), lsel3),
            pl.BlockSpec((1, 1, D_MODEL), lsel3),
            pl.BlockSpec((1, 1, D_MODEL), lsel3),
        ],
        out_specs=pl.BlockSpec((nseq, ts, D_MODEL), lambda i: (i, 0, 0)),
        compiler_params=pltpu.CompilerParams(dimension_semantics=("arbitrary",), vmem_limit_bytes=VMEM_LIMIT),
        name=f"sample_out_l{layer}",
    )(oa, obcd, x, mod_s, wout_b, lng, lnb)


def _rope_tables(pos):
    half = A_HALF // 2
    inv = ROPE_THETA ** (-jnp.arange(half, dtype=F32) * 2.0 / A_HALF)
    ang = pos.astype(F32)[:, None] * inv[None, :]
    cos = jnp.cos(ang)
    sin = jnp.sin(ang)
    return jnp.tile(jnp.concatenate([cos, cos], -1), (1, 4)), jnp.tile(jnp.concatenate([-sin, sin], -1), (1, 4))


def kernel(x_prompt, x_sample, cache_k, cache_v, state_conv, state_hgrn, page_table, c_prompt, c_sample, w_ada, b_ada, w_in, lam_qk, attn_norm_g, sg_norm_g, sg_norm_b, w_s, b_s, conv_w, conv_b, conv_norm_g, conv_norm_b, w_pw, lower_bounds, hgrn_norm_g, w_out, ln_g, ln_b):
    depth = w_in.shape[0]
    bp, t, _ = x_prompt.shape
    bs, ts, _ = x_sample.shape
    n_pool, page = cache_k.shape[1], cache_k.shape[2]
    past_len = page_table.shape[1] * page
    assert ts == 8 and ROWS % ts == 0 and bs % (ROWS // ts) == 0 and t % ROWS == 0
    nseq = ROWS // ts

    gmat = _segment_matrices()
    eones = _head_ones()
    win_b = w_in.astype(BF16)
    wout_b = w_out.astype(BF16)
    wpw_b = w_pw.astype(BF16)
    row3 = lambda a: a.reshape(depth, 1, a.shape[-1])
    sgg, sgb, cb, cng, cnb = row3(sg_norm_g), row3(sg_norm_b), row3(conv_b), row3(conv_norm_g), row3(conv_norm_b)
    lng, lnb = row3(ln_g), row3(ln_b)
    ang = row3(jnp.tile(attn_norm_g, (1, A_HEADS)))
    hng = row3(jnp.tile(hgrn_norm_g, (1, D_HEADS)))
    sgbias_p = jnp.repeat(jnp.swapaxes(b_s, 1, 2), GROUP_W // 4, axis=2)
    sgbias_s = jnp.tile(sgbias_p[:, :ts], (1, nseq, 1))
    eye = jnp.eye(nseq, dtype=F32)
    ws_blk = jnp.einsum('ab,lgts->lgatbs', eye, w_s[:, :, :ts, :ts]).reshape(depth, 4, ROWS, ROWS)
    cos_p, sin_p = _rope_tables(jnp.arange(t))
    cos_s, sin_s = _rope_tables(past_len + jnp.arange(ts))
    cos_s, sin_s = jnp.tile(cos_s, (nseq, 1)), jnp.tile(sin_s, (nseq, 1))
    ck = jnp.transpose(cache_k, (0, 1, 3, 4, 2)).reshape(depth, n_pool, GROUP_W, page)
    cv = jnp.transpose(cache_v, (0, 1, 3, 4, 2)).reshape(depth, n_pool, GROUP_W, page)

    rows = bp + bs
    pad = (-rows) % 8
    c_all = jnp.concatenate([c_prompt, c_sample, jnp.zeros((pad, D_MODEL), F32)], axis=0)
    mod = _ada(c_all, w_ada, b_ada)

    xp, xs = x_prompt, x_sample
    outs = [[] for _ in range(9)]
    for l in range(depth):
        mod_p = mod[l, :bp].reshape(bp, 1, 3 * D_MODEL)
        mod_s = mod[l, bp:bp + bs].reshape(bs, 1, 3 * D_MODEL)
        krow, vrow, qt, kb, vt, ga, obcd, cst_p, hst_p = _prompt_pre(
            l, xp, mod_p, win_b, cos_p, sin_p, sgg, sgb, w_s, sgbias_p, conv_w, cb, cng, cnb, wpw_b, lower_bounds,
            hng, gmat, eones)
        xp = _prompt_attn(l, depth, qt, kb, vt, ga, obcd, xp, mod_p, wout_b, lam_qk, ang, lng, lnb)

        krow_s, vrow_s, q_s, ga_s, obcd_s, chv_s, cst_s, hst_s = _sample_pre(
            l, xs, mod_s, win_b, cos_s, sin_s, sgg, sgb, ws_blk, sgbias_s, conv_w, cb, cng, cnb, wpw_b,
            lower_bounds, hng, gmat, eones, state_conv, state_hgrn)
        oa_s = _sample_attn(l, page_table, ck, cv, q_s, krow_s, vrow_s, ga_s, lam_qk, ang, eones, ts)
        xs = _sample_out(l, depth, oa_s.reshape(bs * ts, GROUP_W), obcd_s, xs, mod_s, wout_b, lng, lnb)

        outs[0].append(jnp.transpose(krow.reshape(bp, A_HEADS, 2 * A_HALF, t), (0, 3, 1, 2)))
        outs[1].append(jnp.transpose(vrow.reshape(bp, A_HEADS, A_VDIM, t), (0, 3, 1, 2)))
        outs[2].append(krow_s.reshape(bs, ts, A_HEADS, 2 * A_HALF))
        outs[3].append(vrow_s.reshape(bs, ts, A_HEADS, A_VDIM))
        outs[4].append(chv_s.reshape(bs, ts, GROUP_W))
        outs[5].append(cst_p)
        outs[6].append(cst_s[0])
        outs[7].append(hst_p)
        outs[8].append(hst_s[0])
    return (xp, xs) + tuple(jnp.stack(o) for o in outs)
```

```python
import functools
import math

import jax
import jax.numpy as jnp
import numpy as np
from jax import lax
from jax.experimental import pallas as pl
from jax.experimental.pallas import tpu as pltpu

F32 = jnp.float32
BF16 = jnp.bfloat16

D_MODEL = 1024
GROUP_W = 256
A_HEADS = 4
A_HALF = 32
A_VDIM = 64
ROPE_THETA = 10000.0
SG_CHUNK = 128
CONV_W = 31
D_HEADS = 4
D_KDIM = 64
F_FLOOR = 1e-30
EPS = 1e-5
NEG_BIG = -1e30
D_IN = 14 * GROUP_W
SQRT_HALF = 0.7071067811865476
Q_SCALE = (A_HALF ** -0.5) * math.log2(math.e)

ROWS = 128
SEQ_PER_STEP = 1
ATTN_TQ = 512
ATTN_TK = 512
LANE_TILE = 256
VMEM_LIMIT = 56 * 1024 * 1024

G_CUM8, G_SUF8, G_LVL0, G_CUM, G_SUF = 0, 1, 2, 9, 10

_NT = (((1,), (1,)), ((), ()))
_TN = (((0,), (0,)), ((), ()))


def _segment_matrices():
    t = np.arange(ROWS)[:, None]
    s = np.arange(ROWS)[None, :]
    same8 = (t >> 3) == (s >> 3)
    mats = [same8 & (s <= t), same8 & (s > t)]
    for lvl in range(7):
        mid = ((t >> (lvl + 1)) << (lvl + 1)) + (1 << lvl)
        second = ((t >> lvl) & 1) == 1
        mats.append(np.where(second, (s >= mid) & (s <= t), (s > t) & (s < mid)))
    mats.append(s <= t)
    mats.append(s > t)
    return jnp.asarray(np.stack(mats).astype(np.float32), dtype=BF16)


def _head_ones():
    h = np.arange(GROUP_W) // 64
    return jnp.asarray((h[:, None] == h[None, :]).astype(np.float32), dtype=BF16)


def _silu(x):
    return x * jax.nn.sigmoid(x)


def _gelu(x):
    return 0.5 * x * (1.0 + lax.erf(x * SQRT_HALF))


def _layer_norm(x, g, b):
    xc = x - jnp.mean(x, axis=-1, keepdims=True)
    var = jnp.mean(xc * xc, axis=-1, keepdims=True)
    return xc * lax.rsqrt(var + EPS) * g + b


def _head_rms(x, e_ref, g):
    ms = jnp.dot((x * x).astype(BF16), e_ref[...], preferred_element_type=F32) * (1.0 / 64.0)
    return x * lax.rsqrt(ms + EPS) * g


def _rope(x, cos, sin):
    outs = []
    for half in range(2):
        xh = x[:, 128 * half:128 * half + 128]
        lane = lax.broadcasted_iota(jnp.int32, xh.shape, 1)
        partner = jnp.where((lane & 16) == 0, pltpu.roll(xh, 112, 1), pltpu.roll(xh, 16, 1))
        outs.append(xh * cos + partner * sin)
    return jnp.concatenate(outs, axis=1)


def _lower_bound(lb_ref, layer):
    lb = lb_ref[...]
    e = jnp.exp(lb - jnp.max(lb, axis=0, keepdims=True))
    soft = e / jnp.sum(e, axis=0, keepdims=True)
    acc = jnp.zeros((1, GROUP_W), F32)
    for i in range(1, layer + 1):
        acc = acc + soft[i:i + 1]
    return acc


def _lam(lam_ref):
    lp = lam_ref[0]
    a = jnp.sum(lp[0:1] * lp[1:2], axis=-1, keepdims=True)
    b = jnp.sum(lp[2:3] * lp[3:4], axis=-1, keepdims=True)
    return jnp.exp(a) - jnp.exp(b)


def _lam_init(layer):
    return 0.8 - 0.6 * math.exp(-0.3 * layer)


def _alpha(depth):
    return (2.0 * depth) ** 0.25


def _sg_mix(vn, ws_ref, bias):
    group = lax.broadcasted_iota(jnp.int32, (1, GROUP_W), 1) >> 6
    r = lax.broadcasted_iota(jnp.int32, (ROWS, ROWS), 0)
    c = lax.broadcasted_iota(jnp.int32, (ROWS, ROWS), 1)
    acc = bias
    for g in range(4):
        wm = jnp.where(r >= c, ws_ref[g], 0.0).astype(BF16)
        vm = jnp.where(group == g, vn, 0.0).astype(BF16)
        acc = acc + jnp.dot(wm, vm, preferred_element_type=F32)
    return acc


def _hgrn_gates(dq, df, di, lb):
    f = lb + (1.0 - lb) * jax.nn.sigmoid(df)
    lf = jnp.log(jnp.maximum(f, F_FLOOR))
    hi = lf.astype(BF16)
    lo = (lf - hi.astype(F32)).astype(BF16)
    return _silu(dq), 1.0 - f, di, jnp.concatenate([hi, lo], axis=1)


def _segsums(g_ref, lo, hi, hilo):
    g = g_ref[lo:hi].reshape((hi - lo) * ROWS, ROWS)
    r = jnp.dot(g, hilo, preferred_element_type=F32)
    return r[:, :GROUP_W] + r[:, GROUP_W:]


def _hgrn_intra(qd, kk, vd, seg, e_ref, levels):
    row = lax.broadcasted_iota(jnp.int32, (ROWS, 1), 0)
    head = lax.broadcasted_iota(jnp.int32, (1, GROUP_W), 1) >> 6
    t_idx = lax.broadcasted_iota(jnp.int32, (ROWS, 4 * ROWS), 0)
    s_idx = lax.broadcasted_iota(jnp.int32, (ROWS, 4 * ROWS), 1) & (ROWS - 1)
    att = jnp.zeros((ROWS, 4 * ROWS), F32)
    for i, lvl in enumerate(levels):
        e = jnp.exp(seg[i])
        second = ((row >> lvl) & 1) == 1
        a = jnp.where(second, qd * e, 0.0).astype(BF16)
        b = jnp.where(second, 0.0, kk * e)
        b_heads = jnp.concatenate([jnp.where(head == h, b, 0.0) for h in range(4)], axis=0).astype(BF16)
        s = lax.dot_general(a, b_heads, _NT, preferred_element_type=F32)
        att = att + jnp.where((t_idx >> (lvl + 1)) == (s_idx >> (lvl + 1)), s, 0.0)
    v_stack = jnp.concatenate([jnp.where(head == h, vd, 0.0) for h in range(4)], axis=0).astype(BF16)
    o = jnp.dot(att.astype(BF16), v_stack, preferred_element_type=F32)
    diag = jnp.dot((qd * kk).astype(BF16), e_ref[...], preferred_element_type=F32)
    return o + diag * vd


def _ada_kernel(c_ref, w_ref, b_ref, o_ref):
    c = c_ref[...]
    a = _silu(c).astype(BF16)
    o_ref[0] = jnp.dot(a, w_ref[0].astype(BF16), preferred_element_type=F32) + b_ref[0]


def _ada(c_all, w_ada, b_ada):
    depth = w_ada.shape[0]
    rows = c_all.shape[0]
    return pl.pallas_call(
        _ada_kernel,
        out_shape=jax.ShapeDtypeStruct((depth, rows, 3 * D_MODEL), F32),
        grid=(depth, 3),
        in_specs=[
            pl.BlockSpec((rows, D_MODEL), lambda l, n: (0, 0)),
            pl.BlockSpec((1, D_MODEL, D_MODEL), lambda l, n: (l, 0, n)),
            pl.BlockSpec((1, 1, D_MODEL), lambda l, n: (l, 0, n)),
        ],
        out_specs=pl.BlockSpec((1, rows, D_MODEL), lambda l, n: (l, 0, n)),
        compiler_params=pltpu.CompilerParams(dimension_semantics=("arbitrary", "arbitrary"),
                                             vmem_limit_bytes=VMEM_LIMIT),
        name="adaln",
    )(c_all, w_ada, b_ada.reshape(depth, 1, 3 * D_MODEL))


def _prompt_pre_kernel(layer, tb,
                       x_ref, mod_ref, win_ref, cos_ref, sin_ref, sgg_ref, sgb_ref, ws_ref, sgbias_ref,
                       cw_ref, cb_ref, cng_ref, cnb_ref, wpw_ref, lb_ref, hng_ref, g_ref, e_ref,
                       krow_ref, vrow_ref, qt_ref, kb_ref, vt_ref, ga_ref, obcd_ref, cst_ref, hst_ref,
                       hc_ref, s_ref):
    i = pl.program_id(1)

    @pl.when(i == 0)
    def _():
        hc_ref[0:32, :] = jnp.zeros((32, GROUP_W), F32)
        s_ref[...] = jnp.zeros(s_ref.shape, F32)

    mod = mod_ref[0]
    shift = mod[:, 0:D_MODEL]
    scale = mod[:, D_MODEL:2 * D_MODEL]
    h = (x_ref[0] * (1.0 + scale) + shift).astype(BF16)
    zc = jnp.dot(h, win_ref[0, :, 1792:2560], preferred_element_type=F32)
    zb = jnp.dot(h, win_ref[0, :, 1024:1792], preferred_element_type=F32)
    zd = jnp.dot(h, win_ref[0, :, 2560:3584], preferred_element_type=F32)
    za = jnp.dot(h, win_ref[0, :, 0:1024], preferred_element_type=F32)

    cos = cos_ref[...]
    sin = sin_ref[...]
    q = _rope(za[:, 0:256], cos, sin)
    k = _rope(za[:, 256:512], cos, sin)
    v = za[:, 512:768]
    kt = k.T
    vt = v.T
    krow_ref[0] = kt
    vrow_ref[0] = vt
    kb_ref[0] = k.astype(BF16)
    qt_ref[0] = (q * Q_SCALE).T.astype(BF16)
    vt_ref[0] = vt.astype(BF16)
    ga_ref[0] = _silu(za[:, 768:1024])

    glu = zc[:, 0:GROUP_W] * jax.nn.sigmoid(zc[:, GROUP_W:2 * GROUP_W])
    hc_ref[32:32 + tb, :] = glu
    y = jnp.zeros((tb, GROUP_W), F32)
    for j in range(CONV_W):
        y = y + hc_ref[pl.ds(2 + j, tb), :] * cw_ref[0, j:j + 1, :]
    cst_ref[0] = hc_ref[pl.ds(tb + 2, CONV_W - 1), :]
    hc_ref[0:32, :] = hc_ref[pl.ds(tb, 32), :]
    yn = _silu(_layer_norm(y + cb_ref[0], cng_ref[0], cnb_ref[0]))
    o_c = jnp.dot(yn.astype(BF16), wpw_ref[0], preferred_element_type=F32) * _silu(zc[:, 512:768])

    lb = _lower_bound(lb_ref, layer)
    o_b_parts, o_d_parts = [], []
    for c in range(tb // ROWS):
        rs = slice(c * ROWS, (c + 1) * ROWS)
        vn = _layer_norm(_gelu(zb[rs, 256:512]), sgg_ref[0], sgb_ref[0])
        mixed = _sg_mix(vn, ws_ref.at[0], sgbias_ref[0])
        o_b_parts.append(_gelu(zb[rs, 0:256]) * mixed * _silu(zb[rs, 512:768]))
        qd, kk, vd, hilo = _hgrn_gates(zd[rs, 0:256], zd[rs, 256:512], zd[rs, 512:768], lb)
        seg = _segsums(g_ref, G_LVL0, G_SUF + 1, hilo)
        o = _hgrn_intra(qd, kk, vd, [seg[ROWS * n:ROWS * (n + 1)] for n in range(7)], e_ref, range(7))
        aq = qd * jnp.exp(seg[7 * ROWS:8 * ROWS])
        bk = kk * jnp.exp(seg[8 * ROWS:9 * ROWS])
        ones = jnp.ones((ROWS, D_KDIM), BF16)
        inter = []
        for hh in range(D_HEADS):
            lo_, hi_ = 64 * hh, 64 * hh + 64
            st = s_ref[hh]
            inter.append(jnp.dot(aq[:, lo_:hi_].astype(BF16), st.astype(BF16), preferred_element_type=F32))
            dec = (lax.dot_general(hilo[:, lo_:hi_], ones, _TN, preferred_element_type=F32)
                   + lax.dot_general(hilo[:, GROUP_W + lo_:GROUP_W + hi_], ones, _TN, preferred_element_type=F32))
            upd = lax.dot_general(bk[:, lo_:hi_].astype(BF16), vd[:, lo_:hi_].astype(BF16), _TN,
                                  preferred_element_type=F32)
            s_ref[hh] = jnp.exp(dec) * st + upd
        o = o + jnp.concatenate(inter, axis=1)
        o_d_parts.append(_head_rms(o, e_ref, hng_ref[0]) * _silu(zd[rs, 768:1024]))
    o_b = jnp.concatenate(o_b_parts, axis=0)
    o_d = jnp.concatenate(o_d_parts, axis=0)
    obcd_ref[0] = jnp.concatenate([o_b, o_c, o_d], axis=1).astype(BF16)
    hst_ref[0] = s_ref[...]


def _prompt_pre(layer, x, mod_p, win_b, cos, sin, sgg, sgb, ws, sgbias, cw, cb, cng, cnb, wpw_b, lbnd, hng,
                gmat, eones):
    b, t, _ = x.shape
    tb = min(256, t)
    kernel = functools.partial(_prompt_pre_kernel, layer, tb)
    lsel3 = lambda bb, i: (layer, 0, 0)
    const2 = lambda bb, i: (0, 0)
    const3 = lambda bb, i: (0, 0, 0)
    row_blk = lambda w: pl.BlockSpec((1, tb, w), lambda bb, i: (bb, i, 0))
    col_blk = pl.BlockSpec((1, GROUP_W, tb), lambda bb, i: (bb, 0, i))
    out_shape = (
        jax.ShapeDtypeStruct((b, GROUP_W, t), F32),
        jax.ShapeDtypeStruct((b, GROUP_W, t), F32),
        jax.ShapeDtypeStruct((b, GROUP_W, t), BF16),
        jax.ShapeDtypeStruct((b, t, GROUP_W), BF16),
        jax.ShapeDtypeStruct((b, GROUP_W, t), BF16),
        jax.ShapeDtypeStruct((b, t, GROUP_W), F32),
        jax.ShapeDtypeStruct((b, t, 3 * GROUP_W), BF16),
        jax.ShapeDtypeStruct((b, CONV_W - 1, GROUP_W), F32),
        jax.ShapeDtypeStruct((b, D_HEADS, D_KDIM, D_KDIM), F32),
    )
    return pl.pallas_call(
        kernel,
        out_shape=out_shape,
        grid=(b, t // tb),
        in_specs=[
            row_blk(D_MODEL),
            pl.BlockSpec((1, 1, 3 * D_MODEL), lambda bb, i: (bb, 0, 0)),
            pl.BlockSpec((1, D_MODEL, D_IN), lsel3),
            pl.BlockSpec((tb, 128), lambda bb, i: (i, 0)),
            pl.BlockSpec((tb, 128), lambda bb, i: (i, 0)),
            pl.BlockSpec((1, 1, GROUP_W), lsel3),
            pl.BlockSpec((1, 1, GROUP_W), lsel3),
            pl.BlockSpec((1, 4, ROWS, ROWS), lambda bb, i: (layer, 0, 0, 0)),
            pl.BlockSpec((1, ROWS, GROUP_W), lsel3),
            pl.BlockSpec((1, CONV_W, GROUP_W), lsel3),
            pl.BlockSpec((1, 1, GROUP_W), lsel3),
            pl.BlockSpec((1, 1, GROUP_W), lsel3),
            pl.BlockSpec((1, 1, GROUP_W), lsel3),
            pl.BlockSpec((1, GROUP_W, GROUP_W), lsel3),
            pl.BlockSpec(lbnd.shape, const2),
            pl.BlockSpec((1, 1, GROUP_W), lsel3),
            pl.BlockSpec(gmat.shape, const3),
            pl.BlockSpec(eones.shape, const2),
        ],
        out_specs=(
            col_blk, col_blk, col_blk, row_blk(GROUP_W), col_blk, row_blk(GROUP_W),
            row_blk(3 * GROUP_W),
            pl.BlockSpec((1, CONV_W - 1, GROUP_W), lambda bb, i: (bb, 0, 0)),
            pl.BlockSpec((1, D_HEADS, D_KDIM, D_KDIM), lambda bb, i: (bb, 0, 0, 0)),
        ),
        scratch_shapes=[pltpu.VMEM((32 + tb, GROUP_W), F32), pltpu.VMEM((D_HEADS, D_KDIM, D_KDIM), F32)],
        compiler_params=pltpu.CompilerParams(dimension_semantics=("arbitrary", "arbitrary"),
                                             vmem_limit_bytes=VMEM_LIMIT),
        name=f"prompt_pre_l{layer}",
    )(x, mod_p, win_b, cos, sin, sgg, sgb, ws, sgbias, cw, cb, cng, cnb, wpw_b, lbnd, hng, gmat, eones)


def _prompt_attn_kernel(layer, depth, tq, tk,
                        qt_ref, kb_ref, vt_ref, ga_ref, obcd_ref, x_ref, mod_ref, wout_ref, lam_ref, ang_ref,
                        lng_ref, lnb_ref, y_ref, acc_ref, m_ref, qm_ref, s_ref):
    i = pl.program_id(1)
    qt = qt_ref[0]
    rowg = lax.broadcasted_iota(jnp.int32, (GROUP_W, 1), 0) >> 5
    for j in range(8):
        qm_ref[j] = jnp.where(rowg == j, qt, jnp.zeros_like(qt))
    m_ref[...] = jnp.full(m_ref.shape, NEG_BIG, F32)
    acc_ref[...] = jnp.zeros(acc_ref.shape, F32)
    ones = jnp.ones((16, tk), BF16)
    kpq = tq // tk

    def tile(kt, diag):
        masked = diag is not None
        off = pl.multiple_of(kt * tk, tk)
        kk = kb_ref[0, pl.ds(off, tk), :]
        if masked:
            key = lax.broadcasted_iota(jnp.int32, (tk, tq), 0) + diag * tk
            qry = lax.broadcasted_iota(jnp.int32, (tk, tq), 1)
            visible = key <= qry
        cmax = []
        for j in range(8):
            s = jnp.dot(kk, qm_ref[j], preferred_element_type=F32)
            if masked:
                s = jnp.where(visible, s, NEG_BIG)
            s_ref[j] = s
            cmax.append(jnp.max(s, axis=0, keepdims=True))
        vvs = [jnp.concatenate([vt_ref[0, 64 * hh:64 * hh + 64, pl.ds(off, tk)], ones], axis=0)
               for hh in range(A_HEADS)]
        for j in range(8):
            m_old = m_ref[j]
            m_new = jnp.maximum(m_old, cmax[j])
            alpha = jnp.exp2(m_old - m_new)
            m_ref[j] = m_new
            for c in range(tq // LANE_TILE):
                cols = slice(LANE_TILE * c, LANE_TILE * (c + 1))
                p = jnp.exp2(s_ref[j, :, cols] - m_new[:, cols]).astype(BF16)
                acc_ref[j, :, cols] = (alpha[:, cols] * acc_ref[j, :, cols]
                                       + jnp.dot(vvs[j // 2], p, preferred_element_type=F32))

    def body(kt, carry):
        tile(kt, None)
        return carry

    lax.fori_loop(0, i * kpq, body, 0)
    for dg in range(kpq):
        tile(i * kpq + dg, dg)

    lam = _lam(lam_ref) + _lam_init(layer)
    heads = []
    for hh in range(A_HEADS):
        a0 = acc_ref[2 * hh]
        a1 = acc_ref[2 * hh + 1]
        d = a0[0:A_VDIM] / a0[A_VDIM:A_VDIM + 1] - lam * (a1[0:A_VDIM] / a1[A_VDIM:A_VDIM + 1])
        ms = jnp.mean(d * d, axis=0, keepdims=True)
        heads.append(d * lax.rsqrt(ms + EPS))
    o_a = jnp.concatenate(heads, axis=0).T
    o_a = o_a * (ang_ref[0] * (1.0 - _lam_init(layer))) * ga_ref[0]
    mixed = (jnp.dot(o_a.astype(BF16), wout_ref[0, 0:GROUP_W, :], preferred_element_type=F32)
             + jnp.dot(obcd_ref[0], wout_ref[0, GROUP_W:, :], preferred_element_type=F32))
    gate = mod_ref[0][:, 2 * D_MODEL:]
    y_ref[0] = _layer_norm(_alpha(depth) * x_ref[0] + gate * mixed, lng_ref[0], lnb_ref[0])


def _prompt_attn(layer, depth, qt, kb, vt, ga, obcd, x, mod_p, wout_b, lam_qk, ang, lng, lnb):
    b, t, _ = x.shape
    tq = min(ATTN_TQ, t)
    tk = min(ATTN_TK, tq)
    kernel = functools.partial(_prompt_attn_kernel, layer, depth, tq, tk)
    lsel3 = lambda bb, i: (layer, 0, 0)
    row_blk = lambda w: pl.BlockSpec((1, tq, w), lambda bb, i: (bb, i, 0))
    return pl.pallas_call(
        kernel,
        out_shape=jax.ShapeDtypeStruct((b, t, D_MODEL), F32),
        grid=(b, t // tq),
        in_specs=[
            pl.BlockSpec((1, GROUP_W, tq), lambda bb, i: (bb, 0, i)),
            pl.BlockSpec((1, t, GROUP_W), lambda bb, i: (bb, 0, 0)),
            pl.BlockSpec((1, GROUP_W, t), lambda bb, i: (bb, 0, 0)),
            row_blk(GROUP_W),
            row_blk(3 * GROUP_W),
            row_blk(D_MODEL),
            pl.BlockSpec((1, 1, 3 * D_MODEL), lambda bb, i: (bb, 0, 0)),
            pl.BlockSpec((1, D_MODEL, D_MODEL), lsel3),
            pl.BlockSpec((1, 4, A_HALF), lsel3),
            pl.BlockSpec((1, 1, GROUP_W), lsel3),
            pl.BlockSpec((1, 1, D_MODEL), lsel3),
            pl.BlockSpec((1, 1, D_MODEL), lsel3),
        ],
        out_specs=row_blk(D_MODEL),
        scratch_shapes=[pltpu.VMEM((8, A_VDIM + 16, tq), F32), pltpu.VMEM((8, 1, tq), F32),
                        pltpu.VMEM((8, GROUP_W, tq), BF16), pltpu.VMEM((8, tk, tq), F32)],
        compiler_params=pltpu.CompilerParams(dimension_semantics=("arbitrary", "arbitrary"),
                                             vmem_limit_bytes=VMEM_LIMIT),
        name=f"prompt_attn_l{layer}",
    )(qt, kb, vt, ga, obcd, x, mod_p, wout_b, lam_qk, ang, lng, lnb)


def _sample_pre_kernel(layer, ts,
                       x_ref, mod_ref, win_ref, cos_ref, sin_ref, sgg_ref, sgb_ref, ws_ref, sgbias_ref,
                       cw_ref, cb_ref, cng_ref, cnb_ref, wpw_ref, lb_ref, hng_ref, g_ref, e_ref, cst_ref, hst_ref,
                       krow_ref, vrow_ref, q_ref, ga_ref, obcd_ref, chv_ref, ncst_ref, nhst_ref,
                       hc_ref):
    nseq = ROWS // ts
    mod = mod_ref[...]
    shift = mod[:, :, 0:D_MODEL]
    scale = mod[:, :, D_MODEL:2 * D_MODEL]
    h = (x_ref[...] * (1.0 + scale) + shift).reshape(ROWS, D_MODEL).astype(BF16)
    z = jnp.dot(h, win_ref[0], preferred_element_type=F32)

    cos = cos_ref[...]
    sin = sin_ref[...]
    k = _rope(z[:, 256:512], cos, sin)
    krow_ref[...] = k
    vrow_ref[...] = z[:, 512:768]
    q_ref[...] = _rope(z[:, 0:256], cos, sin) * Q_SCALE
    ga_ref[...] = _silu(z[:, 768:1024])

    vn = _layer_norm(_gelu(z[:, 1280:1536]), sgg_ref[0], sgb_ref[0])
    chv_ref[...] = vn
    o_b = _gelu(z[:, 1024:1280]) * _sg_mix(vn, ws_ref.at[0], sgbias_ref[0]) * _silu(z[:, 1536:1792])

    a = z[:, 1792:2304]
    glu = a[:, :GROUP_W] * jax.nn.sigmoid(a[:, GROUP_W:])
    hc_ref[:, 0:CONV_W - 1, :] = cst_ref[0]
    hc_ref[:, CONV_W - 1:CONV_W - 1 + ts, :] = glu.reshape(nseq, ts, GROUP_W)
    y = jnp.zeros((nseq, ts, GROUP_W), F32)
    for j in range(CONV_W):
        y = y + hc_ref[:, j:j + ts, :] * cw_ref[0, j:j + 1, :]
    ncst_ref[0] = hc_ref[:, ts:ts + CONV_W - 1, :]
    yn = _silu(_layer_norm(y.reshape(ROWS, GROUP_W) + cb_ref[0], cng_ref[0], cnb_ref[0]))
    o_c = jnp.dot(yn.astype(BF16), wpw_ref[0], preferred_element_type=F32) * _silu(z[:, 2304:2560])

    lb = _lower_bound(lb_ref, layer)
    qd, kk, vd, hilo = _hgrn_gates(z[:, 2560:2816], z[:, 2816:3072], z[:, 3072:3328], lb)
    seg = _segsums(g_ref, G_CUM8, G_LVL0 + 3, hilo)
    o = _hgrn_intra(qd, kk, vd, [seg[ROWS * (2 + n):ROWS * (3 + n)] for n in range(3)], e_ref, range(3))
    aq = (qd * jnp.exp(seg[0:ROWS])).reshape(nseq, ts, GROUP_W)
    bk = (kk * jnp.exp(seg[ROWS:2 * ROWS])).reshape(nseq, ts, GROUP_W)
    v3 = vd.reshape(nseq, ts, GROUP_W)
    hilo3 = hilo.astype(F32).reshape(nseq, ts, 2 * GROUP_W)
    ones = jnp.ones((nseq, ts, D_KDIM), BF16)
    inter = []
    for hh in range(D_HEADS):
        lo_, hi_ = 64 * hh, 64 * hh + 64
        st = hst_ref[0, :, hh]
        inter.append(jnp.einsum('bqk,bkv->bqv', aq[:, :, lo_:hi_].astype(BF16), st.astype(BF16),
                                preferred_element_type=F32))
        dec = (jnp.einsum('bsk,bsv->bkv', hilo3[:, :, lo_:hi_].astype(BF16), ones, preferred_element_type=F32)
               + jnp.einsum('bsk,bsv->bkv', hilo3[:, :, GROUP_W + lo_:GROUP_W + hi_].astype(BF16), ones,
                            preferred_element_type=F32))
        upd = jnp.einsum('bsk,bsv->bkv', bk[:, :, lo_:hi_].astype(BF16), v3[:, :, lo_:hi_].astype(BF16),
                         preferred_element_type=F32)
        nhst_ref[0, :, hh] = jnp.exp(dec) * st + upd
    o = o + jnp.concatenate(inter, axis=2).reshape(ROWS, GROUP_W)
    o_d = _head_rms(o, e_ref, hng_ref[0]) * _silu(z[:, 3328:3584])
    obcd_ref[...] = jnp.concatenate([o_b, o_c, o_d], axis=1).astype(BF16)


def _sample_pre(layer, x, mod_s, win_b, cos, sin, sgg, sgb, ws_blk, sgbias, cw, cb, cng, cnb, wpw_b, lbnd, hng,
                gmat, eones, state_conv, state_hgrn):
    bs, ts, _ = x.shape
    nseq = ROWS // ts
    nblk = bs // nseq
    n = bs * ts
    kernel = functools.partial(_sample_pre_kernel, layer, ts)
    lsel3 = lambda i: (layer, 0, 0)
    const2 = lambda i: (0, 0)
    const3 = lambda i: (0, 0, 0)
    row_blk = lambda w: pl.BlockSpec((ROWS, w), lambda i: (i, 0))
    out_shape = (
        jax.ShapeDtypeStruct((n, GROUP_W), F32),
        jax.ShapeDtypeStruct((n, GROUP_W), F32),
        jax.ShapeDtypeStruct((n, GROUP_W), F32),
        jax.ShapeDtypeStruct((n, GROUP_W), F32),
        jax.ShapeDtypeStruct((n, 3 * GROUP_W), BF16),
        jax.ShapeDtypeStruct((n, GROUP_W), F32),
        jax.ShapeDtypeStruct((1, bs, CONV_W - 1, GROUP_W), F32),
        jax.ShapeDtypeStruct((1, bs, D_HEADS, D_KDIM, D_KDIM), F32),
    )
    return pl.pallas_call(
        kernel,
        out_shape=out_shape,
        grid=(nblk,),
        in_specs=[
            pl.BlockSpec((nseq, ts, D_MODEL), lambda i: (i, 0, 0)),
            pl.BlockSpec((nseq, 1, 3 * D_MODEL), lambda i: (i, 0, 0)),
            pl.BlockSpec((1, D_MODEL, D_IN), lsel3),
            pl.BlockSpec((ROWS, 128), const2),
            pl.BlockSpec((ROWS, 128), const2),
            pl.BlockSpec((1, 1, GROUP_W), lsel3),
            pl.BlockSpec((1, 1, GROUP_W), lsel3),
            pl.BlockSpec((1, 4, ROWS, ROWS), lambda i: (layer, 0, 0, 0)),
            pl.BlockSpec((1, ROWS, GROUP_W), lsel3),
            pl.BlockSpec((1, CONV_W, GROUP_W), lsel3),
            pl.BlockSpec((1, 1, GROUP_W), lsel3),
            pl.BlockSpec((1, 1, GROUP_W), lsel3),
            pl.BlockSpec((1, 1, GROUP_W), lsel3),
            pl.BlockSpec((1, GROUP_W, GROUP_W), lsel3),
            pl.BlockSpec(lbnd.shape, const2),
            pl.BlockSpec((1, 1, GROUP_W), lsel3),
            pl.BlockSpec(gmat.shape, const3),
            pl.BlockSpec(eones.shape, const2),
            pl.BlockSpec((1, nseq, CONV_W - 1, GROUP_W), lambda i: (layer, i, 0, 0)),
            pl.BlockSpec((1, nseq, D_HEADS, D_KDIM, D_KDIM), lambda i: (layer, i, 0, 0, 0)),
        ],
        out_specs=(
            row_blk(GROUP_W), row_blk(GROUP_W), row_blk(GROUP_W), row_blk(GROUP_W), row_blk(3 * GROUP_W),
            row_blk(GROUP_W),
            pl.BlockSpec((1, nseq, CONV_W - 1, GROUP_W), lambda i: (0, i, 0, 0)),
            pl.BlockSpec((1, nseq, D_HEADS, D_KDIM, D_KDIM), lambda i: (0, i, 0, 0, 0)),
        ),
        scratch_shapes=[pltpu.VMEM((nseq, 40, GROUP_W), F32)],
        compiler_params=pltpu.CompilerParams(dimension_semantics=("arbitrary",), vmem_limit_bytes=VMEM_LIMIT),
        name=f"sample_pre_l{layer}",
    )(x, mod_s, win_b, cos, sin, sgg, sgb, ws_blk, sgbias, cw, cb, cng, cnb, wpw_b, lbnd, hng, gmat, eones,
      state_conv, state_hgrn)


def _sample_attn_kernel(layer, n_pages, nsq, ts, pt_ref, *refs):
    k_pages = refs[0:nsq * n_pages]
    v_pages = refs[nsq * n_pages:2 * nsq * n_pages]
    q_ref, kn_ref, vn_ref, ga_ref, lam_ref, ang_ref, e_ref, o_ref = refs[2 * nsq * n_pages:]
    grp = lax.broadcasted_iota(jnp.int32, (1, GROUP_W), 1) >> 5
    head = lax.broadcasted_iota(jnp.int32, (1, GROUP_W), 1) >> 6
    t_q = lax.broadcasted_iota(jnp.int32, (8 * ts, ts), 0) & (ts - 1)
    t_k = lax.broadcasted_iota(jnp.int32, (8 * ts, ts), 1)
    vis = t_k <= t_q
    lam = _lam(lam_ref) + _lam_init(layer)
    for sq in range(nsq):
        q = q_ref[sq]
        qexp = jnp.concatenate([jnp.where(grp == j, q, 0.0) for j in range(8)], axis=0).astype(BF16)
        kt_all = jnp.concatenate([r[...] for r in k_pages[sq * n_pages:(sq + 1) * n_pages]], axis=1).astype(BF16)
        vt_all = jnp.concatenate([r[...] for r in v_pages[sq * n_pages:(sq + 1) * n_pages]], axis=1).astype(BF16)
        s_past = jnp.dot(qexp, kt_all, preferred_element_type=F32)
        s_new = lax.dot_general(qexp, kn_ref[sq].astype(BF16), _NT, preferred_element_type=F32)
        s_new = jnp.where(vis, s_new, NEG_BIG)
        m = jnp.maximum(jnp.max(s_past, axis=-1, keepdims=True), jnp.max(s_new, axis=-1, keepdims=True))
        p_past = jnp.exp2(s_past - m)
        p_new = jnp.where(vis, jnp.exp2(s_new - m), 0.0)
        l = jnp.sum(p_past, axis=-1, keepdims=True) + jnp.sum(p_new, axis=-1, keepdims=True)
        o = (lax.dot_general(p_past.astype(BF16), vt_all, _NT, preferred_element_type=F32)
             + jnp.dot(p_new.astype(BF16), vn_ref[sq].astype(BF16), preferred_element_type=F32)) / l
        o_a = jnp.zeros((ts, GROUP_W), F32)
        for hh in range(A_HEADS):
            d = o[2 * hh * ts:(2 * hh + 1) * ts] - lam * o[(2 * hh + 1) * ts:(2 * hh + 2) * ts]
            o_a = o_a + jnp.where(head == hh, d, 0.0)
        o_a = _head_rms(o_a, e_ref, ang_ref[0] * (1.0 - _lam_init(layer))) * ga_ref[sq]
        o_ref[sq] = o_a.astype(BF16)


def _sample_attn(layer, page_table, cache_k, cache_v, q_s, k_new, v_new, ga_s, lam_qk, ang, eones, ts):
    bs, n_pages = page_table.shape
    page = cache_k.shape[3]
    nsq = SEQ_PER_STEP
    assert bs % nsq == 0
    kernel = functools.partial(_sample_attn_kernel, layer, n_pages, nsq, ts)

    def page_spec(sq, j):
        return pl.BlockSpec((None, None, GROUP_W, page), lambda g, pt: (layer, pt[g * nsq + sq, j], 0, 0))

    pages = [page_spec(sq, j) for sq in range(nsq) for j in range(n_pages)]
    seq_blk = pl.BlockSpec((nsq, ts, GROUP_W), lambda g, pt: (g, 0, 0))
    lsel3 = lambda g, pt: (layer, 0, 0)
    grid_spec = pltpu.PrefetchScalarGridSpec(
        num_scalar_prefetch=1,
        grid=(bs // nsq,),
        in_specs=(pages + pages
                  + [seq_blk, seq_blk, seq_blk, seq_blk,
                     pl.BlockSpec((1, 4, A_HALF), lsel3),
                     pl.BlockSpec((1, 1, GROUP_W), lsel3),
                     pl.BlockSpec(eones.shape, lambda b, pt: (0, 0))]),
        out_specs=seq_blk,
    )
    shp3 = (bs, ts, GROUP_W)
    return pl.pallas_call(
        kernel,
        out_shape=jax.ShapeDtypeStruct(shp3, BF16),
        grid_spec=grid_spec,
        compiler_params=pltpu.CompilerParams(dimension_semantics=("arbitrary",), vmem_limit_bytes=VMEM_LIMIT),
        name=f"sample_attn_l{layer}",
    )(page_table, *([cache_k] * (nsq * n_pages)), *([cache_v] * (nsq * n_pages)),
      q_s.reshape(shp3), k_new.reshape(shp3), v_new.reshape(shp3), ga_s.reshape(shp3), lam_qk, ang, eones)


def _sample_out_kernel(depth, ts, oa_ref, obcd_ref, x_ref, mod_ref, wout_ref, lng_ref, lnb_ref, y_ref):
    nseq = ROWS // ts
    mixed = (jnp.dot(oa_ref[...], wout_ref[0, 0:GROUP_W, :], preferred_element_type=F32)
             + jnp.dot(obcd_ref[...], wout_ref[0, GROUP_W:, :], preferred_element_type=F32))
    gate = mod_ref[...][:, :, 2 * D_MODEL:]
    y = _alpha(depth) * x_ref[...] + gate * mixed.reshape(nseq, ts, D_MODEL)
    y_ref[...] = _layer_norm(y, lng_ref[0], lnb_ref[0])


def _sample_out(layer, depth, oa, obcd, x, mod_s, wout_b, lng, lnb):
    bs, ts, _ = x.shape
    nseq = ROWS // ts
    lsel3 = lambda i: (layer, 0, 0)
    return pl.pallas_call(
        functools.partial(_sample_out_kernel, depth, ts),
        out_shape=jax.ShapeDtypeStruct(x.shape, F32),
        grid=(bs // nseq,),
        in_specs=[
            pl.BlockSpec((ROWS, GROUP_W), lambda i: (i, 0)),
            pl.BlockSpec((ROWS, 3 * GROUP_W), lambda i: (i, 0)),
            pl.BlockSpec((nseq, ts, D_MODEL), lambda i: (i, 0, 0)),
            pl.BlockSpec((nseq, 1, 3 * D_MODEL), lambda i: (i, 0, 0)),
            pl.BlockSpec((1, D_MODEL, D_MODEL), lsel3),
            pl.BlockSpec((1, 1, D_MODEL), lsel3),
            pl.BlockSpec((1, 1, D_MODEL), lsel3),
        ],
        out_specs=pl.BlockSpec((nseq, ts, D_MODEL), lambda i: (i, 0, 0)),
        compiler_params=pltpu.CompilerParams(dimension_semantics=("arbitrary",), vmem_limit_bytes=VMEM_LIMIT),
        name=f"sample_out_l{layer}",
    )(oa, obcd, x, mod_s, wout_b, lng, lnb)


def _rope_tables(pos):
    half = A_HALF // 2
    inv = ROPE_THETA ** (-jnp.arange(half, dtype=F32) * 2.0 / A_HALF)
    ang = pos.astype(F32)[:, None] * inv[None, :]
    cos = jnp.cos(ang)
    sin = jnp.sin(ang)
    return jnp.tile(jnp.concatenate([cos, cos], -1), (1, 4)), jnp.tile(jnp.concatenate([-sin, sin], -1), (1, 4))


def kernel(x_prompt, x_sample, cache_k, cache_v, state_conv, state_hgrn, page_table, c_prompt, c_sample, w_ada, b_ada, w_in, lam_qk, attn_norm_g, sg_norm_g, sg_norm_b, w_s, b_s, conv_w, conv_b, conv_norm_g, conv_norm_b, w_pw, lower_bounds, hgrn_norm_g, w_out, ln_g, ln_b):
    depth = w_in.shape[0]
    bp, t, _ = x_prompt.shape
    bs, ts, _ = x_sample.shape
    n_pool, page = cache_k.shape[1], cache_k.shape[2]
    past_len = page_table.shape[1] * page
    assert ts == 8 and ROWS % ts == 0 and bs % (ROWS // ts) == 0 and t % ROWS == 0
    nseq = ROWS // ts

    gmat = _segment_matrices()
    eones = _head_ones()
    win_b = w_in.astype(BF16)
    wout_b = w_out.astype(BF16)
    wpw_b = w_pw.astype(BF16)
    row3 = lambda a: a.reshape(depth, 1, a.shape[-1])
    sgg, sgb, cb, cng, cnb = row3(sg_norm_g), row3(sg_norm_b), row3(conv_b), row3(conv_norm_g), row3(conv_norm_b)
    lng, lnb = row3(ln_g), row3(ln_b)
    ang = row3(jnp.tile(attn_norm_g, (1, A_HEADS)))
    hng = row3(jnp.tile(hgrn_norm_g, (1, D_HEADS)))
    sgbias_p = jnp.repeat(jnp.swapaxes(b_s, 1, 2), GROUP_W // 4, axis=2)
    sgbias_s = jnp.tile(sgbias_p[:, :ts], (1, nseq, 1))
    eye = jnp.eye(nseq, dtype=F32)
    ws_blk = jnp.einsum('ab,lgts->lgatbs', eye, w_s[:, :, :ts, :ts]).reshape(depth, 4, ROWS, ROWS)
    cos_p, sin_p = _rope_tables(jnp.arange(t))
    cos_s, sin_s = _rope_tables(past_len + jnp.arange(ts))
    cos_s, sin_s = jnp.tile(cos_s, (nseq, 1)), jnp.tile(sin_s, (nseq, 1))
    ck = jnp.transpose(cache_k, (0, 1, 3, 4, 2)).reshape(depth, n_pool, GROUP_W, page)
    cv = jnp.transpose(cache_v, (0, 1, 3, 4, 2)).reshape(depth, n_pool, GROUP_W, page)

    rows = bp + bs
    pad = (-rows) % 8
    c_all = jnp.concatenate([c_prompt, c_sample, jnp.zeros((pad, D_MODEL), F32)], axis=0)
    mod = _ada(c_all, w_ada, b_ada)

    xp, xs = x_prompt, x_sample
    outs = [[] for _ in range(9)]
    for l in range(depth):
        mod_p = mod[l, :bp].reshape(bp, 1, 3 * D_MODEL)
        mod_s = mod[l, bp:bp + bs].reshape(bs, 1, 3 * D_MODEL)
        krow, vrow, qt, kb, vt, ga, obcd, cst_p, hst_p = _prompt_pre(
            l, xp, mod_p, win_b, cos_p, sin_p, sgg, sgb, w_s, sgbias_p, conv_w, cb, cng, cnb, wpw_b, lower_bounds,
            hng, gmat, eones)
        xp = _prompt_attn(l, depth, qt, kb, vt, ga, obcd, xp, mod_p, wout_b, lam_qk, ang, lng, lnb)

        krow_s, vrow_s, q_s, ga_s, obcd_s, chv_s, cst_s, hst_s = _sample_pre(
            l, xs, mod_s, win_b, cos_s, sin_s, sgg, sgb, ws_blk, sgbias_s, conv_w, cb, cng, cnb, wpw_b,
            lower_bounds, hng, gmat, eones, state_conv, state_hgrn)
        oa_s = _sample_attn(l, page_table, ck, cv, q_s, krow_s, vrow_s, ga_s, lam_qk, ang, eones, ts)
        xs = _sample_out(l, depth, oa_s.reshape(bs * ts, GROUP_W), obcd_s, xs, mod_s, wout_b, lng, lnb)

        outs[0].append(jnp.transpose(krow.reshape(bp, A_HEADS, 2 * A_HALF, t), (0, 3, 1, 2)))
        outs[1].append(jnp.transpose(vrow.reshape(bp, A_HEADS, A_VDIM, t), (0, 3, 1, 2)))
        outs[2].append(krow_s.reshape(bs, ts, A_HEADS, 2 * A_HALF))
        outs[3].append(vrow_s.reshape(bs, ts, A_HEADS, A_VDIM))
        outs[4].append(chv_s.reshape(bs, ts, GROUP_W))
        outs[5].append(cst_p)
        outs[6].append(cst_s[0])
        outs[7].append(hst_p)
        outs[8].append(hst_s[0])
    return (xp, xs) + tuple(jnp.stack(o) for o in outs)
```

```python
import functools
import math

import jax
import jax.numpy as jnp
import numpy as np
from jax import lax
from jax.experimental import pallas as pl
from jax.experimental.pallas import tpu as pltpu

F32 = jnp.float32
BF16 = jnp.bfloat16

D_MODEL = 1024
GROUP_W = 256
A_HEADS = 4
A_HALF = 32
A_VDIM = 64
ROPE_THETA = 10000.0
SG_CHUNK = 128
CONV_W = 31
D_HEADS = 4
D_KDIM = 64
F_FLOOR = 1e-30
EPS = 1e-5
NEG_BIG = -1e30
D_IN = 14 * GROUP_W
SQRT_HALF = 0.7071067811865476
Q_SCALE = (A_HALF ** -0.5) * math.log2(math.e)

ROWS = 128
SEQ_PER_STEP = 1
PRE_TB = 512
ATTN_TQ = 512
ATTN_TK = 512
LANE_TILE = 256
VMEM_LIMIT = 56 * 1024 * 1024

G_CUM8, G_SUF8, G_LVL0, G_CUM, G_SUF = 0, 1, 2, 9, 10

_NT = (((1,), (1,)), ((), ()))
_TN = (((0,), (0,)), ((), ()))


def _segment_matrices():
    t = np.arange(ROWS)[:, None]
    s = np.arange(ROWS)[None, :]
    same8 = (t >> 3) == (s >> 3)
    mats = [same8 & (s <= t), same8 & (s > t)]
    for lvl in range(7):
        mid = ((t >> (lvl + 1)) << (lvl + 1)) + (1 << lvl)
        second = ((t >> lvl) & 1) == 1
        mats.append(np.where(second, (s >= mid) & (s <= t), (s > t) & (s < mid)))
    mats.append(s <= t)
    mats.append(s > t)
    return jnp.asarray(np.stack(mats).astype(np.float32), dtype=BF16)


def _head_ones():
    h = np.arange(GROUP_W) // 64
    return jnp.asarray((h[:, None] == h[None, :]).astype(np.float32), dtype=BF16)


def _silu(x):
    return x * jax.nn.sigmoid(x)


def _gelu(x):
    return 0.5 * x * (1.0 + lax.erf(x * SQRT_HALF))


def _layer_norm(x, g, b):
    xc = x - jnp.mean(x, axis=-1, keepdims=True)
    var = jnp.mean(xc * xc, axis=-1, keepdims=True)
    return xc * lax.rsqrt(var + EPS) * g + b


def _head_rms(x, e_ref, g):
    ms = jnp.dot((x * x).astype(BF16), e_ref[...], preferred_element_type=F32) * (1.0 / 64.0)
    return x * lax.rsqrt(ms + EPS) * g


def _rope(x, cos, sin):
    outs = []
    for half in range(2):
        xh = x[:, 128 * half:128 * half + 128]
        lane = lax.broadcasted_iota(jnp.int32, xh.shape, 1)
        partner = jnp.where((lane & 16) == 0, pltpu.roll(xh, 112, 1), pltpu.roll(xh, 16, 1))
        outs.append(xh * cos + partner * sin)
    return jnp.concatenate(outs, axis=1)


def _lower_bound(lb_ref, layer):
    lb = lb_ref[...]
    e = jnp.exp(lb - jnp.max(lb, axis=0, keepdims=True))
    soft = e / jnp.sum(e, axis=0, keepdims=True)
    acc = jnp.zeros((1, GROUP_W), F32)
    for i in range(1, layer + 1):
        acc = acc + soft[i:i + 1]
    return acc


def _lam(lam_ref):
    lp = lam_ref[0]
    a = jnp.sum(lp[0:1] * lp[1:2], axis=-1, keepdims=True)
    b = jnp.sum(lp[2:3] * lp[3:4], axis=-1, keepdims=True)
    return jnp.exp(a) - jnp.exp(b)


def _lam_init(layer):
    return 0.8 - 0.6 * math.exp(-0.3 * layer)


def _alpha(depth):
    return (2.0 * depth) ** 0.25


def _sg_mix(vn, ws_ref, bias):
    group = lax.broadcasted_iota(jnp.int32, (1, GROUP_W), 1) >> 6
    r = lax.broadcasted_iota(jnp.int32, (ROWS, ROWS), 0)
    c = lax.broadcasted_iota(jnp.int32, (ROWS, ROWS), 1)
    acc = bias
    for g in range(4):
        wm = jnp.where(r >= c, ws_ref[g], 0.0).astype(BF16)
        vm = jnp.where(group == g, vn, 0.0).astype(BF16)
        acc = acc + jnp.dot(wm, vm, preferred_element_type=F32)
    return acc


def _hgrn_gates(dq, df, di, lb):
    f = lb + (1.0 - lb) * jax.nn.sigmoid(df)
    lf = jnp.log(jnp.maximum(f, F_FLOOR))
    hi = lf.astype(BF16)
    lo = (lf - hi.astype(F32)).astype(BF16)
    return _silu(dq), 1.0 - f, di, jnp.concatenate([hi, lo], axis=1)


def _segsums(g_ref, lo, hi, hilo):
    g = g_ref[lo:hi].reshape((hi - lo) * ROWS, ROWS)
    r = jnp.dot(g, hilo, preferred_element_type=F32)
    return r[:, :GROUP_W] + r[:, GROUP_W:]


def _hgrn_intra(qd, kk, vd, seg, e_ref, levels):
    row = lax.broadcasted_iota(jnp.int32, (ROWS, 1), 0)
    head = lax.broadcasted_iota(jnp.int32, (1, GROUP_W), 1) >> 6
    t_idx = lax.broadcasted_iota(jnp.int32, (ROWS, 4 * ROWS), 0)
    s_idx = lax.broadcasted_iota(jnp.int32, (ROWS, 4 * ROWS), 1) & (ROWS - 1)
    att = jnp.zeros((ROWS, 4 * ROWS), F32)
    for i, lvl in enumerate(levels):
        e = jnp.exp(seg[i])
        second = ((row >> lvl) & 1) == 1
        a = jnp.where(second, qd * e, 0.0).astype(BF16)
        b = jnp.where(second, 0.0, kk * e)
        b_heads = jnp.concatenate([jnp.where(head == h, b, 0.0) for h in range(4)], axis=0).astype(BF16)
        s = lax.dot_general(a, b_heads, _NT, preferred_element_type=F32)
        att = att + jnp.where((t_idx >> (lvl + 1)) == (s_idx >> (lvl + 1)), s, 0.0)
    v_stack = jnp.concatenate([jnp.where(head == h, vd, 0.0) for h in range(4)], axis=0).astype(BF16)
    o = jnp.dot(att.astype(BF16), v_stack, preferred_element_type=F32)
    diag = jnp.dot((qd * kk).astype(BF16), e_ref[...], preferred_element_type=F32)
    return o + diag * vd


def _ada_kernel(c_ref, w_ref, b_ref, o_ref):
    c = c_ref[...]
    a = _silu(c).astype(BF16)
    o_ref[0] = jnp.dot(a, w_ref[0].astype(BF16), preferred_element_type=F32) + b_ref[0]


def _ada(c_all, w_ada, b_ada):
    depth = w_ada.shape[0]
    rows = c_all.shape[0]
    return pl.pallas_call(
        _ada_kernel,
        out_shape=jax.ShapeDtypeStruct((depth, rows, 3 * D_MODEL), F32),
        grid=(depth, 3),
        in_specs=[
            pl.BlockSpec((rows, D_MODEL), lambda l, n: (0, 0)),
            pl.BlockSpec((1, D_MODEL, D_MODEL), lambda l, n: (l, 0, n)),
            pl.BlockSpec((1, 1, D_MODEL), lambda l, n: (l, 0, n)),
        ],
        out_specs=pl.BlockSpec((1, rows, D_MODEL), lambda l, n: (l, 0, n)),
        compiler_params=pltpu.CompilerParams(dimension_semantics=("arbitrary", "arbitrary"),
                                             vmem_limit_bytes=VMEM_LIMIT),
        name="adaln",
    )(c_all, w_ada, b_ada.reshape(depth, 1, 3 * D_MODEL))


def _prompt_pre_kernel(layer, tb,
                       x_ref, mod_ref, win_ref, cos_ref, sin_ref, sgg_ref, sgb_ref, ws_ref, sgbias_ref,
                       cw_ref, cb_ref, cng_ref, cnb_ref, wpw_ref, lb_ref, hng_ref, g_ref, e_ref,
                       krow_ref, vrow_ref, qt_ref, kb_ref, vt_ref, ga_ref, obcd_ref, cst_ref, hst_ref,
                       hc_ref, s_ref):
    i = pl.program_id(1)

    @pl.when(i == 0)
    def _():
        hc_ref[0:32, :] = jnp.zeros((32, GROUP_W), F32)
        s_ref[...] = jnp.zeros(s_ref.shape, F32)

    mod = mod_ref[0]
    shift = mod[:, 0:D_MODEL]
    scale = mod[:, D_MODEL:2 * D_MODEL]
    h = (x_ref[0] * (1.0 + scale) + shift).astype(BF16)
    zc = jnp.dot(h, win_ref[0, :, 1792:2560], preferred_element_type=F32)
    zb = jnp.dot(h, win_ref[0, :, 1024:1792], preferred_element_type=F32)
    zd = jnp.dot(h, win_ref[0, :, 2560:3584], preferred_element_type=F32)

    glu = zc[:, 0:GROUP_W] * jax.nn.sigmoid(zc[:, GROUP_W:2 * GROUP_W])
    hc_ref[32:32 + tb, :] = glu
    cst_ref[0] = hc_ref[pl.ds(tb + 2, CONV_W - 1), :]

    lb = _lower_bound(lb_ref, layer)
    o_b_parts, o_c_parts, o_d_parts = [], [], []
    for c in range(tb // ROWS):
        rs = slice(c * ROWS, (c + 1) * ROWS)
        y = jnp.zeros((ROWS, GROUP_W), F32)
        for j in range(CONV_W):
            y = y + hc_ref[pl.ds(c * ROWS + 2 + j, ROWS), :] * cw_ref[0, j:j + 1, :]
        yn = _silu(_layer_norm(y + cb_ref[0], cng_ref[0], cnb_ref[0]))
        o_c_parts.append(jnp.dot(yn.astype(BF16), wpw_ref[0], preferred_element_type=F32) * _silu(zc[rs, 512:768]))
        vn = _layer_norm(_gelu(zb[rs, 256:512]), sgg_ref[0], sgb_ref[0])
        mixed = _sg_mix(vn, ws_ref.at[0], sgbias_ref[0])
        o_b_parts.append(_gelu(zb[rs, 0:256]) * mixed * _silu(zb[rs, 512:768]))
        qd, kk, vd, hilo = _hgrn_gates(zd[rs, 0:256], zd[rs, 256:512], zd[rs, 512:768], lb)
        seg = _segsums(g_ref, G_LVL0, G_SUF + 1, hilo)
        o = _hgrn_intra(qd, kk, vd, [seg[ROWS * n:ROWS * (n + 1)] for n in range(7)], e_ref, range(7))
        aq = qd * jnp.exp(seg[7 * ROWS:8 * ROWS])
        bk = kk * jnp.exp(seg[8 * ROWS:9 * ROWS])
        ones = jnp.ones((ROWS, D_KDIM), BF16)
        inter = []
        for hh in range(D_HEADS):
            lo_, hi_ = 64 * hh, 64 * hh + 64
            st = s_ref[hh]
            inter.append(jnp.dot(aq[:, lo_:hi_].astype(BF16), st.astype(BF16), preferred_element_type=F32))
            dec = (lax.dot_general(hilo[:, lo_:hi_], ones, _TN, preferred_element_type=F32)
                   + lax.dot_general(hilo[:, GROUP_W + lo_:GROUP_W + hi_], ones, _TN, preferred_element_type=F32))
            upd = lax.dot_general(bk[:, lo_:hi_].astype(BF16), vd[:, lo_:hi_].astype(BF16), _TN,
                                  preferred_element_type=F32)
            s_ref[hh] = jnp.exp(dec) * st + upd
        o = o + jnp.concatenate(inter, axis=1)
        o_d_parts.append(_head_rms(o, e_ref, hng_ref[0]) * _silu(zd[rs, 768:1024]))
    za = jnp.dot(h, win_ref[0, :, 0:1024], preferred_element_type=F32)
    cos = cos_ref[...]
    sin = sin_ref[...]
    q = _rope(za[:, 0:256], cos, sin)
    k = _rope(za[:, 256:512], cos, sin)
    v = za[:, 512:768]
    kt = k.T
    vt = v.T
    krow_ref[0] = kt
    vrow_ref[0] = vt
    kb_ref[0] = k.astype(BF16)
    qt_ref[0] = (q * Q_SCALE).T.astype(BF16)
    vt_ref[0] = vt.astype(BF16)
    ga_ref[0] = _silu(za[:, 768:1024])

    hc_ref[0:32, :] = hc_ref[pl.ds(tb, 32), :]
    o_b = jnp.concatenate(o_b_parts, axis=0)
    o_c = jnp.concatenate(o_c_parts, axis=0)
    o_d = jnp.concatenate(o_d_parts, axis=0)
    obcd_ref[0] = jnp.concatenate([o_b, o_c, o_d], axis=1).astype(BF16)
    hst_ref[0] = s_ref[...]


def _prompt_pre(layer, x, mod_p, win_b, cos, sin, sgg, sgb, ws, sgbias, cw, cb, cng, cnb, wpw_b, lbnd, hng,
                gmat, eones):
    b, t, _ = x.shape
    tb = min(PRE_TB, t)
    kernel = functools.partial(_prompt_pre_kernel, layer, tb)
    lsel3 = lambda bb, i: (layer, 0, 0)
    const2 = lambda bb, i: (0, 0)
    const3 = lambda bb, i: (0, 0, 0)
    row_blk = lambda w: pl.BlockSpec((1, tb, w), lambda bb, i: (bb, i, 0))
    col_blk = pl.BlockSpec((1, GROUP_W, tb), lambda bb, i: (bb, 0, i))
    out_shape = (
        jax.ShapeDtypeStruct((b, GROUP_W, t), F32),
        jax.ShapeDtypeStruct((b, GROUP_W, t), F32),
        jax.ShapeDtypeStruct((b, GROUP_W, t), BF16),
        jax.ShapeDtypeStruct((b, t, GROUP_W), BF16),
        jax.ShapeDtypeStruct((b, GROUP_W, t), BF16),
        jax.ShapeDtypeStruct((b, t, GROUP_W), F32),
        jax.ShapeDtypeStruct((b, t, 3 * GROUP_W), BF16),
        jax.ShapeDtypeStruct((b, CONV_W - 1, GROUP_W), F32),
        jax.ShapeDtypeStruct((b, D_HEADS, D_KDIM, D_KDIM), F32),
    )
    return pl.pallas_call(
        kernel,
        out_shape=out_shape,
        grid=(b, t // tb),
        in_specs=[
            row_blk(D_MODEL),
            pl.BlockSpec((1, 1, 3 * D_MODEL), lambda bb, i: (bb, 0, 0)),
            pl.BlockSpec((1, D_MODEL, D_IN), lsel3),
            pl.BlockSpec((tb, 128), lambda bb, i: (i, 0)),
            pl.BlockSpec((tb, 128), lambda bb, i: (i, 0)),
            pl.BlockSpec((1, 1, GROUP_W), lsel3),
            pl.BlockSpec((1, 1, GROUP_W), lsel3),
            pl.BlockSpec((1, 4, ROWS, ROWS), lambda bb, i: (layer, 0, 0, 0)),
            pl.BlockSpec((1, ROWS, GROUP_W), lsel3),
            pl.BlockSpec((1, CONV_W, GROUP_W), lsel3),
            pl.BlockSpec((1, 1, GROUP_W), lsel3),
            pl.BlockSpec((1, 1, GROUP_W), lsel3),
            pl.BlockSpec((1, 1, GROUP_W), lsel3),
            pl.BlockSpec((1, GROUP_W, GROUP_W), lsel3),
            pl.BlockSpec(lbnd.shape, const2),
            pl.BlockSpec((1, 1, GROUP_W), lsel3),
            pl.BlockSpec(gmat.shape, const3),
            pl.BlockSpec(eones.shape, const2),
        ],
        out_specs=(
            col_blk, col_blk, col_blk, row_blk(GROUP_W), col_blk, row_blk(GROUP_W),
            row_blk(3 * GROUP_W),
            pl.BlockSpec((1, CONV_W - 1, GROUP_W), lambda bb, i: (bb, 0, 0)),
            pl.BlockSpec((1, D_HEADS, D_KDIM, D_KDIM), lambda bb, i: (bb, 0, 0, 0)),
        ),
        scratch_shapes=[pltpu.VMEM((32 + tb, GROUP_W), F32), pltpu.VMEM((D_HEADS, D_KDIM, D_KDIM), F32)],
        compiler_params=pltpu.CompilerParams(dimension_semantics=("arbitrary", "arbitrary"),
                                             vmem_limit_bytes=VMEM_LIMIT),
        name=f"prompt_pre_l{layer}",
    )(x, mod_p, win_b, cos, sin, sgg, sgb, ws, sgbias, cw, cb, cng, cnb, wpw_b, lbnd, hng, gmat, eones)


def _prompt_attn_kernel(layer, depth, tq, tk,
                        qt_ref, kb_ref, vt_ref, ga_ref, obcd_ref, x_ref, mod_ref, wout_ref, lam_ref, ang_ref,
                        lng_ref, lnb_ref, y_ref, acc_ref, m_ref, qm_ref, s_ref):
    i = pl.program_id(1)
    qt = qt_ref[0]
    rowg = lax.broadcasted_iota(jnp.int32, (GROUP_W, 1), 0) >> 5
    for j in range(8):
        qm_ref[j] = jnp.where(rowg == j, qt, jnp.zeros_like(qt))
    m_ref[...] = jnp.full(m_ref.shape, NEG_BIG, F32)
    acc_ref[...] = jnp.zeros(acc_ref.shape, F32)
    ones = jnp.ones((16, tk), BF16)
    kpq = tq // tk

    def tile(kt, diag):
        masked = diag is not None
        off = pl.multiple_of(kt * tk, tk)
        kk = kb_ref[0, pl.ds(off, tk), :]
        if masked:
            key = lax.broadcasted_iota(jnp.int32, (tk, tq), 0) + diag * tk
            qry = lax.broadcasted_iota(jnp.int32, (tk, tq), 1)
            visible = key <= qry
        cmax = []
        for j in range(8):
            s = jnp.dot(kk, qm_ref[j], preferred_element_type=F32)
            if masked:
                s = jnp.where(visible, s, NEG_BIG)
            s_ref[j] = s
            cmax.append(jnp.max(s, axis=0, keepdims=True))
        vvs = [jnp.concatenate([vt_ref[0, 64 * hh:64 * hh + 64, pl.ds(off, tk)], ones], axis=0)
               for hh in range(A_HEADS)]
        for j in range(8):
            m_old = m_ref[j]
            m_new = jnp.maximum(m_old, cmax[j])
            alpha = jnp.exp2(m_old - m_new)
            m_ref[j] = m_new
            for c in range(tq // LANE_TILE):
                cols = slice(LANE_TILE * c, LANE_TILE * (c + 1))
                p = jnp.exp2(s_ref[j, :, cols] - m_new[:, cols]).astype(BF16)
                acc_ref[j, :, cols] = (alpha[:, cols] * acc_ref[j, :, cols]
                                       + jnp.dot(vvs[j // 2], p, preferred_element_type=F32))

    def body(kt, carry):
        tile(kt, None)
        return carry

    lax.fori_loop(0, i * kpq, body, 0)
    for dg in range(kpq):
        tile(i * kpq + dg, dg)

    lam = _lam(lam_ref) + _lam_init(layer)
    heads = []
    for hh in range(A_HEADS):
        a0 = acc_ref[2 * hh]
        a1 = acc_ref[2 * hh + 1]
        d = a0[0:A_VDIM] / a0[A_VDIM:A_VDIM + 1] - lam * (a1[0:A_VDIM] / a1[A_VDIM:A_VDIM + 1])
        ms = jnp.mean(d * d, axis=0, keepdims=True)
        heads.append(d * lax.rsqrt(ms + EPS))
    o_a = jnp.concatenate(heads, axis=0).T
    o_a = o_a * (ang_ref[0] * (1.0 - _lam_init(layer))) * ga_ref[0]
    mixed = (jnp.dot(o_a.astype(BF16), wout_ref[0, 0:GROUP_W, :], preferred_element_type=F32)
             + jnp.dot(obcd_ref[0], wout_ref[0, GROUP_W:, :], preferred_element_type=F32))
    gate = mod_ref[0][:, 2 * D_MODEL:]
    y_ref[0] = _layer_norm(_alpha(depth) * x_ref[0] + gate * mixed, lng_ref[0], lnb_ref[0])


def _prompt_attn(layer, depth, qt, kb, vt, ga, obcd, x, mod_p, wout_b, lam_qk, ang, lng, lnb):
    b, t, _ = x.shape
    tq = min(ATTN_TQ, t)
    tk = min(ATTN_TK, tq)
    kernel = functools.partial(_prompt_attn_kernel, layer, depth, tq, tk)
    lsel3 = lambda bb, i: (layer, 0, 0)
    row_blk = lambda w: pl.BlockSpec((1, tq, w), lambda bb, i: (bb, i, 0))
    return pl.pallas_call(
        kernel,
        out_shape=jax.ShapeDtypeStruct((b, t, D_MODEL), F32),
        grid=(b, t // tq),
        in_specs=[
            pl.BlockSpec((1, GROUP_W, tq), lambda bb, i: (bb, 0, i)),
            pl.BlockSpec((1, t, GROUP_W), lambda bb, i: (bb, 0, 0)),
            pl.BlockSpec((1, GROUP_W, t), lambda bb, i: (bb, 0, 0)),
            row_blk(GROUP_W),
            row_blk(3 * GROUP_W),
            row_blk(D_MODEL),
            pl.BlockSpec((1, 1, 3 * D_MODEL), lambda bb, i: (bb, 0, 0)),
            pl.BlockSpec((1, D_MODEL, D_MODEL), lsel3),
            pl.BlockSpec((1, 4, A_HALF), lsel3),
            pl.BlockSpec((1, 1, GROUP_W), lsel3),
            pl.BlockSpec((1, 1, D_MODEL), lsel3),
            pl.BlockSpec((1, 1, D_MODEL), lsel3),
        ],
        out_specs=row_blk(D_MODEL),
        scratch_shapes=[pltpu.VMEM((8, A_VDIM + 16, tq), F32), pltpu.VMEM((8, 1, tq), F32),
                        pltpu.VMEM((8, GROUP_W, tq), BF16), pltpu.VMEM((8, tk, tq), F32)],
        compiler_params=pltpu.CompilerParams(dimension_semantics=("arbitrary", "arbitrary"),
                                             vmem_limit_bytes=VMEM_LIMIT),
        name=f"prompt_attn_l{layer}",
    )(qt, kb, vt, ga, obcd, x, mod_p, wout_b, lam_qk, ang, lng, lnb)


def _sample_pre_kernel(layer, ts,
                       x_ref, mod_ref, win_ref, cos_ref, sin_ref, sgg_ref, sgb_ref, ws_ref, sgbias_ref,
                       cw_ref, cb_ref, cng_ref, cnb_ref, wpw_ref, lb_ref, hng_ref, g_ref, e_ref, cst_ref, hst_ref,
                       krow_ref, vrow_ref, q_ref, ga_ref, obcd_ref, chv_ref, ncst_ref, nhst_ref,
                       hc_ref):
    nseq = ROWS // ts
    mod = mod_ref[...]
    shift = mod[:, :, 0:D_MODEL]
    scale = mod[:, :, D_MODEL:2 * D_MODEL]
    h = (x_ref[...] * (1.0 + scale) + shift).reshape(ROWS, D_MODEL).astype(BF16)
    z = jnp.dot(h, win_ref[0], preferred_element_type=F32)

    cos = cos_ref[...]
    sin = sin_ref[...]
    k = _rope(z[:, 256:512], cos, sin)
    krow_ref[...] = k
    vrow_ref[...] = z[:, 512:768]
    q_ref[...] = _rope(z[:, 0:256], cos, sin) * Q_SCALE
    ga_ref[...] = _silu(z[:, 768:1024])

    vn = _layer_norm(_gelu(z[:, 1280:1536]), sgg_ref[0], sgb_ref[0])
    chv_ref[...] = vn
    o_b = _gelu(z[:, 1024:1280]) * _sg_mix(vn, ws_ref.at[0], sgbias_ref[0]) * _silu(z[:, 1536:1792])

    a = z[:, 1792:2304]
    glu = a[:, :GROUP_W] * jax.nn.sigmoid(a[:, GROUP_W:])
    hc_ref[:, 0:CONV_W - 1, :] = cst_ref[0]
    hc_ref[:, CONV_W - 1:CONV_W - 1 + ts, :] = glu.reshape(nseq, ts, GROUP_W)
    y = jnp.zeros((nseq, ts, GROUP_W), F32)
    for j in range(CONV_W):
        y = y + hc_ref[:, j:j + ts, :] * cw_ref[0, j:j + 1, :]
    ncst_ref[0] = hc_ref[:, ts:ts + CONV_W - 1, :]
    yn = _silu(_layer_norm(y.reshape(ROWS, GROUP_W) + cb_ref[0], cng_ref[0], cnb_ref[0]))
    o_c = jnp.dot(yn.astype(BF16), wpw_ref[0], preferred_element_type=F32) * _silu(z[:, 2304:2560])

    lb = _lower_bound(lb_ref, layer)
    qd, kk, vd, hilo = _hgrn_gates(z[:, 2560:2816], z[:, 2816:3072], z[:, 3072:3328], lb)
    seg = _segsums(g_ref, G_CUM8, G_LVL0 + 3, hilo)
    o = _hgrn_intra(qd, kk, vd, [seg[ROWS * (2 + n):ROWS * (3 + n)] for n in range(3)], e_ref, range(3))
    aq = (qd * jnp.exp(seg[0:ROWS])).reshape(nseq, ts, GROUP_W)
    bk = (kk * jnp.exp(seg[ROWS:2 * ROWS])).reshape(nseq, ts, GROUP_W)
    v3 = vd.reshape(nseq, ts, GROUP_W)
    hilo3 = hilo.astype(F32).reshape(nseq, ts, 2 * GROUP_W)
    ones = jnp.ones((nseq, ts, D_KDIM), BF16)
    inter = []
    for hh in range(D_HEADS):
        lo_, hi_ = 64 * hh, 64 * hh + 64
        st = hst_ref[0, :, hh]
        inter.append(jnp.einsum('bqk,bkv->bqv', aq[:, :, lo_:hi_].astype(BF16), st.astype(BF16),
                                preferred_element_type=F32))
        dec = (jnp.einsum('bsk,bsv->bkv', hilo3[:, :, lo_:hi_].astype(BF16), ones, preferred_element_type=F32)
               + jnp.einsum('bsk,bsv->bkv', hilo3[:, :, GROUP_W + lo_:GROUP_W + hi_].astype(BF16), ones,
                            preferred_element_type=F32))
        upd = jnp.einsum('bsk,bsv->bkv', bk[:, :, lo_:hi_].astype(BF16), v3[:, :, lo_:hi_].astype(BF16),
                         preferred_element_type=F32)
        nhst_ref[0, :, hh] = jnp.exp(dec) * st + upd
    o = o + jnp.concatenate(inter, axis=2).reshape(ROWS, GROUP_W)
    o_d = _head_rms(o, e_ref, hng_ref[0]) * _silu(z[:, 3328:3584])
    obcd_ref[...] = jnp.concatenate([o_b, o_c, o_d], axis=1).astype(BF16)


def _sample_pre(layer, x, mod_s, win_b, cos, sin, sgg, sgb, ws_blk, sgbias, cw, cb, cng, cnb, wpw_b, lbnd, hng,
                gmat, eones, state_conv, state_hgrn):
    bs, ts, _ = x.shape
    nseq = ROWS // ts
    nblk = bs // nseq
    n = bs * ts
    kernel = functools.partial(_sample_pre_kernel, layer, ts)
    lsel3 = lambda i: (layer, 0, 0)
    const2 = lambda i: (0, 0)
    const3 = lambda i: (0, 0, 0)
    row_blk = lambda w: pl.BlockSpec((ROWS, w), lambda i: (i, 0))
    out_shape = (
        jax.ShapeDtypeStruct((n, GROUP_W), F32),
        jax.ShapeDtypeStruct((n, GROUP_W), F32),
        jax.ShapeDtypeStruct((n, GROUP_W), F32),
        jax.ShapeDtypeStruct((n, GROUP_W), F32),
        jax.ShapeDtypeStruct((n, 3 * GROUP_W), BF16),
        jax.ShapeDtypeStruct((n, GROUP_W), F32),
        jax.ShapeDtypeStruct((1, bs, CONV_W - 1, GROUP_W), F32),
        jax.ShapeDtypeStruct((1, bs, D_HEADS, D_KDIM, D_KDIM), F32),
    )
    return pl.pallas_call(
        kernel,
        out_shape=out_shape,
        grid=(nblk,),
        in_specs=[
            pl.BlockSpec((nseq, ts, D_MODEL), lambda i: (i, 0, 0)),
            pl.BlockSpec((nseq, 1, 3 * D_MODEL), lambda i: (i, 0, 0)),
            pl.BlockSpec((1, D_MODEL, D_IN), lsel3),
            pl.BlockSpec((ROWS, 128), const2),
            pl.BlockSpec((ROWS, 128), const2),
            pl.BlockSpec((1, 1, GROUP_W), lsel3),
            pl.BlockSpec((1, 1, GROUP_W), lsel3),
            pl.BlockSpec((1, 4, ROWS, ROWS), lambda i: (layer, 0, 0, 0)),
            pl.BlockSpec((1, ROWS, GROUP_W), lsel3),
            pl.BlockSpec((1, CONV_W, GROUP_W), lsel3),
            pl.BlockSpec((1, 1, GROUP_W), lsel3),
            pl.BlockSpec((1, 1, GROUP_W), lsel3),
            pl.BlockSpec((1, 1, GROUP_W), lsel3),
            pl.BlockSpec((1, GROUP_W, GROUP_W), lsel3),
            pl.BlockSpec(lbnd.shape, const2),
            pl.BlockSpec((1, 1, GROUP_W), lsel3),
            pl.BlockSpec(gmat.shape, const3),
            pl.BlockSpec(eones.shape, const2),
            pl.BlockSpec((1, nseq, CONV_W - 1, GROUP_W), lambda i: (layer, i, 0, 0)),
            pl.BlockSpec((1, nseq, D_HEADS, D_KDIM, D_KDIM), lambda i: (layer, i, 0, 0, 0)),
        ],
        out_specs=(
            row_blk(GROUP_W), row_blk(GROUP_W), row_blk(GROUP_W), row_blk(GROUP_W), row_blk(3 * GROUP_W),
            row_blk(GROUP_W),
            pl.BlockSpec((1, nseq, CONV_W - 1, GROUP_W), lambda i: (0, i, 0, 0)),
            pl.BlockSpec((1, nseq, D_HEADS, D_KDIM, D_KDIM), lambda i: (0, i, 0, 0, 0)),
        ),
        scratch_shapes=[pltpu.VMEM((nseq, 40, GROUP_W), F32)],
        compiler_params=pltpu.CompilerParams(dimension_semantics=("arbitrary",), vmem_limit_bytes=VMEM_LIMIT),
        name=f"sample_pre_l{layer}",
    )(x, mod_s, win_b, cos, sin, sgg, sgb, ws_blk, sgbias, cw, cb, cng, cnb, wpw_b, lbnd, hng, gmat, eones,
      state_conv, state_hgrn)


def _sample_attn_kernel(layer, n_pages, nsq, ts, pt_ref, *refs):
    k_pages = refs[0:nsq * n_pages]
    v_pages = refs[nsq * n_pages:2 * nsq * n_pages]
    q_ref, kn_ref, vn_ref, ga_ref, lam_ref, ang_ref, e_ref, o_ref = refs[2 * nsq * n_pages:]
    grp = lax.broadcasted_iota(jnp.int32, (1, GROUP_W), 1) >> 5
    head = lax.broadcasted_iota(jnp.int32, (1, GROUP_W), 1) >> 6
    t_q = lax.broadcasted_iota(jnp.int32, (8 * ts, ts), 0) & (ts - 1)
    t_k = lax.broadcasted_iota(jnp.int32, (8 * ts, ts), 1)
    vis = t_k <= t_q
    lam = _lam(lam_ref) + _lam_init(layer)
    for sq in range(nsq):
        q = q_ref[sq]
        qexp = jnp.concatenate([jnp.where(grp == j, q, 0.0) for j in range(8)], axis=0).astype(BF16)
        kt_all = jnp.concatenate([r[...] for r in k_pages[sq * n_pages:(sq + 1) * n_pages]], axis=1).astype(BF16)
        vt_all = jnp.concatenate([r[...] for r in v_pages[sq * n_pages:(sq + 1) * n_pages]], axis=1).astype(BF16)
        s_past = jnp.dot(qexp, kt_all, preferred_element_type=F32)
        s_new = lax.dot_general(qexp, kn_ref[sq].astype(BF16), _NT, preferred_element_type=F32)
        s_new = jnp.where(vis, s_new, NEG_BIG)
        m = jnp.maximum(jnp.max(s_past, axis=-1, keepdims=True), jnp.max(s_new, axis=-1, keepdims=True))
        p_past = jnp.exp2(s_past - m)
        p_new = jnp.where(vis, jnp.exp2(s_new - m), 0.0)
        l = jnp.sum(p_past, axis=-1, keepdims=True) + jnp.sum(p_new, axis=-1, keepdims=True)
        o = (lax.dot_general(p_past.astype(BF16), vt_all, _NT, preferred_element_type=F32)
             + jnp.dot(p_new.astype(BF16), vn_ref[sq].astype(BF16), preferred_element_type=F32)) / l
        o_a = jnp.zeros((ts, GROUP_W), F32)
        for hh in range(A_HEADS):
            d = o[2 * hh * ts:(2 * hh + 1) * ts] - lam * o[(2 * hh + 1) * ts:(2 * hh + 2) * ts]
            o_a = o_a + jnp.where(head == hh, d, 0.0)
        o_a = _head_rms(o_a, e_ref, ang_ref[0] * (1.0 - _lam_init(layer))) * ga_ref[sq]
        o_ref[sq] = o_a.astype(BF16)


def _sample_attn(layer, page_table, cache_k, cache_v, q_s, k_new, v_new, ga_s, lam_qk, ang, eones, ts):
    bs, n_pages = page_table.shape
    page = cache_k.shape[3]
    nsq = SEQ_PER_STEP
    assert bs % nsq == 0
    kernel = functools.partial(_sample_attn_kernel, layer, n_pages, nsq, ts)

    def page_spec(sq, j):
        return pl.BlockSpec((None, None, GROUP_W, page), lambda g, pt: (layer, pt[g * nsq + sq, j], 0, 0))

    pages = [page_spec(sq, j) for sq in range(nsq) for j in range(n_pages)]
    seq_blk = pl.BlockSpec((nsq, ts, GROUP_W), lambda g, pt: (g, 0, 0))
    lsel3 = lambda g, pt: (layer, 0, 0)
    grid_spec = pltpu.PrefetchScalarGridSpec(
        num_scalar_prefetch=1,
        grid=(bs // nsq,),
        in_specs=(pages + pages
                  + [seq_blk, seq_blk, seq_blk, seq_blk,
                     pl.BlockSpec((1, 4, A_HALF), lsel3),
                     pl.BlockSpec((1, 1, GROUP_W), lsel3),
                     pl.BlockSpec(eones.shape, lambda b, pt: (0, 0))]),
        out_specs=seq_blk,
    )
    shp3 = (bs, ts, GROUP_W)
    return pl.pallas_call(
        kernel,
        out_shape=jax.ShapeDtypeStruct(shp3, BF16),
        grid_spec=grid_spec,
        compiler_params=pltpu.CompilerParams(dimension_semantics=("arbitrary",), vmem_limit_bytes=VMEM_LIMIT),
        name=f"sample_attn_l{layer}",
    )(page_table, *([cache_k] * (nsq * n_pages)), *([cache_v] * (nsq * n_pages)),
      q_s.reshape(shp3), k_new.reshape(shp3), v_new.reshape(shp3), ga_s.reshape(shp3), lam_qk, ang, eones)


def _sample_out_kernel(depth, ts, oa_ref, obcd_ref, x_ref, mod_ref, wout_ref, lng_ref, lnb_ref, y_ref):
    nseq = ROWS // ts
    mixed = (jnp.dot(oa_ref[...], wout_ref[0, 0:GROUP_W, :], preferred_element_type=F32)
             + jnp.dot(obcd_ref[...], wout_ref[0, GROUP_W:, :], preferred_element_type=F32))
    gate = mod_ref[...][:, :, 2 * D_MODEL:]
    y = _alpha(depth) * x_ref[...] + gate * mixed.reshape(nseq, ts, D_MODEL)
    y_ref[...] = _layer_norm(y, lng_ref[0], lnb_ref[0])


def _sample_out(layer, depth, oa, obcd, x, mod_s, wout_b, lng, lnb):
    bs, ts, _ = x.shape
    nseq = ROWS // ts
    lsel3 = lambda i: (layer, 0, 0)
    return pl.pallas_call(
        functools.partial(_sample_out_kernel, depth, ts),
        out_shape=jax.ShapeDtypeStruct(x.shape, F32),
        grid=(bs // nseq,),
        in_specs=[
            pl.BlockSpec((ROWS, GROUP_W), lambda i: (i, 0)),
            pl.BlockSpec((ROWS, 3 * GROUP_W), lambda i: (i, 0)),
            pl.BlockSpec((nseq, ts, D_MODEL), lambda i: (i, 0, 0)),
            pl.BlockSpec((nseq, 1, 3 * D_MODEL), lambda i: (i, 0, 0)),
            pl.BlockSpec((1, D_MODEL, D_MODEL), lsel3),
            pl.BlockSpec((1, 1, D_MODEL), lsel3),
            pl.BlockSpec((1, 1, D_MODEL), lsel3),
        ],
        out_specs=pl.BlockSpec((nseq, ts, D_MODEL), lambda i: (i, 0, 0)),
        compiler_params=pltpu.CompilerParams(dimension_semantics=("arbitrary",), vmem_limit_bytes=VMEM_LIMIT),
        name=f"sample_out_l{layer}",
    )(oa, obcd, x, mod_s, wout_b, lng, lnb)


def _rope_tables(pos):
    half = A_HALF // 2
    inv = ROPE_THETA ** (-jnp.arange(half, dtype=F32) * 2.0 / A_HALF)
    ang = pos.astype(F32)[:, None] * inv[None, :]
    cos = jnp.cos(ang)
    sin = jnp.sin(ang)
    return jnp.tile(jnp.concatenate([cos, cos], -1), (1, 4)), jnp.tile(jnp.concatenate([-sin, sin], -1), (1, 4))


def kernel(x_prompt, x_sample, cache_k, cache_v, state_conv, state_hgrn, page_table, c_prompt, c_sample, w_ada, b_ada, w_in, lam_qk, attn_norm_g, sg_norm_g, sg_norm_b, w_s, b_s, conv_w, conv_b, conv_norm_g, conv_norm_b, w_pw, lower_bounds, hgrn_norm_g, w_out, ln_g, ln_b):
    depth = w_in.shape[0]
    bp, t, _ = x_prompt.shape
    bs, ts, _ = x_sample.shape
    n_pool, page = cache_k.shape[1], cache_k.shape[2]
    past_len = page_table.shape[1] * page
    assert ts == 8 and ROWS % ts == 0 and bs % (ROWS // ts) == 0 and t % ROWS == 0
    nseq = ROWS // ts

    gmat = _segment_matrices()
    eones = _head_ones()
    win_b = w_in.astype(BF16)
    wout_b = w_out.astype(BF16)
    wpw_b = w_pw.astype(BF16)
    row3 = lambda a: a.reshape(depth, 1, a.shape[-1])
    sgg, sgb, cb, cng, cnb = row3(sg_norm_g), row3(sg_norm_b), row3(conv_b), row3(conv_norm_g), row3(conv_norm_b)
    lng, lnb = row3(ln_g), row3(ln_b)
    ang = row3(jnp.tile(attn_norm_g, (1, A_HEADS)))
    hng = row3(jnp.tile(hgrn_norm_g, (1, D_HEADS)))
    sgbias_p = jnp.repeat(jnp.swapaxes(b_s, 1, 2), GROUP_W // 4, axis=2)
    sgbias_s = jnp.tile(sgbias_p[:, :ts], (1, nseq, 1))
    eye = jnp.eye(nseq, dtype=F32)
    ws_blk = jnp.einsum('ab,lgts->lgatbs', eye, w_s[:, :, :ts, :ts]).reshape(depth, 4, ROWS, ROWS)
    cos_p, sin_p = _rope_tables(jnp.arange(t))
    cos_s, sin_s = _rope_tables(past_len + jnp.arange(ts))
    cos_s, sin_s = jnp.tile(cos_s, (nseq, 1)), jnp.tile(sin_s, (nseq, 1))
    ck = jnp.transpose(cache_k, (0, 1, 3, 4, 2)).reshape(depth, n_pool, GROUP_W, page)
    cv = jnp.transpose(cache_v, (0, 1, 3, 4, 2)).reshape(depth, n_pool, GROUP_W, page)

    rows = bp + bs
    pad = (-rows) % 8
    c_all = jnp.concatenate([c_prompt, c_sample, jnp.zeros((pad, D_MODEL), F32)], axis=0)
    mod = _ada(c_all, w_ada, b_ada)

    xp, xs = x_prompt, x_sample
    outs = [[] for _ in range(9)]
    for l in range(depth):
        mod_p = mod[l, :bp].reshape(bp, 1, 3 * D_MODEL)
        mod_s = mod[l, bp:bp + bs].reshape(bs, 1, 3 * D_MODEL)
        krow, vrow, qt, kb, vt, ga, obcd, cst_p, hst_p = _prompt_pre(
            l, xp, mod_p, win_b, cos_p, sin_p, sgg, sgb, w_s, sgbias_p, conv_w, cb, cng, cnb, wpw_b, lower_bounds,
            hng, gmat, eones)
        xp = _prompt_attn(l, depth, qt, kb, vt, ga, obcd, xp, mod_p, wout_b, lam_qk, ang, lng, lnb)

        krow_s, vrow_s, q_s, ga_s, obcd_s, chv_s, cst_s, hst_s = _sample_pre(
            l, xs, mod_s, win_b, cos_s, sin_s, sgg, sgb, ws_blk, sgbias_s, conv_w, cb, cng, cnb, wpw_b,
            lower_bounds, hng, gmat, eones, state_conv, state_hgrn)
        oa_s = _sample_attn(l, page_table, ck, cv, q_s, krow_s, vrow_s, ga_s, lam_qk, ang, eones, ts)
        xs = _sample_out(l, depth, oa_s.reshape(bs * ts, GROUP_W), obcd_s, xs, mod_s, wout_b, lng, lnb)

        outs[0].append(jnp.transpose(krow.reshape(bp, A_HEADS, 2 * A_HALF, t), (0, 3, 1, 2)))
        outs[1].append(jnp.transpose(vrow.reshape(bp, A_HEADS, A_VDIM, t), (0, 3, 1, 2)))
        outs[2].append(krow_s.reshape(bs, ts, A_HEADS, 2 * A_HALF))
        outs[3].append(vrow_s.reshape(bs, ts, A_HEADS, A_VDIM))
        outs[4].append(chv_s.reshape(bs, ts, GROUP_W))
        outs[5].append(cst_p)
        outs[6].append(cst_s[0])
        outs[7].append(hst_p)
        outs[8].append(hst_s[0])
    return (xp, xs) + tuple(jnp.stack(o) for o in outs)
```

```python
import functools
import math

import jax
import jax.numpy as jnp
import numpy as np
from jax import lax
from jax.experimental import pallas as pl
from jax.experimental.pallas import tpu as pltpu

F32 = jnp.float32
BF16 = jnp.bfloat16

D_MODEL = 1024
GROUP_W = 256
A_HEADS = 4
A_HALF = 32
A_VDIM = 64
ROPE_THETA = 10000.0
SG_CHUNK = 128
CONV_W = 31
D_HEADS = 4
D_KDIM = 64
F_FLOOR = 1e-30
EPS = 1e-5
NEG_BIG = -1e30
D_IN = 14 * GROUP_W
SQRT_HALF = 0.7071067811865476
Q_SCALE = (A_HALF ** -0.5) * math.log2(math.e)

ROWS = 128
PRE_TB = 512
ATTN_TQ = 512
ATTN_TK = 512
LANE_TILE = 256
SCORE_LEAD = 2
VMEM_LIMIT = 56 * 1024 * 1024

G_CUM8, G_SUF8, G_LVL0, G_CUM, G_SUF = 0, 1, 2, 9, 10

_NT = (((1,), (1,)), ((), ()))
_TN = (((0,), (0,)), ((), ()))


def _segment_matrices():
    t = np.arange(ROWS)[:, None]
    s = np.arange(ROWS)[None, :]
    same8 = (t >> 3) == (s >> 3)
    mats = [same8 & (s <= t), same8 & (s > t)]
    for lvl in range(7):
        mid = ((t >> (lvl + 1)) << (lvl + 1)) + (1 << lvl)
        second = ((t >> lvl) & 1) == 1
        mats.append(np.where(second, (s >= mid) & (s <= t), (s > t) & (s < mid)))
    mats.append(s <= t)
    mats.append(s > t)
    return jnp.asarray(np.stack(mats).astype(np.float32), dtype=BF16)


def _head_ones():
    h = np.arange(GROUP_W) // 64
    return jnp.asarray((h[:, None] == h[None, :]).astype(np.float32), dtype=BF16)


def _silu(x):
    return x * jax.nn.sigmoid(x)


def _gelu(x):
    return 0.5 * x * (1.0 + lax.erf(x * SQRT_HALF))


def _layer_norm(x, g, b):
    xc = x - jnp.mean(x, axis=-1, keepdims=True)
    var = jnp.mean(xc * xc, axis=-1, keepdims=True)
    return xc * lax.rsqrt(var + EPS) * g + b


def _head_rms(x, e_ref, g):
    ms = jnp.dot((x * x).astype(BF16), e_ref[...], preferred_element_type=F32) * (1.0 / 64.0)
    return x * lax.rsqrt(ms + EPS) * g


def _rope(x, cos, sin):
    outs = []
    for half in range(2):
        xh = x[:, 128 * half:128 * half + 128]
        lane = lax.broadcasted_iota(jnp.int32, xh.shape, 1)
        partner = jnp.where((lane & 16) == 0, pltpu.roll(xh, 112, 1), pltpu.roll(xh, 16, 1))
        outs.append(xh * cos + partner * sin)
    return jnp.concatenate(outs, axis=1)


def _lower_bound(lb_ref, layer):
    lb = lb_ref[...]
    e = jnp.exp(lb - jnp.max(lb, axis=0, keepdims=True))
    soft = e / jnp.sum(e, axis=0, keepdims=True)
    acc = jnp.zeros((1, GROUP_W), F32)
    for i in range(1, layer + 1):
        acc = acc + soft[i:i + 1]
    return acc


def _lam(lam_ref):
    lp = lam_ref[0]
    a = jnp.sum(lp[0:1] * lp[1:2], axis=-1, keepdims=True)
    b = jnp.sum(lp[2:3] * lp[3:4], axis=-1, keepdims=True)
    return jnp.exp(a) - jnp.exp(b)


def _lam_init(layer):
    return 0.8 - 0.6 * math.exp(-0.3 * layer)


def _alpha(depth):
    return (2.0 * depth) ** 0.25


def _sg_mix(vn, ws_ref, bias):
    group = lax.broadcasted_iota(jnp.int32, (1, GROUP_W), 1) >> 6
    r = lax.broadcasted_iota(jnp.int32, (ROWS, ROWS), 0)
    c = lax.broadcasted_iota(jnp.int32, (ROWS, ROWS), 1)
    acc = bias
    for g in range(4):
        wm = jnp.where(r >= c, ws_ref[g], 0.0).astype(BF16)
        vm = jnp.where(group == g, vn, 0.0).astype(BF16)
        acc = acc + jnp.dot(wm, vm, preferred_element_type=F32)
    return acc


def _hgrn_gates(dq, df, di, lb):
    f = lb + (1.0 - lb) * jax.nn.sigmoid(df)
    lf = jnp.log(jnp.maximum(f, F_FLOOR))
    hi = lf.astype(BF16)
    lo = (lf - hi.astype(F32)).astype(BF16)
    return _silu(dq), 1.0 - f, di, jnp.concatenate([hi, lo], axis=1)


def _segsums(g_ref, lo, hi, hilo):
    g = g_ref[lo:hi].reshape((hi - lo) * ROWS, ROWS)
    r = jnp.dot(g, hilo, preferred_element_type=F32)
    return r[:, :GROUP_W] + r[:, GROUP_W:]


def _hgrn_intra(qd, kk, vd, seg, e_ref, levels):
    row = lax.broadcasted_iota(jnp.int32, (ROWS, 1), 0)
    head = lax.broadcasted_iota(jnp.int32, (1, GROUP_W), 1) >> 6
    t_idx = lax.broadcasted_iota(jnp.int32, (ROWS, 4 * ROWS), 0)
    s_idx = lax.broadcasted_iota(jnp.int32, (ROWS, 4 * ROWS), 1) & (ROWS - 1)
    att = jnp.zeros((ROWS, 4 * ROWS), F32)
    for i, lvl in enumerate(levels):
        e = jnp.exp(seg[i])
        second = ((row >> lvl) & 1) == 1
        a = jnp.where(second, qd * e, 0.0).astype(BF16)
        b = jnp.where(second, 0.0, kk * e)
        b_heads = jnp.concatenate([jnp.where(head == h, b, 0.0) for h in range(4)], axis=0).astype(BF16)
        s = lax.dot_general(a, b_heads, _NT, preferred_element_type=F32)
        att = att + jnp.where((t_idx >> (lvl + 1)) == (s_idx >> (lvl + 1)), s, 0.0)
    v_stack = jnp.concatenate([jnp.where(head == h, vd, 0.0) for h in range(4)], axis=0).astype(BF16)
    o = jnp.dot(att.astype(BF16), v_stack, preferred_element_type=F32)
    diag = jnp.dot((qd * kk).astype(BF16), e_ref[...], preferred_element_type=F32)
    return o + diag * vd


def _ada_kernel(c_ref, w_ref, b_ref, o_ref):
    c = c_ref[...]
    a = _silu(c).astype(BF16)
    o_ref[0] = jnp.dot(a, w_ref[0].astype(BF16), preferred_element_type=F32) + b_ref[0]


def _ada(c_all, w_ada, b_ada):
    depth = w_ada.shape[0]
    rows = c_all.shape[0]
    return pl.pallas_call(
        _ada_kernel,
        out_shape=jax.ShapeDtypeStruct((depth, rows, 3 * D_MODEL), F32),
        grid=(depth, 3),
        in_specs=[
            pl.BlockSpec((rows, D_MODEL), lambda l, n: (0, 0)),
            pl.BlockSpec((1, D_MODEL, D_MODEL), lambda l, n: (l, 0, n)),
            pl.BlockSpec((1, 1, D_MODEL), lambda l, n: (l, 0, n)),
        ],
        out_specs=pl.BlockSpec((1, rows, D_MODEL), lambda l, n: (l, 0, n)),
        compiler_params=pltpu.CompilerParams(dimension_semantics=("arbitrary", "arbitrary"),
                                             vmem_limit_bytes=VMEM_LIMIT),
        name="adaln",
    )(c_all, w_ada, b_ada.reshape(depth, 1, 3 * D_MODEL))


def _prompt_pre_kernel(layer, tb,
                       x_ref, mod_ref, win_ref, cos_ref, sin_ref, sgg_ref, sgb_ref, ws_ref, sgbias_ref,
                       cw_ref, cb_ref, cng_ref, cnb_ref, wpw_ref, lb_ref, hng_ref, g_ref, e_ref,
                       krow_ref, vrow_ref, qt_ref, kb_ref, vt_ref, ga_ref, obcd_ref, cst_ref, hst_ref,
                       hc_ref, s_ref):
    i = pl.program_id(1)

    @pl.when(i == 0)
    def _():
        hc_ref[0:32, :] = jnp.zeros((32, GROUP_W), F32)
        s_ref[...] = jnp.zeros(s_ref.shape, F32)

    mod = mod_ref[0]
    shift = mod[:, 0:D_MODEL]
    scale = mod[:, D_MODEL:2 * D_MODEL]
    h = (x_ref[0] * (1.0 + scale) + shift).astype(BF16)
    zc = jnp.dot(h, win_ref[0, :, 1792:2560], preferred_element_type=F32)
    zb = jnp.dot(h, win_ref[0, :, 1024:1792], preferred_element_type=F32)
    zd = jnp.dot(h, win_ref[0, :, 2560:3584], preferred_element_type=F32)

    glu = zc[:, 0:GROUP_W] * jax.nn.sigmoid(zc[:, GROUP_W:2 * GROUP_W])
    hc_ref[32:32 + tb, :] = glu
    cst_ref[0] = hc_ref[pl.ds(tb + 2, CONV_W - 1), :]

    lb = _lower_bound(lb_ref, layer)
    o_b_parts, o_c_parts, o_d_parts = [], [], []
    for c in range(tb // ROWS):
        rs = slice(c * ROWS, (c + 1) * ROWS)
        y = jnp.zeros((ROWS, GROUP_W), F32)
        for j in range(CONV_W):
            y = y + hc_ref[pl.ds(c * ROWS + 2 + j, ROWS), :] * cw_ref[0, j:j + 1, :]
        yn = _silu(_layer_norm(y + cb_ref[0], cng_ref[0], cnb_ref[0]))
        o_c_parts.append(jnp.dot(yn.astype(BF16), wpw_ref[0], preferred_element_type=F32) * _silu(zc[rs, 512:768]))
        vn = _layer_norm(_gelu(zb[rs, 256:512]), sgg_ref[0], sgb_ref[0])
        mixed = _sg_mix(vn, ws_ref.at[0], sgbias_ref[0])
        o_b_parts.append(_gelu(zb[rs, 0:256]) * mixed * _silu(zb[rs, 512:768]))
        qd, kk, vd, hilo = _hgrn_gates(zd[rs, 0:256], zd[rs, 256:512], zd[rs, 512:768], lb)
        seg = _segsums(g_ref, G_LVL0, G_SUF + 1, hilo)
        o = _hgrn_intra(qd, kk, vd, [seg[ROWS * n:ROWS * (n + 1)] for n in range(7)], e_ref, range(7))
        aq = qd * jnp.exp(seg[7 * ROWS:8 * ROWS])
        bk = kk * jnp.exp(seg[8 * ROWS:9 * ROWS])
        ones = jnp.ones((ROWS, D_KDIM), BF16)
        inter = []
        for hh in range(D_HEADS):
            lo_, hi_ = 64 * hh, 64 * hh + 64
            st = s_ref[hh]
            inter.append(jnp.dot(aq[:, lo_:hi_].astype(BF16), st.astype(BF16), preferred_element_type=F32))
            dec = (lax.dot_general(hilo[:, lo_:hi_], ones, _TN, preferred_element_type=F32)
                   + lax.dot_general(hilo[:, GROUP_W + lo_:GROUP_W + hi_], ones, _TN, preferred_element_type=F32))
            upd = lax.dot_general(bk[:, lo_:hi_].astype(BF16), vd[:, lo_:hi_].astype(BF16), _TN,
                                  preferred_element_type=F32)
            s_ref[hh] = jnp.exp(dec) * st + upd
        o = o + jnp.concatenate(inter, axis=1)
        o_d_parts.append(_head_rms(o, e_ref, hng_ref[0]) * _silu(zd[rs, 768:1024]))
    za = jnp.dot(h, win_ref[0, :, 0:1024], preferred_element_type=F32)
    cos = cos_ref[...]
    sin = sin_ref[...]
    q = _rope(za[:, 0:256], cos, sin)
    k = _rope(za[:, 256:512], cos, sin)
    v = za[:, 512:768]
    kt = k.T
    vt = v.T
    krow_ref[0] = kt
    vrow_ref[0] = vt
    kb_ref[0] = k.astype(BF16)
    qt_ref[0] = (q * Q_SCALE).T.astype(BF16)
    vt_ref[0] = vt.astype(BF16)
    ga_ref[0] = _silu(za[:, 768:1024])

    hc_ref[0:32, :] = hc_ref[pl.ds(tb, 32), :]
    o_b = jnp.concatenate(o_b_parts, axis=0)
    o_c = jnp.concatenate(o_c_parts, axis=0)
    o_d = jnp.concatenate(o_d_parts, axis=0)
    obcd_ref[0] = jnp.concatenate([o_b, o_c, o_d], axis=1).astype(BF16)
    hst_ref[0] = s_ref[...]


def _prompt_pre(layer, x, mod_p, win_b, cos, sin, sgg, sgb, ws, sgbias, cw, cb, cng, cnb, wpw_b, lbnd, hng,
                gmat, eones):
    b, t, _ = x.shape
    tb = min(PRE_TB, t)
    kernel = functools.partial(_prompt_pre_kernel, layer, tb)
    lsel3 = lambda bb, i: (layer, 0, 0)
    const2 = lambda bb, i: (0, 0)
    const3 = lambda bb, i: (0, 0, 0)
    row_blk = lambda w: pl.BlockSpec((1, tb, w), lambda bb, i: (bb, i, 0))
    col_blk = pl.BlockSpec((1, GROUP_W, tb), lambda bb, i: (bb, 0, i))
    out_shape = (
        jax.ShapeDtypeStruct((b, GROUP_W, t), F32),
        jax.ShapeDtypeStruct((b, GROUP_W, t), F32),
        jax.ShapeDtypeStruct((b, GROUP_W, t), BF16),
        jax.ShapeDtypeStruct((b, t, GROUP_W), BF16),
        jax.ShapeDtypeStruct((b, GROUP_W, t), BF16),
        jax.ShapeDtypeStruct((b, t, GROUP_W), F32),
        jax.ShapeDtypeStruct((b, t, 3 * GROUP_W), BF16),
        jax.ShapeDtypeStruct((b, CONV_W - 1, GROUP_W), F32),
        jax.ShapeDtypeStruct((b, D_HEADS, D_KDIM, D_KDIM), F32),
    )
    return pl.pallas_call(
        kernel,
        out_shape=out_shape,
        grid=(b, t // tb),
        in_specs=[
            row_blk(D_MODEL),
            pl.BlockSpec((1, 1, 3 * D_MODEL), lambda bb, i: (bb, 0, 0)),
            pl.BlockSpec((1, D_MODEL, D_IN), lsel3),
            pl.BlockSpec((tb, 128), lambda bb, i: (i, 0)),
            pl.BlockSpec((tb, 128), lambda bb, i: (i, 0)),
            pl.BlockSpec((1, 1, GROUP_W), lsel3),
            pl.BlockSpec((1, 1, GROUP_W), lsel3),
            pl.BlockSpec((1, 4, ROWS, ROWS), lambda bb, i: (layer, 0, 0, 0)),
            pl.BlockSpec((1, ROWS, GROUP_W), lsel3),
            pl.BlockSpec((1, CONV_W, GROUP_W), lsel3),
            pl.BlockSpec((1, 1, GROUP_W), lsel3),
            pl.BlockSpec((1, 1, GROUP_W), lsel3),
            pl.BlockSpec((1, 1, GROUP_W), lsel3),
            pl.BlockSpec((1, GROUP_W, GROUP_W), lsel3),
            pl.BlockSpec(lbnd.shape, const2),
            pl.BlockSpec((1, 1, GROUP_W), lsel3),
            pl.BlockSpec(gmat.shape, const3),
            pl.BlockSpec(eones.shape, const2),
        ],
        out_specs=(
            col_blk, col_blk, col_blk, row_blk(GROUP_W), col_blk, row_blk(GROUP_W),
            row_blk(3 * GROUP_W),
            pl.BlockSpec((1, CONV_W - 1, GROUP_W), lambda bb, i: (bb, 0, 0)),
            pl.BlockSpec((1, D_HEADS, D_KDIM, D_KDIM), lambda bb, i: (bb, 0, 0, 0)),
        ),
        scratch_shapes=[pltpu.VMEM((32 + tb, GROUP_W), F32), pltpu.VMEM((D_HEADS, D_KDIM, D_KDIM), F32)],
        compiler_params=pltpu.CompilerParams(dimension_semantics=("arbitrary", "arbitrary"),
                                             vmem_limit_bytes=VMEM_LIMIT),
        name=f"prompt_pre_l{layer}",
    )(x, mod_p, win_b, cos, sin, sgg, sgb, ws, sgbias, cw, cb, cng, cnb, wpw_b, lbnd, hng, gmat, eones)


def _prompt_attn_kernel(layer, depth, tq, tk,
                        qt_ref, kb_ref, vt_ref, ga_ref, obcd_ref, x_ref, mod_ref, wout_ref, lam_ref, ang_ref,
                        lng_ref, lnb_ref, y_ref, acc_ref, m_ref, qm_ref, s_ref):
    i = pl.program_id(1)
    qt = qt_ref[0]
    rowg = lax.broadcasted_iota(jnp.int32, (GROUP_W, 1), 0) >> 5
    for j in range(8):
        qm_ref[j] = jnp.where(rowg == j, qt, jnp.zeros_like(qt))
    m_ref[...] = jnp.full(m_ref.shape, NEG_BIG, F32)
    acc_ref[...] = jnp.zeros(acc_ref.shape, F32)
    ones = jnp.ones((16, tk), BF16)
    kpq = tq // tk

    def tile(kt, diag):
        masked = diag is not None
        off = pl.multiple_of(kt * tk, tk)
        kk = kb_ref[0, pl.ds(off, tk), :]
        if masked:
            key = lax.broadcasted_iota(jnp.int32, (tk, tq), 0) + diag * tk
            qry = lax.broadcasted_iota(jnp.int32, (tk, tq), 1)
            visible = key <= qry
        vvs = [jnp.concatenate([vt_ref[0, 64 * hh:64 * hh + 64, pl.ds(off, tk)], ones], axis=0)
               for hh in range(A_HEADS)]

        def scores(j):
            s = jnp.dot(kk, qm_ref[j], preferred_element_type=F32)
            if masked:
                s = jnp.where(visible, s, NEG_BIG)
            s_ref[j] = s
            return jnp.max(s, axis=0, keepdims=True)

        def update(j, cmax):
            m_old = m_ref[j]
            m_new = jnp.maximum(m_old, cmax)
            alpha = jnp.exp2(m_old - m_new)
            m_ref[j] = m_new
            for c in range(tq // LANE_TILE):
                cols = slice(LANE_TILE * c, LANE_TILE * (c + 1))
                p = jnp.exp2(s_ref[j, :, cols] - m_new[:, cols]).astype(BF16)
                acc_ref[j, :, cols] = (alpha[:, cols] * acc_ref[j, :, cols]
                                       + jnp.dot(vvs[j // 2], p, preferred_element_type=F32))

        cmax = {}
        for j in range(8 + SCORE_LEAD):
            if j < 8:
                cmax[j] = scores(j)
            if j >= SCORE_LEAD:
                update(j - SCORE_LEAD, cmax[j - SCORE_LEAD])

    def body(kt, carry):
        tile(kt, None)
        return carry

    lax.fori_loop(0, i * kpq, body, 0)
    for dg in range(kpq):
        tile(i * kpq + dg, dg)

    lam = _lam(lam_ref) + _lam_init(layer)
    heads = []
    for hh in range(A_HEADS):
        a0 = acc_ref[2 * hh]
        a1 = acc_ref[2 * hh + 1]
        d = a0[0:A_VDIM] / a0[A_VDIM:A_VDIM + 1] - lam * (a1[0:A_VDIM] / a1[A_VDIM:A_VDIM + 1])
        ms = jnp.mean(d * d, axis=0, keepdims=True)
        heads.append(d * lax.rsqrt(ms + EPS))
    o_a = jnp.concatenate(heads, axis=0).T
    o_a = o_a * (ang_ref[0] * (1.0 - _lam_init(layer))) * ga_ref[0]
    mixed = (jnp.dot(o_a.astype(BF16), wout_ref[0, 0:GROUP_W, :], preferred_element_type=F32)
             + jnp.dot(obcd_ref[0], wout_ref[0, GROUP_W:, :], preferred_element_type=F32))
    gate = mod_ref[0][:, 2 * D_MODEL:]
    y_ref[0] = _layer_norm(_alpha(depth) * x_ref[0] + gate * mixed, lng_ref[0], lnb_ref[0])


def _prompt_attn(layer, depth, qt, kb, vt, ga, obcd, x, mod_p, wout_b, lam_qk, ang, lng, lnb):
    b, t, _ = x.shape
    tq = min(ATTN_TQ, t)
    tk = min(ATTN_TK, tq)
    kernel = functools.partial(_prompt_attn_kernel, layer, depth, tq, tk)
    lsel3 = lambda bb, i: (layer, 0, 0)
    row_blk = lambda w: pl.BlockSpec((1, tq, w), lambda bb, i: (bb, i, 0))
    return pl.pallas_call(
        kernel,
        out_shape=jax.ShapeDtypeStruct((b, t, D_MODEL), F32),
        grid=(b, t // tq),
        in_specs=[
            pl.BlockSpec((1, GROUP_W, tq), lambda bb, i: (bb, 0, i)),
            pl.BlockSpec((1, t, GROUP_W), lambda bb, i: (bb, 0, 0)),
            pl.BlockSpec((1, GROUP_W, t), lambda bb, i: (bb, 0, 0)),
            row_blk(GROUP_W),
            row_blk(3 * GROUP_W),
            row_blk(D_MODEL),
            pl.BlockSpec((1, 1, 3 * D_MODEL), lambda bb, i: (bb, 0, 0)),
            pl.BlockSpec((1, D_MODEL, D_MODEL), lsel3),
            pl.BlockSpec((1, 4, A_HALF), lsel3),
            pl.BlockSpec((1, 1, GROUP_W), lsel3),
            pl.BlockSpec((1, 1, D_MODEL), lsel3),
            pl.BlockSpec((1, 1, D_MODEL), lsel3),
        ],
        out_specs=row_blk(D_MODEL),
        scratch_shapes=[pltpu.VMEM((8, A_VDIM + 16, tq), F32), pltpu.VMEM((8, 1, tq), F32),
                        pltpu.VMEM((8, GROUP_W, tq), BF16), pltpu.VMEM((8, tk, tq), F32)],
        compiler_params=pltpu.CompilerParams(dimension_semantics=("arbitrary", "arbitrary"),
                                             vmem_limit_bytes=VMEM_LIMIT),
        name=f"prompt_attn_l{layer}",
    )(qt, kb, vt, ga, obcd, x, mod_p, wout_b, lam_qk, ang, lng, lnb)


def _sample_pre_kernel(layer, ts,
                       x_ref, mod_ref, win_ref, cos_ref, sin_ref, sgg_ref, sgb_ref, ws_ref, sgbias_ref,
                       cw_ref, cb_ref, cng_ref, cnb_ref, wpw_ref, lb_ref, hng_ref, g_ref, e_ref, cst_ref, hst_ref,
                       krow_ref, vrow_ref, q_ref, ga_ref, obcd_ref, chv_ref, ncst_ref, nhst_ref,
                       hc_ref):
    nseq = ROWS // ts
    mod = mod_ref[...]
    shift = mod[:, :, 0:D_MODEL]
    scale = mod[:, :, D_MODEL:2 * D_MODEL]
    h = (x_ref[...] * (1.0 + scale) + shift).reshape(ROWS, D_MODEL).astype(BF16)
    z = jnp.dot(h, win_ref[0], preferred_element_type=F32)

    cos = cos_ref[...]
    sin = sin_ref[...]
    k = _rope(z[:, 256:512], cos, sin)
    krow_ref[...] = k
    vrow_ref[...] = z[:, 512:768]
    q_ref[...] = _rope(z[:, 0:256], cos, sin) * Q_SCALE
    ga_ref[...] = _silu(z[:, 768:1024])

    vn = _layer_norm(_gelu(z[:, 1280:1536]), sgg_ref[0], sgb_ref[0])
    chv_ref[...] = vn
    o_b = _gelu(z[:, 1024:1280]) * _sg_mix(vn, ws_ref.at[0], sgbias_ref[0]) * _silu(z[:, 1536:1792])

    a = z[:, 1792:2304]
    glu = a[:, :GROUP_W] * jax.nn.sigmoid(a[:, GROUP_W:])
    hc_ref[:, 0:CONV_W - 1, :] = cst_ref[0]
    hc_ref[:, CONV_W - 1:CONV_W - 1 + ts, :] = glu.reshape(nseq, ts, GROUP_W)
    y = jnp.zeros((nseq, ts, GROUP_W), F32)
    for j in range(CONV_W):
        y = y + hc_ref[:, j:j + ts, :] * cw_ref[0, j:j + 1, :]
    ncst_ref[0] = hc_ref[:, ts:ts + CONV_W - 1, :]
    yn = _silu(_layer_norm(y.reshape(ROWS, GROUP_W) + cb_ref[0], cng_ref[0], cnb_ref[0]))
    o_c = jnp.dot(yn.astype(BF16), wpw_ref[0], preferred_element_type=F32) * _silu(z[:, 2304:2560])

    lb = _lower_bound(lb_ref, layer)
    qd, kk, vd, hilo = _hgrn_gates(z[:, 2560:2816], z[:, 2816:3072], z[:, 3072:3328], lb)
    seg = _segsums(g_ref, G_CUM8, G_LVL0 + 3, hilo)
    o = _hgrn_intra(qd, kk, vd, [seg[ROWS * (2 + n):ROWS * (3 + n)] for n in range(3)], e_ref, range(3))
    aq = (qd * jnp.exp(seg[0:ROWS])).reshape(nseq, ts, GROUP_W)
    bk = (kk * jnp.exp(seg[ROWS:2 * ROWS])).reshape(nseq, ts, GROUP_W)
    v3 = vd.reshape(nseq, ts, GROUP_W)
    hilo3 = hilo.astype(F32).reshape(nseq, ts, 2 * GROUP_W)
    ones = jnp.ones((nseq, ts, D_KDIM), BF16)
    inter = []
    for hh in range(D_HEADS):
        lo_, hi_ = 64 * hh, 64 * hh + 64
        st = hst_ref[0, :, hh]
        inter.append(jnp.einsum('bqk,bkv->bqv', aq[:, :, lo_:hi_].astype(BF16), st.astype(BF16),
                                preferred_element_type=F32))
        dec = (jnp.einsum('bsk,bsv->bkv', hilo3[:, :, lo_:hi_].astype(BF16), ones, preferred_element_type=F32)
               + jnp.einsum('bsk,bsv->bkv', hilo3[:, :, GROUP_W + lo_:GROUP_W + hi_].astype(BF16), ones,
                            preferred_element_type=F32))
        upd = jnp.einsum('bsk,bsv->bkv', bk[:, :, lo_:hi_].astype(BF16), v3[:, :, lo_:hi_].astype(BF16),
                         preferred_element_type=F32)
        nhst_ref[0, :, hh] = jnp.exp(dec) * st + upd
    o = o + jnp.concatenate(inter, axis=2).reshape(ROWS, GROUP_W)
    o_d = _head_rms(o, e_ref, hng_ref[0]) * _silu(z[:, 3328:3584])
    obcd_ref[...] = jnp.concatenate([o_b, o_c, o_d], axis=1).astype(BF16)


def _sample_pre(layer, x, mod_s, win_b, cos, sin, sgg, sgb, ws_blk, sgbias, cw, cb, cng, cnb, wpw_b, lbnd, hng,
                gmat, eones, state_conv, state_hgrn):
    bs, ts, _ = x.shape
    nseq = ROWS // ts
    nblk = bs // nseq
    n = bs * ts
    kernel = functools.partial(_sample_pre_kernel, layer, ts)
    lsel3 = lambda i: (layer, 0, 0)
    const2 = lambda i: (0, 0)
    const3 = lambda i: (0, 0, 0)
    row_blk = lambda w: pl.BlockSpec((ROWS, w), lambda i: (i, 0))
    out_shape = (
        jax.ShapeDtypeStruct((n, GROUP_W), F32),
        jax.ShapeDtypeStruct((n, GROUP_W), F32),
        jax.ShapeDtypeStruct((n, GROUP_W), F32),
        jax.ShapeDtypeStruct((n, GROUP_W), F32),
        jax.ShapeDtypeStruct((n, 3 * GROUP_W), BF16),
        jax.ShapeDtypeStruct((n, GROUP_W), F32),
        jax.ShapeDtypeStruct((1, bs, CONV_W - 1, GROUP_W), F32),
        jax.ShapeDtypeStruct((1, bs, D_HEADS, D_KDIM, D_KDIM), F32),
    )
    return pl.pallas_call(
        kernel,
        out_shape=out_shape,
        grid=(nblk,),
        in_specs=[
            pl.BlockSpec((nseq, ts, D_MODEL), lambda i: (i, 0, 0)),
            pl.BlockSpec((nseq, 1, 3 * D_MODEL), lambda i: (i, 0, 0)),
            pl.BlockSpec((1, D_MODEL, D_IN), lsel3),
            pl.BlockSpec((ROWS, 128), const2),
            pl.BlockSpec((ROWS, 128), const2),
            pl.BlockSpec((1, 1, GROUP_W), lsel3),
            pl.BlockSpec((1, 1, GROUP_W), lsel3),
            pl.BlockSpec((1, 4, ROWS, ROWS), lambda i: (layer, 0, 0, 0)),
            pl.BlockSpec((1, ROWS, GROUP_W), lsel3),
            pl.BlockSpec((1, CONV_W, GROUP_W), lsel3),
            pl.BlockSpec((1, 1, GROUP_W), lsel3),
            pl.BlockSpec((1, 1, GROUP_W), lsel3),
            pl.BlockSpec((1, 1, GROUP_W), lsel3),
            pl.BlockSpec((1, GROUP_W, GROUP_W), lsel3),
            pl.BlockSpec(lbnd.shape, const2),
            pl.BlockSpec((1, 1, GROUP_W), lsel3),
            pl.BlockSpec(gmat.shape, const3),
            pl.BlockSpec(eones.shape, const2),
            pl.BlockSpec((1, nseq, CONV_W - 1, GROUP_W), lambda i: (layer, i, 0, 0)),
            pl.BlockSpec((1, nseq, D_HEADS, D_KDIM, D_KDIM), lambda i: (layer, i, 0, 0, 0)),
        ],
        out_specs=(
            row_blk(GROUP_W), row_blk(GROUP_W), row_blk(GROUP_W), row_blk(GROUP_W), row_blk(3 * GROUP_W),
            row_blk(GROUP_W),
            pl.BlockSpec((1, nseq, CONV_W - 1, GROUP_W), lambda i: (0, i, 0, 0)),
            pl.BlockSpec((1, nseq, D_HEADS, D_KDIM, D_KDIM), lambda i: (0, i, 0, 0, 0)),
        ),
        scratch_shapes=[pltpu.VMEM((nseq, 40, GROUP_W), F32)],
        compiler_params=pltpu.CompilerParams(dimension_semantics=("arbitrary",), vmem_limit_bytes=VMEM_LIMIT),
        name=f"sample_pre_l{layer}",
    )(x, mod_s, win_b, cos, sin, sgg, sgb, ws_blk, sgbias, cw, cb, cng, cnb, wpw_b, lbnd, hng, gmat, eones,
      state_conv, state_hgrn)


def _sample_attn_kernel(layer, n_pages, page, ts,
                        pt_ref, ck_hbm, cv_hbm, q_ref, kn_ref, vn_ref, ga_ref, lam_ref, ang_ref, e_ref, o_ref,
                        kbuf, vbuf, sem):
    b = pl.program_id(0)
    nb = pl.num_programs(0)

    def page_copies(seq, slot):
        copies = []
        for j in range(n_pages):
            pid = pt_ref[seq, j]
            dst = pl.ds(j * page, page)
            copies.append(pltpu.make_async_copy(ck_hbm.at[layer, pid], kbuf.at[slot, :, dst], sem.at[0, slot]))
            copies.append(pltpu.make_async_copy(cv_hbm.at[layer, pid], vbuf.at[slot, :, dst], sem.at[1, slot]))
        return copies

    @pl.when(b == 0)
    def _():
        for cp in page_copies(0, 0):
            cp.start()

    @pl.when(b + 1 < nb)
    def _():
        for cp in page_copies(b + 1, (b + 1) % 2):
            cp.start()

    slot = b % 2
    for cp in page_copies(b, slot):
        cp.wait()

    grp = lax.broadcasted_iota(jnp.int32, (1, GROUP_W), 1) >> 5
    head = lax.broadcasted_iota(jnp.int32, (1, GROUP_W), 1) >> 6
    t_q = lax.broadcasted_iota(jnp.int32, (8 * ts, ts), 0) & (ts - 1)
    t_k = lax.broadcasted_iota(jnp.int32, (8 * ts, ts), 1)
    vis = t_k <= t_q
    lam = _lam(lam_ref) + _lam_init(layer)
    q = q_ref[0]
    qexp = jnp.concatenate([jnp.where(grp == j, q, 0.0) for j in range(8)], axis=0).astype(BF16)
    kt_all = kbuf[slot].astype(BF16)
    vt_all = vbuf[slot].astype(BF16)
    s_past = jnp.dot(qexp, kt_all, preferred_element_type=F32)
    s_new = lax.dot_general(qexp, kn_ref[0].astype(BF16), _NT, preferred_element_type=F32)
    s_new = jnp.where(vis, s_new, NEG_BIG)
    m = jnp.maximum(jnp.max(s_past, axis=-1, keepdims=True), jnp.max(s_new, axis=-1, keepdims=True))
    p_past = jnp.exp2(s_past - m)
    p_new = jnp.where(vis, jnp.exp2(s_new - m), 0.0)
    l = jnp.sum(p_past, axis=-1, keepdims=True) + jnp.sum(p_new, axis=-1, keepdims=True)
    o = (lax.dot_general(p_past.astype(BF16), vt_all, _NT, preferred_element_type=F32)
         + jnp.dot(p_new.astype(BF16), vn_ref[0].astype(BF16), preferred_element_type=F32)) / l
    o_a = jnp.zeros((ts, GROUP_W), F32)
    for hh in range(A_HEADS):
        d = o[2 * hh * ts:(2 * hh + 1) * ts] - lam * o[(2 * hh + 1) * ts:(2 * hh + 2) * ts]
        o_a = o_a + jnp.where(head == hh, d, 0.0)
    o_a = _head_rms(o_a, e_ref, ang_ref[0] * (1.0 - _lam_init(layer))) * ga_ref[0]
    o_ref[0] = o_a.astype(BF16)


def _sample_attn(layer, page_table, cache_k, cache_v, q_s, k_new, v_new, ga_s, lam_qk, ang, eones, ts):
    bs, n_pages = page_table.shape
    page = cache_k.shape[3]
    past = n_pages * page
    kernel = functools.partial(_sample_attn_kernel, layer, n_pages, page, ts)
    seq_blk = pl.BlockSpec((1, ts, GROUP_W), lambda b, pt: (b, 0, 0))
    lsel3 = lambda b, pt: (layer, 0, 0)
    grid_spec = pltpu.PrefetchScalarGridSpec(
        num_scalar_prefetch=1,
        grid=(bs,),
        in_specs=[pl.BlockSpec(memory_space=pl.ANY), pl.BlockSpec(memory_space=pl.ANY),
                  seq_blk, seq_blk, seq_blk, seq_blk,
                  pl.BlockSpec((1, 4, A_HALF), lsel3),
                  pl.BlockSpec((1, 1, GROUP_W), lsel3),
                  pl.BlockSpec(eones.shape, lambda b, pt: (0, 0))],
        out_specs=seq_blk,
        scratch_shapes=[pltpu.VMEM((2, GROUP_W, past), F32), pltpu.VMEM((2, GROUP_W, past), F32),
                        pltpu.SemaphoreType.DMA((2, 2))],
    )
    shp3 = (bs, ts, GROUP_W)
    return pl.pallas_call(
        kernel,
        out_shape=jax.ShapeDtypeStruct(shp3, BF16),
        grid_spec=grid_spec,
        compiler_params=pltpu.CompilerParams(dimension_semantics=("arbitrary",), vmem_limit_bytes=VMEM_LIMIT),
        name=f"sample_attn_l{layer}",
    )(page_table, cache_k, cache_v,
      q_s.reshape(shp3), k_new.reshape(shp3), v_new.reshape(shp3), ga_s.reshape(shp3), lam_qk, ang, eones)


def _sample_out_kernel(depth, ts, oa_ref, obcd_ref, x_ref, mod_ref, wout_ref, lng_ref, lnb_ref, y_ref):
    nseq = ROWS // ts
    mixed = (jnp.dot(oa_ref[...], wout_ref[0, 0:GROUP_W, :], preferred_element_type=F32)
             + jnp.dot(obcd_ref[...], wout_ref[0, GROUP_W:, :], preferred_element_type=F32))
    gate = mod_ref[...][:, :, 2 * D_MODEL:]
    y = _alpha(depth) * x_ref[...] + gate * mixed.reshape(nseq, ts, D_MODEL)
    y_ref[...] = _layer_norm(y, lng_ref[0], lnb_ref[0])


def _sample_out(layer, depth, oa, obcd, x, mod_s, wout_b, lng, lnb):
    bs, ts, _ = x.shape
    nseq = ROWS // ts
    lsel3 = lambda i: (layer, 0, 0)
    return pl.pallas_call(
        functools.partial(_sample_out_kernel, depth, ts),
        out_shape=jax.ShapeDtypeStruct(x.shape, F32),
        grid=(bs // nseq,),
        in_specs=[
            pl.BlockSpec((ROWS, GROUP_W), lambda i: (i, 0)),
            pl.BlockSpec((ROWS, 3 * GROUP_W), lambda i: (i, 0)),
            pl.BlockSpec((nseq, ts, D_MODEL), lambda i: (i, 0, 0)),
            pl.BlockSpec((nseq, 1, 3 * D_MODEL), lambda i: (i, 0, 0)),
            pl.BlockSpec((1, D_MODEL, D_MODEL), lsel3),
            pl.BlockSpec((1, 1, D_MODEL), lsel3),
            pl.BlockSpec((1, 1, D_MODEL), lsel3),
        ],
        out_specs=pl.BlockSpec((nseq, ts, D_MODEL), lambda i: (i, 0, 0)),
        compiler_params=pltpu.CompilerParams(dimension_semantics=("arbitrary",), vmem_limit_bytes=VMEM_LIMIT),
        name=f"sample_out_l{layer}",
    )(oa, obcd, x, mod_s, wout_b, lng, lnb)


def _rope_tables(pos):
    half = A_HALF // 2
    inv = ROPE_THETA ** (-jnp.arange(half, dtype=F32) * 2.0 / A_HALF)
    ang = pos.astype(F32)[:, None] * inv[None, :]
    cos = jnp.cos(ang)
    sin = jnp.sin(ang)
    return jnp.tile(jnp.concatenate([cos, cos], -1), (1, 4)), jnp.tile(jnp.concatenate([-sin, sin], -1), (1, 4))


def kernel(x_prompt, x_sample, cache_k, cache_v, state_conv, state_hgrn, page_table, c_prompt, c_sample, w_ada, b_ada, w_in, lam_qk, attn_norm_g, sg_norm_g, sg_norm_b, w_s, b_s, conv_w, conv_b, conv_norm_g, conv_norm_b, w_pw, lower_bounds, hgrn_norm_g, w_out, ln_g, ln_b):
    depth = w_in.shape[0]
    bp, t, _ = x_prompt.shape
    bs, ts, _ = x_sample.shape
    n_pool, page = cache_k.shape[1], cache_k.shape[2]
    past_len = page_table.shape[1] * page
    assert ts == 8 and ROWS % ts == 0 and bs % (ROWS // ts) == 0 and t % ROWS == 0
    nseq = ROWS // ts

    gmat = _segment_matrices()
    eones = _head_ones()
    win_b = w_in.astype(BF16)
    wout_b = w_out.astype(BF16)
    wpw_b = w_pw.astype(BF16)
    row3 = lambda a: a.reshape(depth, 1, a.shape[-1])
    sgg, sgb, cb, cng, cnb = row3(sg_norm_g), row3(sg_norm_b), row3(conv_b), row3(conv_norm_g), row3(conv_norm_b)
    lng, lnb = row3(ln_g), row3(ln_b)
    ang = row3(jnp.tile(attn_norm_g, (1, A_HEADS)))
    hng = row3(jnp.tile(hgrn_norm_g, (1, D_HEADS)))
    sgbias_p = jnp.repeat(jnp.swapaxes(b_s, 1, 2), GROUP_W // 4, axis=2)
    sgbias_s = jnp.tile(sgbias_p[:, :ts], (1, nseq, 1))
    eye = jnp.eye(nseq, dtype=F32)
    ws_blk = jnp.einsum('ab,lgts->lgatbs', eye, w_s[:, :, :ts, :ts]).reshape(depth, 4, ROWS, ROWS)
    cos_p, sin_p = _rope_tables(jnp.arange(t))
    cos_s, sin_s = _rope_tables(past_len + jnp.arange(ts))
    cos_s, sin_s = jnp.tile(cos_s, (nseq, 1)), jnp.tile(sin_s, (nseq, 1))
    ck = jnp.transpose(cache_k, (0, 1, 3, 4, 2)).reshape(depth, n_pool, GROUP_W, page)
    cv = jnp.transpose(cache_v, (0, 1, 3, 4, 2)).reshape(depth, n_pool, GROUP_W, page)

    rows = bp + bs
    pad = (-rows) % 8
    c_all = jnp.concatenate([c_prompt, c_sample, jnp.zeros((pad, D_MODEL), F32)], axis=0)
    mod = _ada(c_all, w_ada, b_ada)

    xp, xs = x_prompt, x_sample
    outs = [[] for _ in range(9)]
    for l in range(depth):
        mod_p = mod[l, :bp].reshape(bp, 1, 3 * D_MODEL)
        mod_s = mod[l, bp:bp + bs].reshape(bs, 1, 3 * D_MODEL)
        krow, vrow, qt, kb, vt, ga, obcd, cst_p, hst_p = _prompt_pre(
            l, xp, mod_p, win_b, cos_p, sin_p, sgg, sgb, w_s, sgbias_p, conv_w, cb, cng, cnb, wpw_b, lower_bounds,
            hng, gmat, eones)
        xp = _prompt_attn(l, depth, qt, kb, vt, ga, obcd, xp, mod_p, wout_b, lam_qk, ang, lng, lnb)

        krow_s, vrow_s, q_s, ga_s, obcd_s, chv_s, cst_s, hst_s = _sample_pre(
            l, xs, mod_s, win_b, cos_s, sin_s, sgg, sgb, ws_blk, sgbias_s, conv_w, cb, cng, cnb, wpw_b,
            lower_bounds, hng, gmat, eones, state_conv, state_hgrn)
        oa_s = _sample_attn(l, page_table, ck, cv, q_s, krow_s, vrow_s, ga_s, lam_qk, ang, eones, ts)
        xs = _sample_out(l, depth, oa_s.reshape(bs * ts, GROUP_W), obcd_s, xs, mod_s, wout_b, lng, lnb)

        outs[0].append(jnp.transpose(krow.reshape(bp, A_HEADS, 2 * A_HALF, t), (0, 3, 1, 2)))
        outs[1].append(jnp.transpose(vrow.reshape(bp, A_HEADS, A_VDIM, t), (0, 3, 1, 2)))
        outs[2].append(krow_s.reshape(bs, ts, A_HEADS, 2 * A_HALF))
        outs[3].append(vrow_s.reshape(bs, ts, A_HEADS, A_VDIM))
        outs[4].append(chv_s.reshape(bs, ts, GROUP_W))
        outs[5].append(cst_p)
        outs[6].append(cst_s[0])
        outs[7].append(hst_p)
        outs[8].append(hst_s[0])
    return (xp, xs) + tuple(jnp.stack(o) for o in outs)
```

```python
import functools
import math

import jax
import jax.numpy as jnp
import numpy as np
from jax import lax
from jax.experimental import pallas as pl
from jax.experimental.pallas import tpu as pltpu

F32 = jnp.float32
BF16 = jnp.bfloat16

D_MODEL = 1024
GROUP_W = 256
A_HEADS = 4
A_HALF = 32
A_VDIM = 64
ROPE_THETA = 10000.0
SG_CHUNK = 128
CONV_W = 31
D_HEADS = 4
D_KDIM = 64
F_FLOOR = 1e-30
EPS = 1e-5
NEG_BIG = -1e30
D_IN = 14 * GROUP_W
SQRT_HALF = 0.7071067811865476
Q_SCALE = (A_HALF ** -0.5) * math.log2(math.e)

ROWS = 128
PRE_TB = 512
ATTN_TQ = 512
ATTN_TK = 512
LANE_TILE = 256
SCORE_LEAD = 8
VMEM_LIMIT = 56 * 1024 * 1024

G_CUM8, G_SUF8, G_LVL0, G_CUM, G_SUF = 0, 1, 2, 9, 10

_NT = (((1,), (1,)), ((), ()))
_TN = (((0,), (0,)), ((), ()))


def _segment_matrices():
    t = np.arange(ROWS)[:, None]
    s = np.arange(ROWS)[None, :]
    same8 = (t >> 3) == (s >> 3)
    mats = [same8 & (s <= t), same8 & (s > t)]
    for lvl in range(7):
        mid = ((t >> (lvl + 1)) << (lvl + 1)) + (1 << lvl)
        second = ((t >> lvl) & 1) == 1
        mats.append(np.where(second, (s >= mid) & (s <= t), (s > t) & (s < mid)))
    mats.append(s <= t)
    mats.append(s > t)
    return jnp.asarray(np.stack(mats).astype(np.float32), dtype=BF16)


def _head_ones():
    h = np.arange(GROUP_W) // 64
    return jnp.asarray((h[:, None] == h[None, :]).astype(np.float32), dtype=BF16)


def _silu(x):
    return x * jax.nn.sigmoid(x)


def _gelu(x):
    return 0.5 * x * (1.0 + lax.erf(x * SQRT_HALF))


def _layer_norm(x, g, b):
    xc = x - jnp.mean(x, axis=-1, keepdims=True)
    var = jnp.mean(xc * xc, axis=-1, keepdims=True)
    return xc * lax.rsqrt(var + EPS) * g + b


def _head_rms(x, e_ref, g):
    ms = jnp.dot((x * x).astype(BF16), e_ref[...], preferred_element_type=F32) * (1.0 / 64.0)
    return x * lax.rsqrt(ms + EPS) * g


def _rope(x, cos, sin):
    outs = []
    for half in range(2):
        xh = x[:, 128 * half:128 * half + 128]
        lane = lax.broadcasted_iota(jnp.int32, xh.shape, 1)
        partner = jnp.where((lane & 16) == 0, pltpu.roll(xh, 112, 1), pltpu.roll(xh, 16, 1))
        outs.append(xh * cos + partner * sin)
    return jnp.concatenate(outs, axis=1)


def _lower_bound(lb_ref, layer):
    lb = lb_ref[...]
    e = jnp.exp(lb - jnp.max(lb, axis=0, keepdims=True))
    soft = e / jnp.sum(e, axis=0, keepdims=True)
    acc = jnp.zeros((1, GROUP_W), F32)
    for i in range(1, layer + 1):
        acc = acc + soft[i:i + 1]
    return acc


def _lam(lam_ref):
    lp = lam_ref[0]
    a = jnp.sum(lp[0:1] * lp[1:2], axis=-1, keepdims=True)
    b = jnp.sum(lp[2:3] * lp[3:4], axis=-1, keepdims=True)
    return jnp.exp(a) - jnp.exp(b)


def _lam_init(layer):
    return 0.8 - 0.6 * math.exp(-0.3 * layer)


def _alpha(depth):
    return (2.0 * depth) ** 0.25


def _sg_mix(vn, ws_ref, bias):
    group = lax.broadcasted_iota(jnp.int32, (1, GROUP_W), 1) >> 6
    r = lax.broadcasted_iota(jnp.int32, (ROWS, ROWS), 0)
    c = lax.broadcasted_iota(jnp.int32, (ROWS, ROWS), 1)
    acc = bias
    for g in range(4):
        wm = jnp.where(r >= c, ws_ref[g], 0.0).astype(BF16)
        vm = jnp.where(group == g, vn, 0.0).astype(BF16)
        acc = acc + jnp.dot(wm, vm, preferred_element_type=F32)
    return acc


def _hgrn_gates(dq, df, di, lb):
    f = lb + (1.0 - lb) * jax.nn.sigmoid(df)
    lf = jnp.log(jnp.maximum(f, F_FLOOR))
    hi = lf.astype(BF16)
    lo = (lf - hi.astype(F32)).astype(BF16)
    return _silu(dq), 1.0 - f, di, jnp.concatenate([hi, lo], axis=1)


def _segsums(g_ref, lo, hi, hilo):
    g = g_ref[lo:hi].reshape((hi - lo) * ROWS, ROWS)
    r = jnp.dot(g, hilo, preferred_element_type=F32)
    return r[:, :GROUP_W] + r[:, GROUP_W:]


def _hgrn_intra(qd, kk, vd, seg, e_ref, levels):
    row = lax.broadcasted_iota(jnp.int32, (ROWS, 1), 0)
    head = lax.broadcasted_iota(jnp.int32, (1, GROUP_W), 1) >> 6
    t_idx = lax.broadcasted_iota(jnp.int32, (ROWS, 4 * ROWS), 0)
    s_idx = lax.broadcasted_iota(jnp.int32, (ROWS, 4 * ROWS), 1) & (ROWS - 1)
    att = jnp.zeros((ROWS, 4 * ROWS), F32)
    for i, lvl in enumerate(levels):
        e = jnp.exp(seg[i])
        second = ((row >> lvl) & 1) == 1
        a = jnp.where(second, qd * e, 0.0).astype(BF16)
        b = jnp.where(second, 0.0, kk * e)
        b_heads = jnp.concatenate([jnp.where(head == h, b, 0.0) for h in range(4)], axis=0).astype(BF16)
        s = lax.dot_general(a, b_heads, _NT, preferred_element_type=F32)
        att = att + jnp.where((t_idx >> (lvl + 1)) == (s_idx >> (lvl + 1)), s, 0.0)
    v_stack = jnp.concatenate([jnp.where(head == h, vd, 0.0) for h in range(4)], axis=0).astype(BF16)
    o = jnp.dot(att.astype(BF16), v_stack, preferred_element_type=F32)
    diag = jnp.dot((qd * kk).astype(BF16), e_ref[...], preferred_element_type=F32)
    return o + diag * vd


def _ada_kernel(c_ref, w_ref, b_ref, o_ref):
    c = c_ref[...]
    a = _silu(c).astype(BF16)
    o_ref[0] = jnp.dot(a, w_ref[0].astype(BF16), preferred_element_type=F32) + b_ref[0]


def _ada(c_all, w_ada, b_ada):
    depth = w_ada.shape[0]
    rows = c_all.shape[0]
    return pl.pallas_call(
        _ada_kernel,
        out_shape=jax.ShapeDtypeStruct((depth, rows, 3 * D_MODEL), F32),
        grid=(depth, 3),
        in_specs=[
            pl.BlockSpec((rows, D_MODEL), lambda l, n: (0, 0)),
            pl.BlockSpec((1, D_MODEL, D_MODEL), lambda l, n: (l, 0, n)),
            pl.BlockSpec((1, 1, D_MODEL), lambda l, n: (l, 0, n)),
        ],
        out_specs=pl.BlockSpec((1, rows, D_MODEL), lambda l, n: (l, 0, n)),
        compiler_params=pltpu.CompilerParams(dimension_semantics=("arbitrary", "arbitrary"),
                                             vmem_limit_bytes=VMEM_LIMIT),
        name="adaln",
    )(c_all, w_ada, b_ada.reshape(depth, 1, 3 * D_MODEL))


def _prompt_pre_kernel(layer, tb,
                       x_ref, mod_ref, win_ref, cos_ref, sin_ref, sgg_ref, sgb_ref, ws_ref, sgbias_ref,
                       cw_ref, cb_ref, cng_ref, cnb_ref, wpw_ref, lb_ref, hng_ref, g_ref, e_ref,
                       krow_ref, vrow_ref, qt_ref, kb_ref, vt_ref, ga_ref, obcd_ref, cst_ref, hst_ref,
                       hc_ref, s_ref):
    i = pl.program_id(1)

    @pl.when(i == 0)
    def _():
        hc_ref[0:32, :] = jnp.zeros((32, GROUP_W), F32)
        s_ref[...] = jnp.zeros(s_ref.shape, F32)

    mod = mod_ref[0]
    shift = mod[:, 0:D_MODEL]
    scale = mod[:, D_MODEL:2 * D_MODEL]
    h = (x_ref[0] * (1.0 + scale) + shift).astype(BF16)
    zc = jnp.dot(h, win_ref[0, :, 1792:2560], preferred_element_type=F32)
    zb = jnp.dot(h, win_ref[0, :, 1024:1792], preferred_element_type=F32)
    zd = jnp.dot(h, win_ref[0, :, 2560:3584], preferred_element_type=F32)

    glu = zc[:, 0:GROUP_W] * jax.nn.sigmoid(zc[:, GROUP_W:2 * GROUP_W])
    hc_ref[32:32 + tb, :] = glu
    cst_ref[0] = hc_ref[pl.ds(tb + 2, CONV_W - 1), :]

    lb = _lower_bound(lb_ref, layer)
    o_b_parts, o_c_parts, o_d_parts = [], [], []
    for c in range(tb // ROWS):
        rs = slice(c * ROWS, (c + 1) * ROWS)
        y = jnp.zeros((ROWS, GROUP_W), F32)
        for j in range(CONV_W):
            y = y + hc_ref[pl.ds(c * ROWS + 2 + j, ROWS), :] * cw_ref[0, j:j + 1, :]
        yn = _silu(_layer_norm(y + cb_ref[0], cng_ref[0], cnb_ref[0]))
        o_c_parts.append(jnp.dot(yn.astype(BF16), wpw_ref[0], preferred_element_type=F32) * _silu(zc[rs, 512:768]))
        vn = _layer_norm(_gelu(zb[rs, 256:512]), sgg_ref[0], sgb_ref[0])
        mixed = _sg_mix(vn, ws_ref.at[0], sgbias_ref[0])
        o_b_parts.append(_gelu(zb[rs, 0:256]) * mixed * _silu(zb[rs, 512:768]))
        qd, kk, vd, hilo = _hgrn_gates(zd[rs, 0:256], zd[rs, 256:512], zd[rs, 512:768], lb)
        seg = _segsums(g_ref, G_LVL0, G_SUF + 1, hilo)
        o = _hgrn_intra(qd, kk, vd, [seg[ROWS * n:ROWS * (n + 1)] for n in range(7)], e_ref, range(7))
        aq = qd * jnp.exp(seg[7 * ROWS:8 * ROWS])
        bk = kk * jnp.exp(seg[8 * ROWS:9 * ROWS])
        ones = jnp.ones((ROWS, D_KDIM), BF16)
        inter = []
        for hh in range(D_HEADS):
            lo_, hi_ = 64 * hh, 64 * hh + 64
            st = s_ref[hh]
            inter.append(jnp.dot(aq[:, lo_:hi_].astype(BF16), st.astype(BF16), preferred_element_type=F32))
            dec = (lax.dot_general(hilo[:, lo_:hi_], ones, _TN, preferred_element_type=F32)
                   + lax.dot_general(hilo[:, GROUP_W + lo_:GROUP_W + hi_], ones, _TN, preferred_element_type=F32))
            upd = lax.dot_general(bk[:, lo_:hi_].astype(BF16), vd[:, lo_:hi_].astype(BF16), _TN,
                                  preferred_element_type=F32)
            s_ref[hh] = jnp.exp(dec) * st + upd
        o = o + jnp.concatenate(inter, axis=1)
        o_d_parts.append(_head_rms(o, e_ref, hng_ref[0]) * _silu(zd[rs, 768:1024]))
    za = jnp.dot(h, win_ref[0, :, 0:1024], preferred_element_type=F32)
    cos = cos_ref[...]
    sin = sin_ref[...]
    q = _rope(za[:, 0:256], cos, sin)
    k = _rope(za[:, 256:512], cos, sin)
    v = za[:, 512:768]
    kt = k.T
    vt = v.T
    krow_ref[0] = kt
    vrow_ref[0] = vt
    kb_ref[0] = k.astype(BF16)
    qt_ref[0] = (q * Q_SCALE).T.astype(BF16)
    vt_ref[0] = vt.astype(BF16)
    ga_ref[0] = _silu(za[:, 768:1024])

    hc_ref[0:32, :] = hc_ref[pl.ds(tb, 32), :]
    o_b = jnp.concatenate(o_b_parts, axis=0)
    o_c = jnp.concatenate(o_c_parts, axis=0)
    o_d = jnp.concatenate(o_d_parts, axis=0)
    obcd_ref[0] = jnp.concatenate([o_b, o_c, o_d], axis=1).astype(BF16)
    hst_ref[0] = s_ref[...]


def _prompt_pre(layer, x, mod_p, win_b, cos, sin, sgg, sgb, ws, sgbias, cw, cb, cng, cnb, wpw_b, lbnd, hng,
                gmat, eones):
    b, t, _ = x.shape
    tb = min(PRE_TB, t)
    kernel = functools.partial(_prompt_pre_kernel, layer, tb)
    lsel3 = lambda bb, i: (layer, 0, 0)
    const2 = lambda bb, i: (0, 0)
    const3 = lambda bb, i: (0, 0, 0)
    row_blk = lambda w: pl.BlockSpec((1, tb, w), lambda bb, i: (bb, i, 0))
    col_blk = pl.BlockSpec((1, GROUP_W, tb), lambda bb, i: (bb, 0, i))
    out_shape = (
        jax.ShapeDtypeStruct((b, GROUP_W, t), F32),
        jax.ShapeDtypeStruct((b, GROUP_W, t), F32),
        jax.ShapeDtypeStruct((b, GROUP_W, t), BF16),
        jax.ShapeDtypeStruct((b, t, GROUP_W), BF16),
        jax.ShapeDtypeStruct((b, GROUP_W, t), BF16),
        jax.ShapeDtypeStruct((b, t, GROUP_W), F32),
        jax.ShapeDtypeStruct((b, t, 3 * GROUP_W), BF16),
        jax.ShapeDtypeStruct((b, CONV_W - 1, GROUP_W), F32),
        jax.ShapeDtypeStruct((b, D_HEADS, D_KDIM, D_KDIM), F32),
    )
    return pl.pallas_call(
        kernel,
        out_shape=out_shape,
        grid=(b, t // tb),
        in_specs=[
            row_blk(D_MODEL),
            pl.BlockSpec((1, 1, 3 * D_MODEL), lambda bb, i: (bb, 0, 0)),
            pl.BlockSpec((1, D_MODEL, D_IN), lsel3),
            pl.BlockSpec((tb, 128), lambda bb, i: (i, 0)),
            pl.BlockSpec((tb, 128), lambda bb, i: (i, 0)),
            pl.BlockSpec((1, 1, GROUP_W), lsel3),
            pl.BlockSpec((1, 1, GROUP_W), lsel3),
            pl.BlockSpec((1, 4, ROWS, ROWS), lambda bb, i: (layer, 0, 0, 0)),
            pl.BlockSpec((1, ROWS, GROUP_W), lsel3),
            pl.BlockSpec((1, CONV_W, GROUP_W), lsel3),
            pl.BlockSpec((1, 1, GROUP_W), lsel3),
            pl.BlockSpec((1, 1, GROUP_W), lsel3),
            pl.BlockSpec((1, 1, GROUP_W), lsel3),
            pl.BlockSpec((1, GROUP_W, GROUP_W), lsel3),
            pl.BlockSpec(lbnd.shape, const2),
            pl.BlockSpec((1, 1, GROUP_W), lsel3),
            pl.BlockSpec(gmat.shape, const3),
            pl.BlockSpec(eones.shape, const2),
        ],
        out_specs=(
            col_blk, col_blk, col_blk, row_blk(GROUP_W), col_blk, row_blk(GROUP_W),
            row_blk(3 * GROUP_W),
            pl.BlockSpec((1, CONV_W - 1, GROUP_W), lambda bb, i: (bb, 0, 0)),
            pl.BlockSpec((1, D_HEADS, D_KDIM, D_KDIM), lambda bb, i: (bb, 0, 0, 0)),
        ),
        scratch_shapes=[pltpu.VMEM((32 + tb, GROUP_W), F32), pltpu.VMEM((D_HEADS, D_KDIM, D_KDIM), F32)],
        compiler_params=pltpu.CompilerParams(dimension_semantics=("arbitrary", "arbitrary"),
                                             vmem_limit_bytes=VMEM_LIMIT),
        name=f"prompt_pre_l{layer}",
    )(x, mod_p, win_b, cos, sin, sgg, sgb, ws, sgbias, cw, cb, cng, cnb, wpw_b, lbnd, hng, gmat, eones)


def _prompt_attn_kernel(layer, depth, tq, tk,
                        qt_ref, kb_ref, vt_ref, ga_ref, obcd_ref, x_ref, mod_ref, wout_ref, lam_ref, ang_ref,
                        lng_ref, lnb_ref, y_ref, acc_ref, m_ref, qm_ref, s_ref):
    i = pl.program_id(1)
    qt = qt_ref[0]
    rowg = lax.broadcasted_iota(jnp.int32, (GROUP_W, 1), 0) >> 5
    for j in range(8):
        qm_ref[j] = jnp.where(rowg == j, qt, jnp.zeros_like(qt))
    m_ref[...] = jnp.full(m_ref.shape, NEG_BIG, F32)
    acc_ref[...] = jnp.zeros(acc_ref.shape, F32)
    ones = jnp.ones((16, tk), BF16)
    kpq = tq // tk

    def tile(kt, diag):
        masked = diag is not None
        off = pl.multiple_of(kt * tk, tk)
        kk = kb_ref[0, pl.ds(off, tk), :]
        if masked:
            key = lax.broadcasted_iota(jnp.int32, (tk, tq), 0) + diag * tk
            qry = lax.broadcasted_iota(jnp.int32, (tk, tq), 1)
            visible = key <= qry
        vvs = [jnp.concatenate([vt_ref[0, 64 * hh:64 * hh + 64, pl.ds(off, tk)], ones], axis=0)
               for hh in range(A_HEADS)]

        def scores(j):
            s = jnp.dot(kk, qm_ref[j], preferred_element_type=F32)
            if masked:
                s = jnp.where(visible, s, NEG_BIG)
            s_ref[j] = s
            return jnp.max(s, axis=0, keepdims=True)

        def update(j, cmax):
            m_old = m_ref[j]
            m_new = jnp.maximum(m_old, cmax)
            alpha = jnp.exp2(m_old - m_new)
            m_ref[j] = m_new
            for c in range(tq // LANE_TILE):
                cols = slice(LANE_TILE * c, LANE_TILE * (c + 1))
                p = jnp.exp2(s_ref[j, :, cols] - m_new[:, cols]).astype(BF16)
                acc_ref[j, :, cols] = (alpha[:, cols] * acc_ref[j, :, cols]
                                       + jnp.dot(vvs[j // 2], p, preferred_element_type=F32))

        cmax = {}
        for j in range(8 + SCORE_LEAD):
            if j < 8:
                cmax[j] = scores(j)
            if j >= SCORE_LEAD:
                update(j - SCORE_LEAD, cmax[j - SCORE_LEAD])

    def body(kt, carry):
        tile(kt, None)
        return carry

    lax.fori_loop(0, i * kpq, body, 0)
    for dg in range(kpq):
        tile(i * kpq + dg, dg)

    lam = _lam(lam_ref) + _lam_init(layer)
    heads = []
    for hh in range(A_HEADS):
        a0 = acc_ref[2 * hh]
        a1 = acc_ref[2 * hh + 1]
        d = a0[0:A_VDIM] / a0[A_VDIM:A_VDIM + 1] - lam * (a1[0:A_VDIM] / a1[A_VDIM:A_VDIM + 1])
        ms = jnp.mean(d * d, axis=0, keepdims=True)
        heads.append(d * lax.rsqrt(ms + EPS))
    o_a = jnp.concatenate(heads, axis=0).T
    o_a = o_a * (ang_ref[0] * (1.0 - _lam_init(layer))) * ga_ref[0]
    mixed = (jnp.dot(o_a.astype(BF16), wout_ref[0, 0:GROUP_W, :], preferred_element_type=F32)
             + jnp.dot(obcd_ref[0], wout_ref[0, GROUP_W:, :], preferred_element_type=F32))
    gate = mod_ref[0][:, 2 * D_MODEL:]
    y_ref[0] = _layer_norm(_alpha(depth) * x_ref[0] + gate * mixed, lng_ref[0], lnb_ref[0])


def _prompt_attn(layer, depth, qt, kb, vt, ga, obcd, x, mod_p, wout_b, lam_qk, ang, lng, lnb):
    b, t, _ = x.shape
    tq = min(ATTN_TQ, t)
    tk = min(ATTN_TK, tq)
    kernel = functools.partial(_prompt_attn_kernel, layer, depth, tq, tk)
    lsel3 = lambda bb, i: (layer, 0, 0)
    row_blk = lambda w: pl.BlockSpec((1, tq, w), lambda bb, i: (bb, i, 0))
    return pl.pallas_call(
        kernel,
        out_shape=jax.ShapeDtypeStruct((b, t, D_MODEL), F32),
        grid=(b, t // tq),
        in_specs=[
            pl.BlockSpec((1, GROUP_W, tq), lambda bb, i: (bb, 0, i)),
            pl.BlockSpec((1, t, GROUP_W), lambda bb, i: (bb, 0, 0)),
            pl.BlockSpec((1, GROUP_W, t), lambda bb, i: (bb, 0, 0)),
            row_blk(GROUP_W),
            row_blk(3 * GROUP_W),
            row_blk(D_MODEL),
            pl.BlockSpec((1, 1, 3 * D_MODEL), lambda bb, i: (bb, 0, 0)),
            pl.BlockSpec((1, D_MODEL, D_MODEL), lsel3),
            pl.BlockSpec((1, 4, A_HALF), lsel3),
            pl.BlockSpec((1, 1, GROUP_W), lsel3),
            pl.BlockSpec((1, 1, D_MODEL), lsel3),
            pl.BlockSpec((1, 1, D_MODEL), lsel3),
        ],
        out_specs=row_blk(D_MODEL),
        scratch_shapes=[pltpu.VMEM((8, A_VDIM + 16, tq), F32), pltpu.VMEM((8, 1, tq), F32),
                        pltpu.VMEM((8, GROUP_W, tq), BF16), pltpu.VMEM((8, tk, tq), F32)],
        compiler_params=pltpu.CompilerParams(dimension_semantics=("arbitrary", "arbitrary"),
                                             vmem_limit_bytes=VMEM_LIMIT),
        name=f"prompt_attn_l{layer}",
    )(qt, kb, vt, ga, obcd, x, mod_p, wout_b, lam_qk, ang, lng, lnb)


def _sample_pre_kernel(layer, ts,
                       x_ref, mod_ref, win_ref, cos_ref, sin_ref, sgg_ref, sgb_ref, ws_ref, sgbias_ref,
                       cw_ref, cb_ref, cng_ref, cnb_ref, wpw_ref, lb_ref, hng_ref, g_ref, e_ref, cst_ref, hst_ref,
                       krow_ref, vrow_ref, q_ref, ga_ref, obcd_ref, chv_ref, ncst_ref, nhst_ref,
                       hc_ref):
    nseq = ROWS // ts
    mod = mod_ref[...]
    shift = mod[:, :, 0:D_MODEL]
    scale = mod[:, :, D_MODEL:2 * D_MODEL]
    h = (x_ref[...] * (1.0 + scale) + shift).reshape(ROWS, D_MODEL).astype(BF16)
    z = jnp.dot(h, win_ref[0], preferred_element_type=F32)

    cos = cos_ref[...]
    sin = sin_ref[...]
    k = _rope(z[:, 256:512], cos, sin)
    krow_ref[...] = k
    vrow_ref[...] = z[:, 512:768]
    q_ref[...] = _rope(z[:, 0:256], cos, sin) * Q_SCALE
    ga_ref[...] = _silu(z[:, 768:1024])

    vn = _layer_norm(_gelu(z[:, 1280:1536]), sgg_ref[0], sgb_ref[0])
    chv_ref[...] = vn
    o_b = _gelu(z[:, 1024:1280]) * _sg_mix(vn, ws_ref.at[0], sgbias_ref[0]) * _silu(z[:, 1536:1792])

    a = z[:, 1792:2304]
    glu = a[:, :GROUP_W] * jax.nn.sigmoid(a[:, GROUP_W:])
    hc_ref[:, 0:CONV_W - 1, :] = cst_ref[0]
    hc_ref[:, CONV_W - 1:CONV_W - 1 + ts, :] = glu.reshape(nseq, ts, GROUP_W)
    y = jnp.zeros((nseq, ts, GROUP_W), F32)
    for j in range(CONV_W):
        y = y + hc_ref[:, j:j + ts, :] * cw_ref[0, j:j + 1, :]
    ncst_ref[0] = hc_ref[:, ts:ts + CONV_W - 1, :]
    yn = _silu(_layer_norm(y.reshape(ROWS, GROUP_W) + cb_ref[0], cng_ref[0], cnb_ref[0]))
    o_c = jnp.dot(yn.astype(BF16), wpw_ref[0], preferred_element_type=F32) * _silu(z[:, 2304:2560])

    lb = _lower_bound(lb_ref, layer)
    qd, kk, vd, hilo = _hgrn_gates(z[:, 2560:2816], z[:, 2816:3072], z[:, 3072:3328], lb)
    seg = _segsums(g_ref, G_CUM8, G_LVL0 + 3, hilo)
    o = _hgrn_intra(qd, kk, vd, [seg[ROWS * (2 + n):ROWS * (3 + n)] for n in range(3)], e_ref, range(3))
    aq = (qd * jnp.exp(seg[0:ROWS])).reshape(nseq, ts, GROUP_W)
    bk = (kk * jnp.exp(seg[ROWS:2 * ROWS])).reshape(nseq, ts, GROUP_W)
    v3 = vd.reshape(nseq, ts, GROUP_W)
    hilo3 = hilo.astype(F32).reshape(nseq, ts, 2 * GROUP_W)
    ones = jnp.ones((nseq, ts, D_KDIM), BF16)
    inter = []
    for hh in range(D_HEADS):
        lo_, hi_ = 64 * hh, 64 * hh + 64
        st = hst_ref[0, :, hh]
        inter.append(jnp.einsum('bqk,bkv->bqv', aq[:, :, lo_:hi_].astype(BF16), st.astype(BF16),
                                preferred_element_type=F32))
        dec = (jnp.einsum('bsk,bsv->bkv', hilo3[:, :, lo_:hi_].astype(BF16), ones, preferred_element_type=F32)
               + jnp.einsum('bsk,bsv->bkv', hilo3[:, :, GROUP_W + lo_:GROUP_W + hi_].astype(BF16), ones,
                            preferred_element_type=F32))
        upd = jnp.einsum('bsk,bsv->bkv', bk[:, :, lo_:hi_].astype(BF16), v3[:, :, lo_:hi_].astype(BF16),
                         preferred_element_type=F32)
        nhst_ref[0, :, hh] = jnp.exp(dec) * st + upd
    o = o + jnp.concatenate(inter, axis=2).reshape(ROWS, GROUP_W)
    o_d = _head_rms(o, e_ref, hng_ref[0]) * _silu(z[:, 3328:3584])
    obcd_ref[...] = jnp.concatenate([o_b, o_c, o_d], axis=1).astype(BF16)


def _sample_pre(layer, x, mod_s, win_b, cos, sin, sgg, sgb, ws_blk, sgbias, cw, cb, cng, cnb, wpw_b, lbnd, hng,
                gmat, eones, state_conv, state_hgrn):
    bs, ts, _ = x.shape
    nseq = ROWS // ts
    nblk = bs // nseq
    n = bs * ts
    kernel = functools.partial(_sample_pre_kernel, layer, ts)
    lsel3 = lambda i: (layer, 0, 0)
    const2 = lambda i: (0, 0)
    const3 = lambda i: (0, 0, 0)
    row_blk = lambda w: pl.BlockSpec((ROWS, w), lambda i: (i, 0))
    out_shape = (
        jax.ShapeDtypeStruct((n, GROUP_W), F32),
        jax.ShapeDtypeStruct((n, GROUP_W), F32),
        jax.ShapeDtypeStruct((n, GROUP_W), F32),
        jax.ShapeDtypeStruct((n, GROUP_W), F32),
        jax.ShapeDtypeStruct((n, 3 * GROUP_W), BF16),
        jax.ShapeDtypeStruct((n, GROUP_W), F32),
        jax.ShapeDtypeStruct((1, bs, CONV_W - 1, GROUP_W), F32),
        jax.ShapeDtypeStruct((1, bs, D_HEADS, D_KDIM, D_KDIM), F32),
    )
    return pl.pallas_call(
        kernel,
        out_shape=out_shape,
        grid=(nblk,),
        in_specs=[
            pl.BlockSpec((nseq, ts, D_MODEL), lambda i: (i, 0, 0)),
            pl.BlockSpec((nseq, 1, 3 * D_MODEL), lambda i: (i, 0, 0)),
            pl.BlockSpec((1, D_MODEL, D_IN), lsel3),
            pl.BlockSpec((ROWS, 128), const2),
            pl.BlockSpec((ROWS, 128), const2),
            pl.BlockSpec((1, 1, GROUP_W), lsel3),
            pl.BlockSpec((1, 1, GROUP_W), lsel3),
            pl.BlockSpec((1, 4, ROWS, ROWS), lambda i: (layer, 0, 0, 0)),
            pl.BlockSpec((1, ROWS, GROUP_W), lsel3),
            pl.BlockSpec((1, CONV_W, GROUP_W), lsel3),
            pl.BlockSpec((1, 1, GROUP_W), lsel3),
            pl.BlockSpec((1, 1, GROUP_W), lsel3),
            pl.BlockSpec((1, 1, GROUP_W), lsel3),
            pl.BlockSpec((1, GROUP_W, GROUP_W), lsel3),
            pl.BlockSpec(lbnd.shape, const2),
            pl.BlockSpec((1, 1, GROUP_W), lsel3),
            pl.BlockSpec(gmat.shape, const3),
            pl.BlockSpec(eones.shape, const2),
            pl.BlockSpec((1, nseq, CONV_W - 1, GROUP_W), lambda i: (layer, i, 0, 0)),
            pl.BlockSpec((1, nseq, D_HEADS, D_KDIM, D_KDIM), lambda i: (layer, i, 0, 0, 0)),
        ],
        out_specs=(
            row_blk(GROUP_W), row_blk(GROUP_W), row_blk(GROUP_W), row_blk(GROUP_W), row_blk(3 * GROUP_W),
            row_blk(GROUP_W),
            pl.BlockSpec((1, nseq, CONV_W - 1, GROUP_W), lambda i: (0, i, 0, 0)),
            pl.BlockSpec((1, nseq, D_HEADS, D_KDIM, D_KDIM), lambda i: (0, i, 0, 0, 0)),
        ),
        scratch_shapes=[pltpu.VMEM((nseq, 40, GROUP_W), F32)],
        compiler_params=pltpu.CompilerParams(dimension_semantics=("arbitrary",), vmem_limit_bytes=VMEM_LIMIT),
        name=f"sample_pre_l{layer}",
    )(x, mod_s, win_b, cos, sin, sgg, sgb, ws_blk, sgbias, cw, cb, cng, cnb, wpw_b, lbnd, hng, gmat, eones,
      state_conv, state_hgrn)


def _sample_attn_kernel(layer, n_pages, page, ts,
                        pt_ref, ck_hbm, cv_hbm, q_ref, kn_ref, vn_ref, ga_ref, lam_ref, ang_ref, e_ref, o_ref,
                        kbuf, vbuf, sem):
    b = pl.program_id(0)
    nb = pl.num_programs(0)

    def page_copies(seq, slot):
        copies = []
        for j in range(n_pages):
            pid = pt_ref[seq, j]
            dst = pl.ds(j * page, page)
            copies.append(pltpu.make_async_copy(ck_hbm.at[layer, pid], kbuf.at[slot, :, dst], sem.at[0, slot]))
            copies.append(pltpu.make_async_copy(cv_hbm.at[layer, pid], vbuf.at[slot, :, dst], sem.at[1, slot]))
        return copies

    def start_all(copies):
        for n, cp in enumerate(copies):
            cp.start(priority=n % 2)

    @pl.when(b == 0)
    def _():
        start_all(page_copies(0, 0))

    @pl.when(b + 1 < nb)
    def _():
        start_all(page_copies(b + 1, (b + 1) % 2))

    slot = b % 2
    for cp in page_copies(b, slot):
        cp.wait()

    grp = lax.broadcasted_iota(jnp.int32, (1, GROUP_W), 1) >> 5
    head = lax.broadcasted_iota(jnp.int32, (1, GROUP_W), 1) >> 6
    t_q = lax.broadcasted_iota(jnp.int32, (8 * ts, ts), 0) & (ts - 1)
    t_k = lax.broadcasted_iota(jnp.int32, (8 * ts, ts), 1)
    vis = t_k <= t_q
    lam = _lam(lam_ref) + _lam_init(layer)
    q = q_ref[0]
    qexp = jnp.concatenate([jnp.where(grp == j, q, 0.0) for j in range(8)], axis=0).astype(BF16)
    kt_all = kbuf[slot].astype(BF16)
    vt_all = vbuf[slot].astype(BF16)
    s_past = jnp.dot(qexp, kt_all, preferred_element_type=F32)
    s_new = lax.dot_general(qexp, kn_ref[0].astype(BF16), _NT, preferred_element_type=F32)
    s_new = jnp.where(vis, s_new, NEG_BIG)
    m = jnp.maximum(jnp.max(s_past, axis=-1, keepdims=True), jnp.max(s_new, axis=-1, keepdims=True))
    p_past = jnp.exp2(s_past - m)
    p_new = jnp.where(vis, jnp.exp2(s_new - m), 0.0)
    l = jnp.sum(p_past, axis=-1, keepdims=True) + jnp.sum(p_new, axis=-1, keepdims=True)
    o = (lax.dot_general(p_past.astype(BF16), vt_all, _NT, preferred_element_type=F32)
         + jnp.dot(p_new.astype(BF16), vn_ref[0].astype(BF16), preferred_element_type=F32)) / l
    o_a = jnp.zeros((ts, GROUP_W), F32)
    for hh in range(A_HEADS):
        d = o[2 * hh * ts:(2 * hh + 1) * ts] - lam * o[(2 * hh + 1) * ts:(2 * hh + 2) * ts]
        o_a = o_a + jnp.where(head == hh, d, 0.0)
    o_a = _head_rms(o_a, e_ref, ang_ref[0] * (1.0 - _lam_init(layer))) * ga_ref[0]
    o_ref[0] = o_a.astype(BF16)


def _sample_attn(layer, page_table, cache_k, cache_v, q_s, k_new, v_new, ga_s, lam_qk, ang, eones, ts):
    bs, n_pages = page_table.shape
    page = cache_k.shape[3]
    past = n_pages * page
    kernel = functools.partial(_sample_attn_kernel, layer, n_pages, page, ts)
    seq_blk = pl.BlockSpec((1, ts, GROUP_W), lambda b, pt: (b, 0, 0))
    lsel3 = lambda b, pt: (layer, 0, 0)
    grid_spec = pltpu.PrefetchScalarGridSpec(
        num_scalar_prefetch=1,
        grid=(bs,),
        in_specs=[pl.BlockSpec(memory_space=pl.ANY), pl.BlockSpec(memory_space=pl.ANY),
                  seq_blk, seq_blk, seq_blk, seq_blk,
                  pl.BlockSpec((1, 4, A_HALF), lsel3),
                  pl.BlockSpec((1, 1, GROUP_W), lsel3),
                  pl.BlockSpec(eones.shape, lambda b, pt: (0, 0))],
        out_specs=seq_blk,
        scratch_shapes=[pltpu.VMEM((2, GROUP_W, past), F32), pltpu.VMEM((2, GROUP_W, past), F32),
                        pltpu.SemaphoreType.DMA((2, 2))],
    )
    shp3 = (bs, ts, GROUP_W)
    return pl.pallas_call(
        kernel,
        out_shape=jax.ShapeDtypeStruct(shp3, BF16),
        grid_spec=grid_spec,
        compiler_params=pltpu.CompilerParams(dimension_semantics=("arbitrary",), vmem_limit_bytes=VMEM_LIMIT),
        name=f"sample_attn_l{layer}",
    )(page_table, cache_k, cache_v,
      q_s.reshape(shp3), k_new.reshape(shp3), v_new.reshape(shp3), ga_s.reshape(shp3), lam_qk, ang, eones)


def _sample_out_kernel(depth, ts, oa_ref, obcd_ref, x_ref, mod_ref, wout_ref, lng_ref, lnb_ref, y_ref):
    nseq = ROWS // ts
    mixed = (jnp.dot(oa_ref[...], wout_ref[0, 0:GROUP_W, :], preferred_element_type=F32)
             + jnp.dot(obcd_ref[...], wout_ref[0, GROUP_W:, :], preferred_element_type=F32))
    gate = mod_ref[...][:, :, 2 * D_MODEL:]
    y = _alpha(depth) * x_ref[...] + gate * mixed.reshape(nseq, ts, D_MODEL)
    y_ref[...] = _layer_norm(y, lng_ref[0], lnb_ref[0])


def _sample_out(layer, depth, oa, obcd, x, mod_s, wout_b, lng, lnb):
    bs, ts, _ = x.shape
    nseq = ROWS // ts
    lsel3 = lambda i: (layer, 0, 0)
    return pl.pallas_call(
        functools.partial(_sample_out_kernel, depth, ts),
        out_shape=jax.ShapeDtypeStruct(x.shape, F32),
        grid=(bs // nseq,),
        in_specs=[
            pl.BlockSpec((ROWS, GROUP_W), lambda i: (i, 0)),
            pl.BlockSpec((ROWS, 3 * GROUP_W), lambda i: (i, 0)),
            pl.BlockSpec((nseq, ts, D_MODEL), lambda i: (i, 0, 0)),
            pl.BlockSpec((nseq, 1, 3 * D_MODEL), lambda i: (i, 0, 0)),
            pl.BlockSpec((1, D_MODEL, D_MODEL), lsel3),
            pl.BlockSpec((1, 1, D_MODEL), lsel3),
            pl.BlockSpec((1, 1, D_MODEL), lsel3),
        ],
        out_specs=pl.BlockSpec((nseq, ts, D_MODEL), lambda i: (i, 0, 0)),
        compiler_params=pltpu.CompilerParams(dimension_semantics=("arbitrary",), vmem_limit_bytes=VMEM_LIMIT),
        name=f"sample_out_l{layer}",
    )(oa, obcd, x, mod_s, wout_b, lng, lnb)


def _rope_tables(pos):
    half = A_HALF // 2
    inv = ROPE_THETA ** (-jnp.arange(half, dtype=F32) * 2.0 / A_HALF)
    ang = pos.astype(F32)[:, None] * inv[None, :]
    cos = jnp.cos(ang)
    sin = jnp.sin(ang)
    return jnp.tile(jnp.concatenate([cos, cos], -1), (1, 4)), jnp.tile(jnp.concatenate([-sin, sin], -1), (1, 4))


def kernel(x_prompt, x_sample, cache_k, cache_v, state_conv, state_hgrn, page_table, c_prompt, c_sample, w_ada, b_ada, w_in, lam_qk, attn_norm_g, sg_norm_g, sg_norm_b, w_s, b_s, conv_w, conv_b, conv_norm_g, conv_norm_b, w_pw, lower_bounds, hgrn_norm_g, w_out, ln_g, ln_b):
    depth = w_in.shape[0]
    bp, t, _ = x_prompt.shape
    bs, ts, _ = x_sample.shape
    n_pool, page = cache_k.shape[1], cache_k.shape[2]
    past_len = page_table.shape[1] * page
    assert ts == 8 and ROWS % ts == 0 and bs % (ROWS // ts) == 0 and t % ROWS == 0
    nseq = ROWS // ts

    gmat = _segment_matrices()
    eones = _head_ones()
    win_b = w_in.astype(BF16)
    wout_b = w_out.astype(BF16)
    wpw_b = w_pw.astype(BF16)
    row3 = lambda a: a.reshape(depth, 1, a.shape[-1])
    sgg, sgb, cb, cng, cnb = row3(sg_norm_g), row3(sg_norm_b), row3(conv_b), row3(conv_norm_g), row3(conv_norm_b)
    lng, lnb = row3(ln_g), row3(ln_b)
    ang = row3(jnp.tile(attn_norm_g, (1, A_HEADS)))
    hng = row3(jnp.tile(hgrn_norm_g, (1, D_HEADS)))
    sgbias_p = jnp.repeat(jnp.swapaxes(b_s, 1, 2), GROUP_W // 4, axis=2)
    sgbias_s = jnp.tile(sgbias_p[:, :ts], (1, nseq, 1))
    eye = jnp.eye(nseq, dtype=F32)
    ws_blk = jnp.einsum('ab,lgts->lgatbs', eye, w_s[:, :, :ts, :ts]).reshape(depth, 4, ROWS, ROWS)
    cos_p, sin_p = _rope_tables(jnp.arange(t))
    cos_s, sin_s = _rope_tables(past_len + jnp.arange(ts))
    cos_s, sin_s = jnp.tile(cos_s, (nseq, 1)), jnp.tile(sin_s, (nseq, 1))
    ck = jnp.transpose(cache_k, (0, 1, 3, 4, 2)).reshape(depth, n_pool, GROUP_W, page)
    cv = jnp.transpose(cache_v, (0, 1, 3, 4, 2)).reshape(depth, n_pool, GROUP_W, page)

    rows = bp + bs
    pad = (-rows) % 8
    c_all = jnp.concatenate([c_prompt, c_sample, jnp.zeros((pad, D_MODEL), F32)], axis=0)
    mod = _ada(c_all, w_ada, b_ada)

    xp, xs = x_prompt, x_sample
    outs = [[] for _ in range(9)]
    for l in range(depth):
        mod_p = mod[l, :bp].reshape(bp, 1, 3 * D_MODEL)
        mod_s = mod[l, bp:bp + bs].reshape(bs, 1, 3 * D_MODEL)
        krow, vrow, qt, kb, vt, ga, obcd, cst_p, hst_p = _prompt_pre(
            l, xp, mod_p, win_b, cos_p, sin_p, sgg, sgb, w_s, sgbias_p, conv_w, cb, cng, cnb, wpw_b, lower_bounds,
            hng, gmat, eones)
        xp = _prompt_attn(l, depth, qt, kb, vt, ga, obcd, xp, mod_p, wout_b, lam_qk, ang, lng, lnb)

        krow_s, vrow_s, q_s, ga_s, obcd_s, chv_s, cst_s, hst_s = _sample_pre(
            l, xs, mod_s, win_b, cos_s, sin_s, sgg, sgb, ws_blk, sgbias_s, conv_w, cb, cng, cnb, wpw_b,
            lower_bounds, hng, gmat, eones, state_conv, state_hgrn)
        oa_s = _sample_attn(l, page_table, ck, cv, q_s, krow_s, vrow_s, ga_s, lam_qk, ang, eones, ts)
        xs = _sample_out(l, depth, oa_s.reshape(bs * ts, GROUP_W), obcd_s, xs, mod_s, wout_b, lng, lnb)

        outs[0].append(jnp.transpose(krow.reshape(bp, A_HEADS, 2 * A_HALF, t), (0, 3, 1, 2)))
        outs[1].append(jnp.transpose(vrow.reshape(bp, A_HEADS, A_VDIM, t), (0, 3, 1, 2)))
        outs[2].append(krow_s.reshape(bs, ts, A_HEADS, 2 * A_HALF))
        outs[3].append(vrow_s.reshape(bs, ts, A_HEADS, A_VDIM))
        outs[4].append(chv_s.reshape(bs, ts, GROUP_W))
        outs[5].append(cst_p)
        outs[6].append(cst_s[0])
        outs[7].append(hst_p)
        outs[8].append(hst_s[0])
    return (xp, xs) + tuple(jnp.stack(o) for o in outs)
```

```python
import functools
import math

import jax
import jax.numpy as jnp
import numpy as np
from jax import lax
from jax.experimental import pallas as pl
from jax.experimental.pallas import tpu as pltpu

F32 = jnp.float32
BF16 = jnp.bfloat16

D_MODEL = 1024
GROUP_W = 256
A_HEADS = 4
A_HALF = 32
A_VDIM = 64
ROPE_THETA = 10000.0
SG_CHUNK = 128
CONV_W = 31
D_HEADS = 4
D_KDIM = 64
F_FLOOR = 1e-30
EPS = 1e-5
NEG_BIG = -1e30
D_IN = 14 * GROUP_W
SQRT_HALF = 0.7071067811865476
Q_SCALE = (A_HALF ** -0.5) * math.log2(math.e)

ROWS = 128
PRE_TB = 512
ATTN_TQ = 512
ATTN_TK = 512
LANE_TILE = 256
SCORE_LEAD = 8
VMEM_LIMIT = 56 * 1024 * 1024

G_CUM8, G_SUF8, G_LVL0, G_CUM, G_SUF = 0, 1, 2, 9, 10

_NT = (((1,), (1,)), ((), ()))
_TN = (((0,), (0,)), ((), ()))


def _segment_matrices():
    t = np.arange(ROWS)[:, None]
    s = np.arange(ROWS)[None, :]
    same8 = (t >> 3) == (s >> 3)
    mats = [same8 & (s <= t), same8 & (s > t)]
    for lvl in range(7):
        mid = ((t >> (lvl + 1)) << (lvl + 1)) + (1 << lvl)
        second = ((t >> lvl) & 1) == 1
        mats.append(np.where(second, (s >= mid) & (s <= t), (s > t) & (s < mid)))
    mats.append(s <= t)
    mats.append(s > t)
    return jnp.asarray(np.stack(mats).astype(np.float32), dtype=BF16)


def _head_ones():
    h = np.arange(GROUP_W) // 64
    return jnp.asarray((h[:, None] == h[None, :]).astype(np.float32), dtype=BF16)


def _silu(x):
    return x * jax.nn.sigmoid(x)


def _gelu(x):
    return 0.5 * x * (1.0 + lax.erf(x * SQRT_HALF))


def _layer_norm(x, g, b):
    xc = x - jnp.mean(x, axis=-1, keepdims=True)
    var = jnp.mean(xc * xc, axis=-1, keepdims=True)
    return xc * lax.rsqrt(var + EPS) * g + b


def _head_rms(x, e_ref, g):
    ms = jnp.dot((x * x).astype(BF16), e_ref[...], preferred_element_type=F32) * (1.0 / 64.0)
    return x * lax.rsqrt(ms + EPS) * g


def _rope(x, cos, sin):
    outs = []
    for half in range(2):
        xh = x[:, 128 * half:128 * half + 128]
        lane = lax.broadcasted_iota(jnp.int32, xh.shape, 1)
        partner = jnp.where((lane & 16) == 0, pltpu.roll(xh, 112, 1), pltpu.roll(xh, 16, 1))
        outs.append(xh * cos + partner * sin)
    return jnp.concatenate(outs, axis=1)


def _lower_bound(lb_ref, layer):
    lb = lb_ref[...]
    e = jnp.exp(lb - jnp.max(lb, axis=0, keepdims=True))
    soft = e / jnp.sum(e, axis=0, keepdims=True)
    acc = jnp.zeros((1, GROUP_W), F32)
    for i in range(1, layer + 1):
        acc = acc + soft[i:i + 1]
    return acc


def _lam(lam_ref):
    lp = lam_ref[0]
    a = jnp.sum(lp[0:1] * lp[1:2], axis=-1, keepdims=True)
    b = jnp.sum(lp[2:3] * lp[3:4], axis=-1, keepdims=True)
    return jnp.exp(a) - jnp.exp(b)


def _lam_init(layer):
    return 0.8 - 0.6 * math.exp(-0.3 * layer)


def _alpha(depth):
    return (2.0 * depth) ** 0.25


def _sg_mix(vn, ws_ref, bias):
    group = lax.broadcasted_iota(jnp.int32, (1, GROUP_W), 1) >> 6
    r = lax.broadcasted_iota(jnp.int32, (ROWS, ROWS), 0)
    c = lax.broadcasted_iota(jnp.int32, (ROWS, ROWS), 1)
    acc = bias
    for g in range(4):
        wm = jnp.where(r >= c, ws_ref[g], 0.0).astype(BF16)
        vm = jnp.where(group == g, vn, 0.0).astype(BF16)
        acc = acc + jnp.dot(wm, vm, preferred_element_type=F32)
    return acc


def _hgrn_gates(dq, df, di, lb):
    f = lb + (1.0 - lb) * jax.nn.sigmoid(df)
    lf = jnp.log(jnp.maximum(f, F_FLOOR))
    hi = lf.astype(BF16)
    lo = (lf - hi.astype(F32)).astype(BF16)
    return _silu(dq), 1.0 - f, di, jnp.concatenate([hi, lo], axis=1)


def _segsums(g_ref, lo, hi, hilo):
    g = g_ref[lo:hi].reshape((hi - lo) * ROWS, ROWS)
    r = jnp.dot(g, hilo, preferred_element_type=F32)
    return r[:, :GROUP_W] + r[:, GROUP_W:]


def _hgrn_intra(qd, kk, vd, seg, e_ref, levels):
    row = lax.broadcasted_iota(jnp.int32, (ROWS, 1), 0)
    head = lax.broadcasted_iota(jnp.int32, (1, GROUP_W), 1) >> 6
    t_idx = lax.broadcasted_iota(jnp.int32, (ROWS, 4 * ROWS), 0)
    s_idx = lax.broadcasted_iota(jnp.int32, (ROWS, 4 * ROWS), 1) & (ROWS - 1)
    att = jnp.zeros((ROWS, 4 * ROWS), F32)
    for i, lvl in enumerate(levels):
        e = jnp.exp(seg[i])
        second = ((row >> lvl) & 1) == 1
        a = jnp.where(second, qd * e, 0.0).astype(BF16)
        b = jnp.where(second, 0.0, kk * e)
        b_heads = jnp.concatenate([jnp.where(head == h, b, 0.0) for h in range(4)], axis=0).astype(BF16)
        s = lax.dot_general(a, b_heads, _NT, preferred_element_type=F32)
        att = att + jnp.where((t_idx >> (lvl + 1)) == (s_idx >> (lvl + 1)), s, 0.0)
    v_stack = jnp.concatenate([jnp.where(head == h, vd, 0.0) for h in range(4)], axis=0).astype(BF16)
    o = jnp.dot(att.astype(BF16), v_stack, preferred_element_type=F32)
    diag = jnp.dot((qd * kk).astype(BF16), e_ref[...], preferred_element_type=F32)
    return o + diag * vd


def _ada_kernel(c_ref, w_ref, b_ref, o_ref):
    c = c_ref[...]
    a = _silu(c).astype(BF16)
    o_ref[0] = jnp.dot(a, w_ref[0].astype(BF16), preferred_element_type=F32) + b_ref[0]


def _ada(c_all, w_ada, b_ada):
    depth = w_ada.shape[0]
    rows = c_all.shape[0]
    return pl.pallas_call(
        _ada_kernel,
        out_shape=jax.ShapeDtypeStruct((depth, rows, 3 * D_MODEL), F32),
        grid=(depth, 3),
        in_specs=[
            pl.BlockSpec((rows, D_MODEL), lambda l, n: (0, 0)),
            pl.BlockSpec((1, D_MODEL, D_MODEL), lambda l, n: (l, 0, n)),
            pl.BlockSpec((1, 1, D_MODEL), lambda l, n: (l, 0, n)),
        ],
        out_specs=pl.BlockSpec((1, rows, D_MODEL), lambda l, n: (l, 0, n)),
        compiler_params=pltpu.CompilerParams(dimension_semantics=("arbitrary", "arbitrary"),
                                             vmem_limit_bytes=VMEM_LIMIT),
        name="adaln",
    )(c_all, w_ada, b_ada.reshape(depth, 1, 3 * D_MODEL))


def _prompt_pre_kernel(layer, tb,
                       x_ref, mod_ref, win_ref, cos_ref, sin_ref, sgg_ref, sgb_ref, ws_ref, sgbias_ref,
                       cw_ref, cb_ref, cng_ref, cnb_ref, wpw_ref, lb_ref, hng_ref, g_ref, e_ref,
                       krow_ref, vrow_ref, qt_ref, kb_ref, vt_ref, ga_ref, obcd_ref, cst_ref, hst_ref,
                       hc_ref, s_ref):
    i = pl.program_id(1)

    @pl.when(i == 0)
    def _():
        hc_ref[0:32, :] = jnp.zeros((32, GROUP_W), F32)
        s_ref[...] = jnp.zeros(s_ref.shape, F32)

    mod = mod_ref[0]
    shift = mod[:, 0:D_MODEL]
    scale = mod[:, D_MODEL:2 * D_MODEL]
    h = (x_ref[0] * (1.0 + scale) + shift).astype(BF16)
    zc = jnp.dot(h, win_ref[0, :, 1792:2560], preferred_element_type=F32)
    zb = jnp.dot(h, win_ref[0, :, 1024:1792], preferred_element_type=F32)
    zd = jnp.dot(h, win_ref[0, :, 2560:3584], preferred_element_type=F32)

    glu = zc[:, 0:GROUP_W] * jax.nn.sigmoid(zc[:, GROUP_W:2 * GROUP_W])
    hc_ref[32:32 + tb, :] = glu
    cst_ref[0] = hc_ref[pl.ds(tb + 2, CONV_W - 1), :]

    lb = _lower_bound(lb_ref, layer)
    o_b_parts, o_c_parts, o_d_parts = [], [], []
    for c in range(tb // ROWS):
        rs = slice(c * ROWS, (c + 1) * ROWS)
        y = jnp.zeros((ROWS, GROUP_W), F32)
        for j in range(CONV_W):
            y = y + hc_ref[pl.ds(c * ROWS + 2 + j, ROWS), :] * cw_ref[0, j:j + 1, :]
        yn = _silu(_layer_norm(y + cb_ref[0], cng_ref[0], cnb_ref[0]))
        o_c_parts.append(jnp.dot(yn.astype(BF16), wpw_ref[0], preferred_element_type=F32) * _silu(zc[rs, 512:768]))
        vn = _layer_norm(_gelu(zb[rs, 256:512]), sgg_ref[0], sgb_ref[0])
        mixed = _sg_mix(vn, ws_ref.at[0], sgbias_ref[0])
        o_b_parts.append(_gelu(zb[rs, 0:256]) * mixed * _silu(zb[rs, 512:768]))
        qd, kk, vd, hilo = _hgrn_gates(zd[rs, 0:256], zd[rs, 256:512], zd[rs, 512:768], lb)
        seg = _segsums(g_ref, G_LVL0, G_SUF + 1, hilo)
        o = _hgrn_intra(qd, kk, vd, [seg[ROWS * n:ROWS * (n + 1)] for n in range(7)], e_ref, range(7))
        aq = qd * jnp.exp(seg[7 * ROWS:8 * ROWS])
        bk = kk * jnp.exp(seg[8 * ROWS:9 * ROWS])
        ones = jnp.ones((ROWS, D_KDIM), BF16)
        inter = []
        for hh in range(D_HEADS):
            lo_, hi_ = 64 * hh, 64 * hh + 64
            st = s_ref[hh]
            inter.append(jnp.dot(aq[:, lo_:hi_].astype(BF16), st.astype(BF16), preferred_element_type=F32))
            dec = (lax.dot_general(hilo[:, lo_:hi_], ones, _TN, preferred_element_type=F32)
                   + lax.dot_general(hilo[:, GROUP_W + lo_:GROUP_W + hi_], ones, _TN, preferred_element_type=F32))
            upd = lax.dot_general(bk[:, lo_:hi_].astype(BF16), vd[:, lo_:hi_].astype(BF16), _TN,
                                  preferred_element_type=F32)
            s_ref[hh] = jnp.exp(dec) * st + upd
        o = o + jnp.concatenate(inter, axis=1)
        o_d_parts.append(_head_rms(o, e_ref, hng_ref[0]) * _silu(zd[rs, 768:1024]))
    za = jnp.dot(h, win_ref[0, :, 0:1024], preferred_element_type=F32)
    cos = cos_ref[...]
    sin = sin_ref[...]
    q = _rope(za[:, 0:256], cos, sin)
    k = _rope(za[:, 256:512], cos, sin)
    v = za[:, 512:768]
    kt = k.T
    vt = v.T
    krow_ref[0] = kt
    vrow_ref[0] = vt
    kb_ref[0] = k.astype(BF16)
    qt_ref[0] = (q * Q_SCALE).T.astype(BF16)
    vt_ref[0] = vt.astype(BF16)
    ga_ref[0] = _silu(za[:, 768:1024])

    hc_ref[0:32, :] = hc_ref[pl.ds(tb, 32), :]
    o_b = jnp.concatenate(o_b_parts, axis=0)
    o_c = jnp.concatenate(o_c_parts, axis=0)
    o_d = jnp.concatenate(o_d_parts, axis=0)
    obcd_ref[0] = jnp.concatenate([o_b, o_c, o_d], axis=1).astype(BF16)
    hst_ref[0] = s_ref[...]


def _prompt_pre(layer, x, mod_p, win_b, cos, sin, sgg, sgb, ws, sgbias, cw, cb, cng, cnb, wpw_b, lbnd, hng,
                gmat, eones):
    b, t, _ = x.shape
    tb = min(PRE_TB, t)
    kernel = functools.partial(_prompt_pre_kernel, layer, tb)
    lsel3 = lambda bb, i: (layer, 0, 0)
    const2 = lambda bb, i: (0, 0)
    const3 = lambda bb, i: (0, 0, 0)
    row_blk = lambda w: pl.BlockSpec((1, tb, w), lambda bb, i: (bb, i, 0))
    col_blk = pl.BlockSpec((1, GROUP_W, tb), lambda bb, i: (bb, 0, i))
    out_shape = (
        jax.ShapeDtypeStruct((b, GROUP_W, t), F32),
        jax.ShapeDtypeStruct((b, GROUP_W, t), F32),
        jax.ShapeDtypeStruct((b, GROUP_W, t), BF16),
        jax.ShapeDtypeStruct((b, t, GROUP_W), BF16),
        jax.ShapeDtypeStruct((b, GROUP_W, t), BF16),
        jax.ShapeDtypeStruct((b, t, GROUP_W), F32),
        jax.ShapeDtypeStruct((b, t, 3 * GROUP_W), BF16),
        jax.ShapeDtypeStruct((b, CONV_W - 1, GROUP_W), F32),
        jax.ShapeDtypeStruct((b, D_HEADS, D_KDIM, D_KDIM), F32),
    )
    return pl.pallas_call(
        kernel,
        out_shape=out_shape,
        grid=(b, t // tb),
        in_specs=[
            row_blk(D_MODEL),
            pl.BlockSpec((1, 1, 3 * D_MODEL), lambda bb, i: (bb, 0, 0)),
            pl.BlockSpec((1, D_MODEL, D_IN), lsel3),
            pl.BlockSpec((tb, 128), lambda bb, i: (i, 0)),
            pl.BlockSpec((tb, 128), lambda bb, i: (i, 0)),
            pl.BlockSpec((1, 1, GROUP_W), lsel3),
            pl.BlockSpec((1, 1, GROUP_W), lsel3),
            pl.BlockSpec((1, 4, ROWS, ROWS), lambda bb, i: (layer, 0, 0, 0)),
            pl.BlockSpec((1, ROWS, GROUP_W), lsel3),
            pl.BlockSpec((1, CONV_W, GROUP_W), lsel3),
            pl.BlockSpec((1, 1, GROUP_W), lsel3),
            pl.BlockSpec((1, 1, GROUP_W), lsel3),
            pl.BlockSpec((1, 1, GROUP_W), lsel3),
            pl.BlockSpec((1, GROUP_W, GROUP_W), lsel3),
            pl.BlockSpec(lbnd.shape, const2),
            pl.BlockSpec((1, 1, GROUP_W), lsel3),
            pl.BlockSpec(gmat.shape, const3),
            pl.BlockSpec(eones.shape, const2),
        ],
        out_specs=(
            col_blk, col_blk, col_blk, row_blk(GROUP_W), col_blk, row_blk(GROUP_W),
            row_blk(3 * GROUP_W),
            pl.BlockSpec((1, CONV_W - 1, GROUP_W), lambda bb, i: (bb, 0, 0)),
            pl.BlockSpec((1, D_HEADS, D_KDIM, D_KDIM), lambda bb, i: (bb, 0, 0, 0)),
        ),
        scratch_shapes=[pltpu.VMEM((32 + tb, GROUP_W), F32), pltpu.VMEM((D_HEADS, D_KDIM, D_KDIM), F32)],
        compiler_params=pltpu.CompilerParams(dimension_semantics=("arbitrary", "arbitrary"),
                                             vmem_limit_bytes=VMEM_LIMIT),
        name=f"prompt_pre_l{layer}",
    )(x, mod_p, win_b, cos, sin, sgg, sgb, ws, sgbias, cw, cb, cng, cnb, wpw_b, lbnd, hng, gmat, eones)


def _prompt_attn_kernel(layer, depth, tq, tk,
                        qt_ref, kb_ref, vt_ref, ga_ref, obcd_ref, x_ref, mod_ref, wout_ref, lam_ref, ang_ref,
                        lng_ref, lnb_ref, y_ref, acc_ref, m_ref, qm_ref, s_ref):
    i = pl.program_id(1)
    qt = qt_ref[0]
    rowg = lax.broadcasted_iota(jnp.int32, (GROUP_W, 1), 0) >> 5
    for j in range(8):
        qm_ref[j] = jnp.where(rowg == j, qt, jnp.zeros_like(qt))
    m_ref[...] = jnp.full(m_ref.shape, NEG_BIG, F32)
    acc_ref[...] = jnp.zeros(acc_ref.shape, F32)
    ones = jnp.ones((16, tk), BF16)
    kpq = tq // tk

    def tile(kt, diag):
        masked = diag is not None
        off = pl.multiple_of(kt * tk, tk)
        kk = kb_ref[0, pl.ds(off, tk), :]
        if masked:
            key = lax.broadcasted_iota(jnp.int32, (tk, tq), 0) + diag * tk
            qry = lax.broadcasted_iota(jnp.int32, (tk, tq), 1)
            visible = key <= qry
        vvs = [jnp.concatenate([vt_ref[0, 64 * hh:64 * hh + 64, pl.ds(off, tk)], ones], axis=0)
               for hh in range(A_HEADS)]

        def scores(j):
            s = jnp.dot(kk, qm_ref[j], preferred_element_type=F32)
            if masked:
                s = jnp.where(visible, s, NEG_BIG)
            s_ref[j] = s
            return jnp.max(s, axis=0, keepdims=True)

        def update(j, cmax):
            m_old = m_ref[j]
            m_new = jnp.maximum(m_old, cmax)
            alpha = jnp.exp2(m_old - m_new)
            m_ref[j] = m_new
            for c in range(tq // LANE_TILE):
                cols = slice(LANE_TILE * c, LANE_TILE * (c + 1))
                p = jnp.exp2(s_ref[j, :, cols] - m_new[:, cols]).astype(BF16)
                acc_ref[j, :, cols] = (alpha[:, cols] * acc_ref[j, :, cols]
                                       + jnp.dot(vvs[j // 2], p, preferred_element_type=F32))

        cmax = {}
        for j in range(8 + SCORE_LEAD):
            if j < 8:
                cmax[j] = scores(j)
            if j >= SCORE_LEAD:
                update(j - SCORE_LEAD, cmax[j - SCORE_LEAD])

    def body(kt, carry):
        tile(kt, None)
        return carry

    lax.fori_loop(0, i * kpq, body, 0)
    for dg in range(kpq):
        tile(i * kpq + dg, dg)

    lam = _lam(lam_ref) + _lam_init(layer)
    heads = []
    for hh in range(A_HEADS):
        a0 = acc_ref[2 * hh]
        a1 = acc_ref[2 * hh + 1]
        d = a0[0:A_VDIM] / a0[A_VDIM:A_VDIM + 1] - lam * (a1[0:A_VDIM] / a1[A_VDIM:A_VDIM + 1])
        ms = jnp.mean(d * d, axis=0, keepdims=True)
        heads.append(d * lax.rsqrt(ms + EPS))
    o_a = jnp.concatenate(heads, axis=0).T
    o_a = o_a * (ang_ref[0] * (1.0 - _lam_init(layer))) * ga_ref[0]
    mixed = (jnp.dot(o_a.astype(BF16), wout_ref[0, 0:GROUP_W, :], preferred_element_type=F32)
             + jnp.dot(obcd_ref[0], wout_ref[0, GROUP_W:, :], preferred_element_type=F32))
    gate = mod_ref[0][:, 2 * D_MODEL:]
    y_ref[0] = _layer_norm(_alpha(depth) * x_ref[0] + gate * mixed, lng_ref[0], lnb_ref[0])


def _prompt_attn(layer, depth, qt, kb, vt, ga, obcd, x, mod_p, wout_b, lam_qk, ang, lng, lnb):
    b, t, _ = x.shape
    tq = min(ATTN_TQ, t)
    tk = min(ATTN_TK, tq)
    kernel = functools.partial(_prompt_attn_kernel, layer, depth, tq, tk)
    lsel3 = lambda bb, i: (layer, 0, 0)
    row_blk = lambda w: pl.BlockSpec((1, tq, w), lambda bb, i: (bb, i, 0))
    return pl.pallas_call(
        kernel,
        out_shape=jax.ShapeDtypeStruct((b, t, D_MODEL), F32),
        grid=(b, t // tq),
        in_specs=[
            pl.BlockSpec((1, GROUP_W, tq), lambda bb, i: (bb, 0, i)),
            pl.BlockSpec((1, t, GROUP_W), lambda bb, i: (bb, 0, 0)),
            pl.BlockSpec((1, GROUP_W, t), lambda bb, i: (bb, 0, 0)),
            row_blk(GROUP_W),
            row_blk(3 * GROUP_W),
            row_blk(D_MODEL),
            pl.BlockSpec((1, 1, 3 * D_MODEL), lambda bb, i: (bb, 0, 0)),
            pl.BlockSpec((1, D_MODEL, D_MODEL), lsel3),
            pl.BlockSpec((1, 4, A_HALF), lsel3),
            pl.BlockSpec((1, 1, GROUP_W), lsel3),
            pl.BlockSpec((1, 1, D_MODEL), lsel3),
            pl.BlockSpec((1, 1, D_MODEL), lsel3),
        ],
        out_specs=row_blk(D_MODEL),
        scratch_shapes=[pltpu.VMEM((8, A_VDIM + 16, tq), F32), pltpu.VMEM((8, 1, tq), F32),
                        pltpu.VMEM((8, GROUP_W, tq), BF16), pltpu.VMEM((8, tk, tq), F32)],
        compiler_params=pltpu.CompilerParams(dimension_semantics=("arbitrary", "arbitrary"),
                                             vmem_limit_bytes=VMEM_LIMIT),
        name=f"prompt_attn_l{layer}",
    )(qt, kb, vt, ga, obcd, x, mod_p, wout_b, lam_qk, ang, lng, lnb)


def _sample_pre_kernel(layer, ts,
                       x_ref, mod_ref, win_ref, cos_ref, sin_ref, sgg_ref, sgb_ref, ws_ref, sgbias_ref,
                       cw_ref, cb_ref, cng_ref, cnb_ref, wpw_ref, lb_ref, hng_ref, g_ref, e_ref, cst_ref, hst_ref,
                       krow_ref, vrow_ref, q_ref, ga_ref, obcd_ref, chv_ref, ncst_ref, nhst_ref,
                       hc_ref):
    nseq = ROWS // ts
    mod = mod_ref[...]
    shift = mod[:, :, 0:D_MODEL]
    scale = mod[:, :, D_MODEL:2 * D_MODEL]
    h = (x_ref[...] * (1.0 + scale) + shift).reshape(ROWS, D_MODEL).astype(BF16)
    z = jnp.dot(h, win_ref[0], preferred_element_type=F32)

    cos = cos_ref[...]
    sin = sin_ref[...]
    k = _rope(z[:, 256:512], cos, sin)
    krow_ref[...] = k
    vrow_ref[...] = z[:, 512:768]
    q_ref[...] = _rope(z[:, 0:256], cos, sin) * Q_SCALE
    ga_ref[...] = _silu(z[:, 768:1024])

    vn = _layer_norm(_gelu(z[:, 1280:1536]), sgg_ref[0], sgb_ref[0])
    chv_ref[...] = vn
    o_b = _gelu(z[:, 1024:1280]) * _sg_mix(vn, ws_ref.at[0], sgbias_ref[0]) * _silu(z[:, 1536:1792])

    a = z[:, 1792:2304]
    glu = a[:, :GROUP_W] * jax.nn.sigmoid(a[:, GROUP_W:])
    hc_ref[:, 0:CONV_W - 1, :] = cst_ref[0]
    hc_ref[:, CONV_W - 1:CONV_W - 1 + ts, :] = glu.reshape(nseq, ts, GROUP_W)
    y = jnp.zeros((nseq, ts, GROUP_W), F32)
    for j in range(CONV_W):
        y = y + hc_ref[:, j:j + ts, :] * cw_ref[0, j:j + 1, :]
    ncst_ref[0] = hc_ref[:, ts:ts + CONV_W - 1, :]
    yn = _silu(_layer_norm(y.reshape(ROWS, GROUP_W) + cb_ref[0], cng_ref[0], cnb_ref[0]))
    o_c = jnp.dot(yn.astype(BF16), wpw_ref[0], preferred_element_type=F32) * _silu(z[:, 2304:2560])

    lb = _lower_bound(lb_ref, layer)
    qd, kk, vd, hilo = _hgrn_gates(z[:, 2560:2816], z[:, 2816:3072], z[:, 3072:3328], lb)
    seg = _segsums(g_ref, G_CUM8, G_LVL0 + 3, hilo)
    o = _hgrn_intra(qd, kk, vd, [seg[ROWS * (2 + n):ROWS * (3 + n)] for n in range(3)], e_ref, range(3))
    aq = (qd * jnp.exp(seg[0:ROWS])).reshape(nseq, ts, GROUP_W)
    bk = (kk * jnp.exp(seg[ROWS:2 * ROWS])).reshape(nseq, ts, GROUP_W)
    v3 = vd.reshape(nseq, ts, GROUP_W)
    hilo3 = hilo.astype(F32).reshape(nseq, ts, 2 * GROUP_W)
    ones = jnp.ones((nseq, ts, D_KDIM), BF16)
    inter = []
    for hh in range(D_HEADS):
        lo_, hi_ = 64 * hh, 64 * hh + 64
        st = hst_ref[0, :, hh]
        inter.append(jnp.einsum('bqk,bkv->bqv', aq[:, :, lo_:hi_].astype(BF16), st.astype(BF16),
                                preferred_element_type=F32))
        dec = (jnp.einsum('bsk,bsv->bkv', hilo3[:, :, lo_:hi_].astype(BF16), ones, preferred_element_type=F32)
               + jnp.einsum('bsk,bsv->bkv', hilo3[:, :, GROUP_W + lo_:GROUP_W + hi_].astype(BF16), ones,
                            preferred_element_type=F32))
        upd = jnp.einsum('bsk,bsv->bkv', bk[:, :, lo_:hi_].astype(BF16), v3[:, :, lo_:hi_].astype(BF16),
                         preferred_element_type=F32)
        nhst_ref[0, :, hh] = jnp.exp(dec) * st + upd
    o = o + jnp.concatenate(inter, axis=2).reshape(ROWS, GROUP_W)
    o_d = _head_rms(o, e_ref, hng_ref[0]) * _silu(z[:, 3328:3584])
    obcd_ref[...] = jnp.concatenate([o_b, o_c, o_d], axis=1).astype(BF16)


def _sample_pre(layer, x, mod_s, win_b, cos, sin, sgg, sgb, ws_blk, sgbias, cw, cb, cng, cnb, wpw_b, lbnd, hng,
                gmat, eones, state_conv, state_hgrn):
    bs, ts, _ = x.shape
    nseq = ROWS // ts
    nblk = bs // nseq
    n = bs * ts
    kernel = functools.partial(_sample_pre_kernel, layer, ts)
    lsel3 = lambda i: (layer, 0, 0)
    const2 = lambda i: (0, 0)
    const3 = lambda i: (0, 0, 0)
    row_blk = lambda w: pl.BlockSpec((ROWS, w), lambda i: (i, 0))
    out_shape = (
        jax.ShapeDtypeStruct((n, GROUP_W), F32),
        jax.ShapeDtypeStruct((n, GROUP_W), F32),
        jax.ShapeDtypeStruct((n, GROUP_W), F32),
        jax.ShapeDtypeStruct((n, GROUP_W), F32),
        jax.ShapeDtypeStruct((n, 3 * GROUP_W), BF16),
        jax.ShapeDtypeStruct((n, GROUP_W), F32),
        jax.ShapeDtypeStruct((1, bs, CONV_W - 1, GROUP_W), F32),
        jax.ShapeDtypeStruct((1, bs, D_HEADS, D_KDIM, D_KDIM), F32),
    )
    return pl.pallas_call(
        kernel,
        out_shape=out_shape,
        grid=(nblk,),
        in_specs=[
            pl.BlockSpec((nseq, ts, D_MODEL), lambda i: (i, 0, 0)),
            pl.BlockSpec((nseq, 1, 3 * D_MODEL), lambda i: (i, 0, 0)),
            pl.BlockSpec((1, D_MODEL, D_IN), lsel3),
            pl.BlockSpec((ROWS, 128), const2),
            pl.BlockSpec((ROWS, 128), const2),
            pl.BlockSpec((1, 1, GROUP_W), lsel3),
            pl.BlockSpec((1, 1, GROUP_W), lsel3),
            pl.BlockSpec((1, 4, ROWS, ROWS), lambda i: (layer, 0, 0, 0)),
            pl.BlockSpec((1, ROWS, GROUP_W), lsel3),
            pl.BlockSpec((1, CONV_W, GROUP_W), lsel3),
            pl.BlockSpec((1, 1, GROUP_W), lsel3),
            pl.BlockSpec((1, 1, GROUP_W), lsel3),
            pl.BlockSpec((1, 1, GROUP_W), lsel3),
            pl.BlockSpec((1, GROUP_W, GROUP_W), lsel3),
            pl.BlockSpec(lbnd.shape, const2),
            pl.BlockSpec((1, 1, GROUP_W), lsel3),
            pl.BlockSpec(gmat.shape, const3),
            pl.BlockSpec(eones.shape, const2),
            pl.BlockSpec((1, nseq, CONV_W - 1, GROUP_W), lambda i: (layer, i, 0, 0)),
            pl.BlockSpec((1, nseq, D_HEADS, D_KDIM, D_KDIM), lambda i: (layer, i, 0, 0, 0)),
        ],
        out_specs=(
            row_blk(GROUP_W), row_blk(GROUP_W), row_blk(GROUP_W), row_blk(GROUP_W), row_blk(3 * GROUP_W),
            row_blk(GROUP_W),
            pl.BlockSpec((1, nseq, CONV_W - 1, GROUP_W), lambda i: (0, i, 0, 0)),
            pl.BlockSpec((1, nseq, D_HEADS, D_KDIM, D_KDIM), lambda i: (0, i, 0, 0, 0)),
        ),
        scratch_shapes=[pltpu.VMEM((nseq, 40, GROUP_W), F32)],
        compiler_params=pltpu.CompilerParams(dimension_semantics=("arbitrary",), vmem_limit_bytes=VMEM_LIMIT),
        name=f"sample_pre_l{layer}",
    )(x, mod_s, win_b, cos, sin, sgg, sgb, ws_blk, sgbias, cw, cb, cng, cnb, wpw_b, lbnd, hng, gmat, eones,
      state_conv, state_hgrn)


def _sample_attn_kernel(layer, n_pages, page, ts,
                        pt_ref, ck_hbm, cv_hbm, q_ref, kn_ref, vn_ref, ga_ref, lam_ref, ang_ref, e_ref, o_ref,
                        kbuf, vbuf, sem):
    b = pl.program_id(0)
    nb = pl.num_programs(0)

    def page_copies(seq, slot):
        copies = []
        for j in range(n_pages):
            pid = pt_ref[seq, j]
            copies.append(pltpu.make_async_copy(ck_hbm.at[layer, pid], kbuf.at[slot, j], sem.at[0, slot]))
            copies.append(pltpu.make_async_copy(cv_hbm.at[layer, pid], vbuf.at[slot, j], sem.at[1, slot]))
        return copies

    def start_all(copies):
        for n, cp in enumerate(copies):
            cp.start(priority=n % 2)

    @pl.when(b == 0)
    def _():
        start_all(page_copies(0, 0))

    @pl.when(b + 1 < nb)
    def _():
        start_all(page_copies(b + 1, (b + 1) % 2))

    slot = b % 2
    for cp in page_copies(b, slot):
        cp.wait()

    grp = lax.broadcasted_iota(jnp.int32, (1, GROUP_W), 1) >> 5
    head = lax.broadcasted_iota(jnp.int32, (1, GROUP_W), 1) >> 6
    t_q = lax.broadcasted_iota(jnp.int32, (8 * ts, ts), 0) & (ts - 1)
    t_k = lax.broadcasted_iota(jnp.int32, (8 * ts, ts), 1)
    vis = t_k <= t_q
    lam = _lam(lam_ref) + _lam_init(layer)
    q = q_ref[b]
    qexp = jnp.concatenate([jnp.where(grp == j, q, 0.0) for j in range(8)], axis=0).astype(BF16)
    kt_all = jnp.concatenate([kbuf[slot, j] for j in range(n_pages)], axis=1).astype(BF16)
    vt_all = jnp.concatenate([vbuf[slot, j] for j in range(n_pages)], axis=1).astype(BF16)
    s_past = jnp.dot(qexp, kt_all, preferred_element_type=F32)
    s_new = lax.dot_general(qexp, kn_ref[b].astype(BF16), _NT, preferred_element_type=F32)
    s_new = jnp.where(vis, s_new, NEG_BIG)
    m = jnp.maximum(jnp.max(s_past, axis=-1, keepdims=True), jnp.max(s_new, axis=-1, keepdims=True))
    p_past = jnp.exp2(s_past - m)
    p_new = jnp.where(vis, jnp.exp2(s_new - m), 0.0)
    l = jnp.sum(p_past, axis=-1, keepdims=True) + jnp.sum(p_new, axis=-1, keepdims=True)
    o = (lax.dot_general(p_past.astype(BF16), vt_all, _NT, preferred_element_type=F32)
         + jnp.dot(p_new.astype(BF16), vn_ref[b].astype(BF16), preferred_element_type=F32)) / l
    o_a = jnp.zeros((ts, GROUP_W), F32)
    for hh in range(A_HEADS):
        d = o[2 * hh * ts:(2 * hh + 1) * ts] - lam * o[(2 * hh + 1) * ts:(2 * hh + 2) * ts]
        o_a = o_a + jnp.where(head == hh, d, 0.0)
    o_a = _head_rms(o_a, e_ref, ang_ref[0] * (1.0 - _lam_init(layer))) * ga_ref[b]
    o_ref[b] = o_a.astype(BF16)


def _sample_attn(layer, page_table, cache_k, cache_v, q_s, k_new, v_new, ga_s, lam_qk, ang, eones, ts):
    bs, n_pages = page_table.shape
    page = cache_k.shape[3]
    kernel = functools.partial(_sample_attn_kernel, layer, n_pages, page, ts)
    seq_blk = pl.BlockSpec((bs, ts, GROUP_W), lambda b, pt: (0, 0, 0))
    lsel3 = lambda b, pt: (layer, 0, 0)
    grid_spec = pltpu.PrefetchScalarGridSpec(
        num_scalar_prefetch=1,
        grid=(bs,),
        in_specs=[pl.BlockSpec(memory_space=pl.ANY), pl.BlockSpec(memory_space=pl.ANY),
                  seq_blk, seq_blk, seq_blk, seq_blk,
                  pl.BlockSpec((1, 4, A_HALF), lsel3),
                  pl.BlockSpec((1, 1, GROUP_W), lsel3),
                  pl.BlockSpec(eones.shape, lambda b, pt: (0, 0))],
        out_specs=seq_blk,
        scratch_shapes=[pltpu.VMEM((2, n_pages, GROUP_W, page), F32), pltpu.VMEM((2, n_pages, GROUP_W, page), F32),
                        pltpu.SemaphoreType.DMA((2, 2))],
    )
    shp3 = (bs, ts, GROUP_W)
    return pl.pallas_call(
        kernel,
        out_shape=jax.ShapeDtypeStruct(shp3, BF16),
        grid_spec=grid_spec,
        compiler_params=pltpu.CompilerParams(dimension_semantics=("arbitrary",), vmem_limit_bytes=VMEM_LIMIT),
        name=f"sample_attn_l{layer}",
    )(page_table, cache_k, cache_v,
      q_s.reshape(shp3), k_new.reshape(shp3), v_new.reshape(shp3), ga_s.reshape(shp3), lam_qk, ang, eones)


def _sample_out_kernel(depth, ts, oa_ref, obcd_ref, x_ref, mod_ref, wout_ref, lng_ref, lnb_ref, y_ref):
    nseq = ROWS // ts
    mixed = (jnp.dot(oa_ref[...], wout_ref[0, 0:GROUP_W, :], preferred_element_type=F32)
             + jnp.dot(obcd_ref[...], wout_ref[0, GROUP_W:, :], preferred_element_type=F32))
    gate = mod_ref[...][:, :, 2 * D_MODEL:]
    y = _alpha(depth) * x_ref[...] + gate * mixed.reshape(nseq, ts, D_MODEL)
    y_ref[...] = _layer_norm(y, lng_ref[0], lnb_ref[0])


def _sample_out(layer, depth, oa, obcd, x, mod_s, wout_b, lng, lnb):
    bs, ts, _ = x.shape
    nseq = ROWS // ts
    lsel3 = lambda i: (layer, 0, 0)
    return pl.pallas_call(
        functools.partial(_sample_out_kernel, depth, ts),
        out_shape=jax.ShapeDtypeStruct(x.shape, F32),
        grid=(bs // nseq,),
        in_specs=[
            pl.BlockSpec((ROWS, GROUP_W), lambda i: (i, 0)),
            pl.BlockSpec((ROWS, 3 * GROUP_W), lambda i: (i, 0)),
            pl.BlockSpec((nseq, ts, D_MODEL), lambda i: (i, 0, 0)),
            pl.BlockSpec((nseq, 1, 3 * D_MODEL), lambda i: (i, 0, 0)),
            pl.BlockSpec((1, D_MODEL, D_MODEL), lsel3),
            pl.BlockSpec((1, 1, D_MODEL), lsel3),
            pl.BlockSpec((1, 1, D_MODEL), lsel3),
        ],
        out_specs=pl.BlockSpec((nseq, ts, D_MODEL), lambda i: (i, 0, 0)),
        compiler_params=pltpu.CompilerParams(dimension_semantics=("arbitrary",), vmem_limit_bytes=VMEM_LIMIT),
        name=f"sample_out_l{layer}",
    )(oa, obcd, x, mod_s, wout_b, lng, lnb)


def _rope_tables(pos):
    half = A_HALF // 2
    inv = ROPE_THETA ** (-jnp.arange(half, dtype=F32) * 2.0 / A_HALF)
    ang = pos.astype(F32)[:, None] * inv[None, :]
    cos = jnp.cos(ang)
    sin = jnp.sin(ang)
    return jnp.tile(jnp.concatenate([cos, cos], -1), (1, 4)), jnp.tile(jnp.concatenate([-sin, sin], -1), (1, 4))


def kernel(x_prompt, x_sample, cache_k, cache_v, state_conv, state_hgrn, page_table, c_prompt, c_sample, w_ada, b_ada, w_in, lam_qk, attn_norm_g, sg_norm_g, sg_norm_b, w_s, b_s, conv_w, conv_b, conv_norm_g, conv_norm_b, w_pw, lower_bounds, hgrn_norm_g, w_out, ln_g, ln_b):
    depth = w_in.shape[0]
    bp, t, _ = x_prompt.shape
    bs, ts, _ = x_sample.shape
    n_pool, page = cache_k.shape[1], cache_k.shape[2]
    past_len = page_table.shape[1] * page
    assert ts == 8 and ROWS % ts == 0 and bs % (ROWS // ts) == 0 and t % ROWS == 0
    nseq = ROWS // ts

    gmat = _segment_matrices()
    eones = _head_ones()
    win_b = w_in.astype(BF16)
    wout_b = w_out.astype(BF16)
    wpw_b = w_pw.astype(BF16)
    row3 = lambda a: a.reshape(depth, 1, a.shape[-1])
    sgg, sgb, cb, cng, cnb = row3(sg_norm_g), row3(sg_norm_b), row3(conv_b), row3(conv_norm_g), row3(conv_norm_b)
    lng, lnb = row3(ln_g), row3(ln_b)
    ang = row3(jnp.tile(attn_norm_g, (1, A_HEADS)))
    hng = row3(jnp.tile(hgrn_norm_g, (1, D_HEADS)))
    sgbias_p = jnp.repeat(jnp.swapaxes(b_s, 1, 2), GROUP_W // 4, axis=2)
    sgbias_s = jnp.tile(sgbias_p[:, :ts], (1, nseq, 1))
    eye = jnp.eye(nseq, dtype=F32)
    ws_blk = jnp.einsum('ab,lgts->lgatbs', eye, w_s[:, :, :ts, :ts]).reshape(depth, 4, ROWS, ROWS)
    cos_p, sin_p = _rope_tables(jnp.arange(t))
    cos_s, sin_s = _rope_tables(past_len + jnp.arange(ts))
    cos_s, sin_s = jnp.tile(cos_s, (nseq, 1)), jnp.tile(sin_s, (nseq, 1))
    ck = jnp.transpose(cache_k, (0, 1, 3, 4, 2)).reshape(depth, n_pool, GROUP_W, page)
    cv = jnp.transpose(cache_v, (0, 1, 3, 4, 2)).reshape(depth, n_pool, GROUP_W, page)

    rows = bp + bs
    pad = (-rows) % 8
    c_all = jnp.concatenate([c_prompt, c_sample, jnp.zeros((pad, D_MODEL), F32)], axis=0)
    mod = _ada(c_all, w_ada, b_ada)

    xp, xs = x_prompt, x_sample
    outs = [[] for _ in range(9)]
    for l in range(depth):
        mod_p = mod[l, :bp].reshape(bp, 1, 3 * D_MODEL)
        mod_s = mod[l, bp:bp + bs].reshape(bs, 1, 3 * D_MODEL)
        krow, vrow, qt, kb, vt, ga, obcd, cst_p, hst_p = _prompt_pre(
            l, xp, mod_p, win_b, cos_p, sin_p, sgg, sgb, w_s, sgbias_p, conv_w, cb, cng, cnb, wpw_b, lower_bounds,
            hng, gmat, eones)
        xp = _prompt_attn(l, depth, qt, kb, vt, ga, obcd, xp, mod_p, wout_b, lam_qk, ang, lng, lnb)

        krow_s, vrow_s, q_s, ga_s, obcd_s, chv_s, cst_s, hst_s = _sample_pre(
            l, xs, mod_s, win_b, cos_s, sin_s, sgg, sgb, ws_blk, sgbias_s, conv_w, cb, cng, cnb, wpw_b,
            lower_bounds, hng, gmat, eones, state_conv, state_hgrn)
        oa_s = _sample_attn(l, page_table, ck, cv, q_s, krow_s, vrow_s, ga_s, lam_qk, ang, eones, ts)
        xs = _sample_out(l, depth, oa_s.reshape(bs * ts, GROUP_W), obcd_s, xs, mod_s, wout_b, lng, lnb)

        outs[0].append(jnp.transpose(krow.reshape(bp, A_HEADS, 2 * A_HALF, t), (0, 3, 1, 2)))
        outs[1].append(jnp.transpose(vrow.reshape(bp, A_HEADS, A_VDIM, t), (0, 3, 1, 2)))
        outs[2].append(krow_s.reshape(bs, ts, A_HEADS, 2 * A_HALF))
        outs[3].append(vrow_s.reshape(bs, ts, A_HEADS, A_VDIM))
        outs[4].append(chv_s.reshape(bs, ts, GROUP_W))
        outs[5].append(cst_p)
        outs[6].append(cst_s[0])
        outs[7].append(hst_p)
        outs[8].append(hst_s[0])
    return (xp, xs) + tuple(jnp.stack(o) for o in outs)
```

```python
import functools
import math

import jax
import jax.numpy as jnp
import numpy as np
from jax import lax
from jax.experimental import pallas as pl
from jax.experimental.pallas import tpu as pltpu

F32 = jnp.float32
BF16 = jnp.bfloat16

D_MODEL = 1024
GROUP_W = 256
A_HEADS = 4
A_HALF = 32
A_VDIM = 64
ROPE_THETA = 10000.0
SG_CHUNK = 128
CONV_W = 31
D_HEADS = 4
D_KDIM = 64
F_FLOOR = 1e-30
EPS = 1e-5
NEG_BIG = -1e30
D_IN = 14 * GROUP_W
SQRT_HALF = 0.7071067811865476
Q_SCALE = (A_HALF ** -0.5) * math.log2(math.e)

ROWS = 128
PRE_TB = 512
ATTN_TQ = 512
ATTN_TK = 512
PAGE_SLOTS = 3
LANE_TILE = 256
SCORE_LEAD = 8
VMEM_LIMIT = 56 * 1024 * 1024

G_CUM8, G_SUF8, G_LVL0, G_CUM, G_SUF = 0, 1, 2, 9, 10

_NT = (((1,), (1,)), ((), ()))
_TN = (((0,), (0,)), ((), ()))


def _segment_matrices():
    t = np.arange(ROWS)[:, None]
    s = np.arange(ROWS)[None, :]
    same8 = (t >> 3) == (s >> 3)
    mats = [same8 & (s <= t), same8 & (s > t)]
    for lvl in range(7):
        mid = ((t >> (lvl + 1)) << (lvl + 1)) + (1 << lvl)
        second = ((t >> lvl) & 1) == 1
        mats.append(np.where(second, (s >= mid) & (s <= t), (s > t) & (s < mid)))
    mats.append(s <= t)
    mats.append(s > t)
    return jnp.asarray(np.stack(mats).astype(np.float32), dtype=BF16)


def _head_ones():
    h = np.arange(GROUP_W) // 64
    return jnp.asarray((h[:, None] == h[None, :]).astype(np.float32), dtype=BF16)


def _silu(x):
    return x * jax.nn.sigmoid(x)


def _gelu(x):
    return 0.5 * x * (1.0 + lax.erf(x * SQRT_HALF))


def _layer_norm(x, g, b):
    xc = x - jnp.mean(x, axis=-1, keepdims=True)
    var = jnp.mean(xc * xc, axis=-1, keepdims=True)
    return xc * lax.rsqrt(var + EPS) * g + b


def _head_rms(x, e_ref, g):
    ms = jnp.dot((x * x).astype(BF16), e_ref[...], preferred_element_type=F32) * (1.0 / 64.0)
    return x * lax.rsqrt(ms + EPS) * g


def _rope(x, cos, sin):
    outs = []
    for half in range(2):
        xh = x[:, 128 * half:128 * half + 128]
        lane = lax.broadcasted_iota(jnp.int32, xh.shape, 1)
        partner = jnp.where((lane & 16) == 0, pltpu.roll(xh, 112, 1), pltpu.roll(xh, 16, 1))
        outs.append(xh * cos + partner * sin)
    return jnp.concatenate(outs, axis=1)


def _lower_bound(lb_ref, layer):
    lb = lb_ref[...]
    e = jnp.exp(lb - jnp.max(lb, axis=0, keepdims=True))
    soft = e / jnp.sum(e, axis=0, keepdims=True)
    acc = jnp.zeros((1, GROUP_W), F32)
    for i in range(1, layer + 1):
        acc = acc + soft[i:i + 1]
    return acc


def _lam(lam_ref):
    lp = lam_ref[0]
    a = jnp.sum(lp[0:1] * lp[1:2], axis=-1, keepdims=True)
    b = jnp.sum(lp[2:3] * lp[3:4], axis=-1, keepdims=True)
    return jnp.exp(a) - jnp.exp(b)


def _lam_init(layer):
    return 0.8 - 0.6 * math.exp(-0.3 * layer)


def _alpha(depth):
    return (2.0 * depth) ** 0.25


def _sg_mix(vn, ws_ref, bias):
    group = lax.broadcasted_iota(jnp.int32, (1, GROUP_W), 1) >> 6
    r = lax.broadcasted_iota(jnp.int32, (ROWS, ROWS), 0)
    c = lax.broadcasted_iota(jnp.int32, (ROWS, ROWS), 1)
    acc = bias
    for g in range(4):
        wm = jnp.where(r >= c, ws_ref[g], 0.0).astype(BF16)
        vm = jnp.where(group == g, vn, 0.0).astype(BF16)
        acc = acc + jnp.dot(wm, vm, preferred_element_type=F32)
    return acc


def _hgrn_gates(dq, df, di, lb):
    f = lb + (1.0 - lb) * jax.nn.sigmoid(df)
    lf = jnp.log(jnp.maximum(f, F_FLOOR))
    hi = lf.astype(BF16)
    lo = (lf - hi.astype(F32)).astype(BF16)
    return _silu(dq), 1.0 - f, di, jnp.concatenate([hi, lo], axis=1)


def _segsums(g_ref, lo, hi, hilo):
    g = g_ref[lo:hi].reshape((hi - lo) * ROWS, ROWS)
    r = jnp.dot(g, hilo, preferred_element_type=F32)
    return r[:, :GROUP_W] + r[:, GROUP_W:]


def _hgrn_intra(qd, kk, vd, seg, e_ref, levels):
    row = lax.broadcasted_iota(jnp.int32, (ROWS, 1), 0)
    head = lax.broadcasted_iota(jnp.int32, (1, GROUP_W), 1) >> 6
    t_idx = lax.broadcasted_iota(jnp.int32, (ROWS, 4 * ROWS), 0)
    s_idx = lax.broadcasted_iota(jnp.int32, (ROWS, 4 * ROWS), 1) & (ROWS - 1)
    att = jnp.zeros((ROWS, 4 * ROWS), F32)
    for i, lvl in enumerate(levels):
        e = jnp.exp(seg[i])
        second = ((row >> lvl) & 1) == 1
        a = jnp.where(second, qd * e, 0.0).astype(BF16)
        b = jnp.where(second, 0.0, kk * e)
        b_heads = jnp.concatenate([jnp.where(head == h, b, 0.0) for h in range(4)], axis=0).astype(BF16)
        s = lax.dot_general(a, b_heads, _NT, preferred_element_type=F32)
        att = att + jnp.where((t_idx >> (lvl + 1)) == (s_idx >> (lvl + 1)), s, 0.0)
    v_stack = jnp.concatenate([jnp.where(head == h, vd, 0.0) for h in range(4)], axis=0).astype(BF16)
    o = jnp.dot(att.astype(BF16), v_stack, preferred_element_type=F32)
    diag = jnp.dot((qd * kk).astype(BF16), e_ref[...], preferred_element_type=F32)
    return o + diag * vd


def _ada_kernel(c_ref, w_ref, b_ref, o_ref):
    c = c_ref[...]
    a = _silu(c).astype(BF16)
    o_ref[0] = jnp.dot(a, w_ref[0].astype(BF16), preferred_element_type=F32) + b_ref[0]


def _ada(c_all, w_ada, b_ada):
    depth = w_ada.shape[0]
    rows = c_all.shape[0]
    return pl.pallas_call(
        _ada_kernel,
        out_shape=jax.ShapeDtypeStruct((depth, rows, 3 * D_MODEL), F32),
        grid=(depth, 3),
        in_specs=[
            pl.BlockSpec((rows, D_MODEL), lambda l, n: (0, 0)),
            pl.BlockSpec((1, D_MODEL, D_MODEL), lambda l, n: (l, 0, n)),
            pl.BlockSpec((1, 1, D_MODEL), lambda l, n: (l, 0, n)),
        ],
        out_specs=pl.BlockSpec((1, rows, D_MODEL), lambda l, n: (l, 0, n)),
        compiler_params=pltpu.CompilerParams(dimension_semantics=("arbitrary", "arbitrary"),
                                             vmem_limit_bytes=VMEM_LIMIT),
        name="adaln",
    )(c_all, w_ada, b_ada.reshape(depth, 1, 3 * D_MODEL))


def _prompt_pre_kernel(layer, tb,
                       x_ref, mod_ref, win_ref, cos_ref, sin_ref, sgg_ref, sgb_ref, ws_ref, sgbias_ref,
                       cw_ref, cb_ref, cng_ref, cnb_ref, wpw_ref, lb_ref, hng_ref, g_ref, e_ref,
                       krow_ref, vrow_ref, qt_ref, kb_ref, vt_ref, ga_ref, obcd_ref, cst_ref, hst_ref,
                       hc_ref, s_ref):
    i = pl.program_id(1)

    @pl.when(i == 0)
    def _():
        hc_ref[0:32, :] = jnp.zeros((32, GROUP_W), F32)
        s_ref[...] = jnp.zeros(s_ref.shape, F32)

    mod = mod_ref[0]
    shift = mod[:, 0:D_MODEL]
    scale = mod[:, D_MODEL:2 * D_MODEL]
    h = (x_ref[0] * (1.0 + scale) + shift).astype(BF16)
    zc = jnp.dot(h, win_ref[0, :, 1792:2560], preferred_element_type=F32)
    zb = jnp.dot(h, win_ref[0, :, 1024:1792], preferred_element_type=F32)
    zd = jnp.dot(h, win_ref[0, :, 2560:3584], preferred_element_type=F32)

    glu = zc[:, 0:GROUP_W] * jax.nn.sigmoid(zc[:, GROUP_W:2 * GROUP_W])
    hc_ref[32:32 + tb, :] = glu
    cst_ref[0] = hc_ref[pl.ds(tb + 2, CONV_W - 1), :]

    lb = _lower_bound(lb_ref, layer)
    pre_d = []
    for c in range(tb // ROWS):
        rs = slice(c * ROWS, (c + 1) * ROWS)
        qd, kk, vd, hilo = _hgrn_gates(zd[rs, 0:256], zd[rs, 256:512], zd[rs, 512:768], lb)
        pre_d.append((qd, kk, vd, hilo, _segsums(g_ref, G_LVL0, G_SUF + 1, hilo)))
    o_b_parts, o_c_parts, o_d_parts = [], [], []
    for c in range(tb // ROWS):
        rs = slice(c * ROWS, (c + 1) * ROWS)
        y = jnp.zeros((ROWS, GROUP_W), F32)
        for j in range(CONV_W):
            y = y + hc_ref[pl.ds(c * ROWS + 2 + j, ROWS), :] * cw_ref[0, j:j + 1, :]
        yn = _silu(_layer_norm(y + cb_ref[0], cng_ref[0], cnb_ref[0]))
        o_c_parts.append(jnp.dot(yn.astype(BF16), wpw_ref[0], preferred_element_type=F32) * _silu(zc[rs, 512:768]))
        vn = _layer_norm(_gelu(zb[rs, 256:512]), sgg_ref[0], sgb_ref[0])
        mixed = _sg_mix(vn, ws_ref.at[0], sgbias_ref[0])
        o_b_parts.append(_gelu(zb[rs, 0:256]) * mixed * _silu(zb[rs, 512:768]))
        qd, kk, vd, hilo, seg = pre_d[c]
        o = _hgrn_intra(qd, kk, vd, [seg[ROWS * n:ROWS * (n + 1)] for n in range(7)], e_ref, range(7))
        aq = qd * jnp.exp(seg[7 * ROWS:8 * ROWS])
        bk = kk * jnp.exp(seg[8 * ROWS:9 * ROWS])
        ones = jnp.ones((ROWS, D_KDIM), BF16)
        inter = []
        for hh in range(D_HEADS):
            lo_, hi_ = 64 * hh, 64 * hh + 64
            st = s_ref[hh]
            inter.append(jnp.dot(aq[:, lo_:hi_].astype(BF16), st.astype(BF16), preferred_element_type=F32))
            dec = (lax.dot_general(hilo[:, lo_:hi_], ones, _TN, preferred_element_type=F32)
                   + lax.dot_general(hilo[:, GROUP_W + lo_:GROUP_W + hi_], ones, _TN, preferred_element_type=F32))
            upd = lax.dot_general(bk[:, lo_:hi_].astype(BF16), vd[:, lo_:hi_].astype(BF16), _TN,
                                  preferred_element_type=F32)
            s_ref[hh] = jnp.exp(dec) * st + upd
        o = o + jnp.concatenate(inter, axis=1)
        o_d_parts.append(_head_rms(o, e_ref, hng_ref[0]) * _silu(zd[rs, 768:1024]))
    za = jnp.dot(h, win_ref[0, :, 0:1024], preferred_element_type=F32)
    cos = cos_ref[...]
    sin = sin_ref[...]
    q = _rope(za[:, 0:256], cos, sin)
    k = _rope(za[:, 256:512], cos, sin)
    v = za[:, 512:768]
    kt = k.T
    vt = v.T
    krow_ref[0] = kt
    vrow_ref[0] = vt
    kb_ref[0] = k.astype(BF16)
    qt_ref[0] = (q * Q_SCALE).T.astype(BF16)
    vt_ref[0] = vt.astype(BF16)
    ga_ref[0] = _silu(za[:, 768:1024])

    hc_ref[0:32, :] = hc_ref[pl.ds(tb, 32), :]
    o_b = jnp.concatenate(o_b_parts, axis=0)
    o_c = jnp.concatenate(o_c_parts, axis=0)
    o_d = jnp.concatenate(o_d_parts, axis=0)
    obcd_ref[0] = jnp.concatenate([o_b, o_c, o_d], axis=1).astype(BF16)
    hst_ref[0] = s_ref[...]


def _prompt_pre(layer, x, mod_p, win_b, cos, sin, sgg, sgb, ws, sgbias, cw, cb, cng, cnb, wpw_b, lbnd, hng,
                gmat, eones):
    b, t, _ = x.shape
    tb = min(PRE_TB, t)
    kernel = functools.partial(_prompt_pre_kernel, layer, tb)
    lsel3 = lambda bb, i: (layer, 0, 0)
    const2 = lambda bb, i: (0, 0)
    const3 = lambda bb, i: (0, 0, 0)
    row_blk = lambda w: pl.BlockSpec((1, tb, w), lambda bb, i: (bb, i, 0))
    col_blk = pl.BlockSpec((1, GROUP_W, tb), lambda bb, i: (bb, 0, i))
    out_shape = (
        jax.ShapeDtypeStruct((b, GROUP_W, t), F32),
        jax.ShapeDtypeStruct((b, GROUP_W, t), F32),
        jax.ShapeDtypeStruct((b, GROUP_W, t), BF16),
        jax.ShapeDtypeStruct((b, t, GROUP_W), BF16),
        jax.ShapeDtypeStruct((b, GROUP_W, t), BF16),
        jax.ShapeDtypeStruct((b, t, GROUP_W), F32),
        jax.ShapeDtypeStruct((b, t, 3 * GROUP_W), BF16),
        jax.ShapeDtypeStruct((b, CONV_W - 1, GROUP_W), F32),
        jax.ShapeDtypeStruct((b, D_HEADS, D_KDIM, D_KDIM), F32),
    )
    return pl.pallas_call(
        kernel,
        out_shape=out_shape,
        grid=(b, t // tb),
        in_specs=[
            row_blk(D_MODEL),
            pl.BlockSpec((1, 1, 3 * D_MODEL), lambda bb, i: (bb, 0, 0)),
            pl.BlockSpec((1, D_MODEL, D_IN), lsel3),
            pl.BlockSpec((tb, 128), lambda bb, i: (i, 0)),
            pl.BlockSpec((tb, 128), lambda bb, i: (i, 0)),
            pl.BlockSpec((1, 1, GROUP_W), lsel3),
            pl.BlockSpec((1, 1, GROUP_W), lsel3),
            pl.BlockSpec((1, 4, ROWS, ROWS), lambda bb, i: (layer, 0, 0, 0)),
            pl.BlockSpec((1, ROWS, GROUP_W), lsel3),
            pl.BlockSpec((1, CONV_W, GROUP_W), lsel3),
            pl.BlockSpec((1, 1, GROUP_W), lsel3),
            pl.BlockSpec((1, 1, GROUP_W), lsel3),
            pl.BlockSpec((1, 1, GROUP_W), lsel3),
            pl.BlockSpec((1, GROUP_W, GROUP_W), lsel3),
            pl.BlockSpec(lbnd.shape, const2),
            pl.BlockSpec((1, 1, GROUP_W), lsel3),
            pl.BlockSpec(gmat.shape, const3),
            pl.BlockSpec(eones.shape, const2),
        ],
        out_specs=(
            col_blk, col_blk, col_blk, row_blk(GROUP_W), col_blk, row_blk(GROUP_W),
            row_blk(3 * GROUP_W),
            pl.BlockSpec((1, CONV_W - 1, GROUP_W), lambda bb, i: (bb, 0, 0)),
            pl.BlockSpec((1, D_HEADS, D_KDIM, D_KDIM), lambda bb, i: (bb, 0, 0, 0)),
        ),
        scratch_shapes=[pltpu.VMEM((32 + tb, GROUP_W), F32), pltpu.VMEM((D_HEADS, D_KDIM, D_KDIM), F32)],
        compiler_params=pltpu.CompilerParams(dimension_semantics=("arbitrary", "arbitrary"),
                                             vmem_limit_bytes=VMEM_LIMIT),
        name=f"prompt_pre_l{layer}",
    )(x, mod_p, win_b, cos, sin, sgg, sgb, ws, sgbias, cw, cb, cng, cnb, wpw_b, lbnd, hng, gmat, eones)


def _prompt_attn_kernel(layer, depth, tq, tk,
                        qt_ref, kb_ref, vt_ref, ga_ref, obcd_ref, x_ref, mod_ref, wout_ref, lam_ref, ang_ref,
                        lng_ref, lnb_ref, y_ref, acc_ref, m_ref, qm_ref, s_ref):
    i = pl.program_id(1)
    qt = qt_ref[0]
    rowg = lax.broadcasted_iota(jnp.int32, (GROUP_W, 1), 0) >> 5
    for j in range(8):
        qm_ref[j] = jnp.where(rowg == j, qt, jnp.zeros_like(qt))
    m_ref[...] = jnp.full(m_ref.shape, NEG_BIG, F32)
    acc_ref[...] = jnp.zeros(acc_ref.shape, F32)
    ones = jnp.ones((16, tk), BF16)
    kpq = tq // tk

    def tile(kt, diag):
        masked = diag is not None
        off = pl.multiple_of(kt * tk, tk)
        kk = kb_ref[0, pl.ds(off, tk), :]
        if masked:
            key = lax.broadcasted_iota(jnp.int32, (tk, tq), 0) + diag * tk
            qry = lax.broadcasted_iota(jnp.int32, (tk, tq), 1)
            visible = key <= qry
        vvs = [jnp.concatenate([vt_ref[0, 64 * hh:64 * hh + 64, pl.ds(off, tk)], ones], axis=0)
               for hh in range(A_HEADS)]

        def scores(j):
            s = jnp.dot(kk, qm_ref[j], preferred_element_type=F32)
            if masked:
                s = jnp.where(visible, s, NEG_BIG)
            s_ref[j] = s
            return jnp.max(s, axis=0, keepdims=True)

        def update(j, cmax):
            m_old = m_ref[j]
            m_new = jnp.maximum(m_old, cmax)
            alpha = jnp.exp2(m_old - m_new)
            m_ref[j] = m_new
            for c in range(tq // LANE_TILE):
                cols = slice(LANE_TILE * c, LANE_TILE * (c + 1))
                p = jnp.exp2(s_ref[j, :, cols] - m_new[:, cols]).astype(BF16)
                acc_ref[j, :, cols] = (alpha[:, cols] * acc_ref[j, :, cols]
                                       + jnp.dot(vvs[j // 2], p, preferred_element_type=F32))

        cmax = {}
        for j in range(8 + SCORE_LEAD):
            if j < 8:
                cmax[j] = scores(j)
            if j >= SCORE_LEAD:
                update(j - SCORE_LEAD, cmax[j - SCORE_LEAD])

    def body(kt, carry):
        tile(kt, None)
        return carry

    lax.fori_loop(0, i * kpq, body, 0)
    for dg in range(kpq):
        tile(i * kpq + dg, dg)

    lam = _lam(lam_ref) + _lam_init(layer)
    heads = []
    for hh in range(A_HEADS):
        a0 = acc_ref[2 * hh]
        a1 = acc_ref[2 * hh + 1]
        d = a0[0:A_VDIM] / a0[A_VDIM:A_VDIM + 1] - lam * (a1[0:A_VDIM] / a1[A_VDIM:A_VDIM + 1])
        ms = jnp.mean(d * d, axis=0, keepdims=True)
        heads.append(d * lax.rsqrt(ms + EPS))
    o_a = jnp.concatenate(heads, axis=0).T
    o_a = o_a * (ang_ref[0] * (1.0 - _lam_init(layer))) * ga_ref[0]
    mixed = (jnp.dot(o_a.astype(BF16), wout_ref[0, 0:GROUP_W, :], preferred_element_type=F32)
             + jnp.dot(obcd_ref[0], wout_ref[0, GROUP_W:, :], preferred_element_type=F32))
    gate = mod_ref[0][:, 2 * D_MODEL:]
    y_ref[0] = _layer_norm(_alpha(depth) * x_ref[0] + gate * mixed, lng_ref[0], lnb_ref[0])


def _prompt_attn(layer, depth, qt, kb, vt, ga, obcd, x, mod_p, wout_b, lam_qk, ang, lng, lnb):
    b, t, _ = x.shape
    tq = min(ATTN_TQ, t)
    tk = min(ATTN_TK, tq)
    kernel = functools.partial(_prompt_attn_kernel, layer, depth, tq, tk)
    lsel3 = lambda bb, i: (layer, 0, 0)
    row_blk = lambda w: pl.BlockSpec((1, tq, w), lambda bb, i: (bb, i, 0))
    return pl.pallas_call(
        kernel,
        out_shape=jax.ShapeDtypeStruct((b, t, D_MODEL), F32),
        grid=(b, t // tq),
        in_specs=[
            pl.BlockSpec((1, GROUP_W, tq), lambda bb, i: (bb, 0, i)),
            pl.BlockSpec((1, t, GROUP_W), lambda bb, i: (bb, 0, 0)),
            pl.BlockSpec((1, GROUP_W, t), lambda bb, i: (bb, 0, 0)),
            row_blk(GROUP_W),
            row_blk(3 * GROUP_W),
            row_blk(D_MODEL),
            pl.BlockSpec((1, 1, 3 * D_MODEL), lambda bb, i: (bb, 0, 0)),
            pl.BlockSpec((1, D_MODEL, D_MODEL), lsel3),
            pl.BlockSpec((1, 4, A_HALF), lsel3),
            pl.BlockSpec((1, 1, GROUP_W), lsel3),
            pl.BlockSpec((1, 1, D_MODEL), lsel3),
            pl.BlockSpec((1, 1, D_MODEL), lsel3),
        ],
        out_specs=row_blk(D_MODEL),
        scratch_shapes=[pltpu.VMEM((8, A_VDIM + 16, tq), F32), pltpu.VMEM((8, 1, tq), F32),
                        pltpu.VMEM((8, GROUP_W, tq), BF16), pltpu.VMEM((8, tk, tq), F32)],
        compiler_params=pltpu.CompilerParams(dimension_semantics=("arbitrary", "arbitrary"),
                                             vmem_limit_bytes=VMEM_LIMIT),
        name=f"prompt_attn_l{layer}",
    )(qt, kb, vt, ga, obcd, x, mod_p, wout_b, lam_qk, ang, lng, lnb)


def _sample_pre_kernel(layer, ts,
                       x_ref, mod_ref, win_ref, cos_ref, sin_ref, sgg_ref, sgb_ref, ws_ref, sgbias_ref,
                       cw_ref, cb_ref, cng_ref, cnb_ref, wpw_ref, lb_ref, hng_ref, g_ref, e_ref, cst_ref, hst_ref,
                       krow_ref, vrow_ref, q_ref, ga_ref, obcd_ref, chv_ref, ncst_ref, nhst_ref,
                       hc_ref):
    nseq = ROWS // ts
    mod = mod_ref[...]
    shift = mod[:, :, 0:D_MODEL]
    scale = mod[:, :, D_MODEL:2 * D_MODEL]
    h = (x_ref[...] * (1.0 + scale) + shift).reshape(ROWS, D_MODEL).astype(BF16)
    z = jnp.dot(h, win_ref[0], preferred_element_type=F32)

    cos = cos_ref[...]
    sin = sin_ref[...]
    k = _rope(z[:, 256:512], cos, sin)
    krow_ref[...] = k
    vrow_ref[...] = z[:, 512:768]
    q_ref[...] = _rope(z[:, 0:256], cos, sin) * Q_SCALE
    ga_ref[...] = _silu(z[:, 768:1024])

    vn = _layer_norm(_gelu(z[:, 1280:1536]), sgg_ref[0], sgb_ref[0])
    chv_ref[...] = vn
    o_b = _gelu(z[:, 1024:1280]) * _sg_mix(vn, ws_ref.at[0], sgbias_ref[0]) * _silu(z[:, 1536:1792])

    a = z[:, 1792:2304]
    glu = a[:, :GROUP_W] * jax.nn.sigmoid(a[:, GROUP_W:])
    hc_ref[:, 0:CONV_W - 1, :] = cst_ref[0]
    hc_ref[:, CONV_W - 1:CONV_W - 1 + ts, :] = glu.reshape(nseq, ts, GROUP_W)
    y = jnp.zeros((nseq, ts, GROUP_W), F32)
    for j in range(CONV_W):
        y = y + hc_ref[:, j:j + ts, :] * cw_ref[0, j:j + 1, :]
    ncst_ref[0] = hc_ref[:, ts:ts + CONV_W - 1, :]
    yn = _silu(_layer_norm(y.reshape(ROWS, GROUP_W) + cb_ref[0], cng_ref[0], cnb_ref[0]))
    o_c = jnp.dot(yn.astype(BF16), wpw_ref[0], preferred_element_type=F32) * _silu(z[:, 2304:2560])

    lb = _lower_bound(lb_ref, layer)
    qd, kk, vd, hilo = _hgrn_gates(z[:, 2560:2816], z[:, 2816:3072], z[:, 3072:3328], lb)
    seg = _segsums(g_ref, G_CUM8, G_LVL0 + 3, hilo)
    o = _hgrn_intra(qd, kk, vd, [seg[ROWS * (2 + n):ROWS * (3 + n)] for n in range(3)], e_ref, range(3))
    aq = (qd * jnp.exp(seg[0:ROWS])).reshape(nseq, ts, GROUP_W)
    bk = (kk * jnp.exp(seg[ROWS:2 * ROWS])).reshape(nseq, ts, GROUP_W)
    v3 = vd.reshape(nseq, ts, GROUP_W)
    hilo3 = hilo.astype(F32).reshape(nseq, ts, 2 * GROUP_W)
    ones = jnp.ones((nseq, ts, D_KDIM), BF16)
    inter = []
    for hh in range(D_HEADS):
        lo_, hi_ = 64 * hh, 64 * hh + 64
        st = hst_ref[0, :, hh]
        inter.append(jnp.einsum('bqk,bkv->bqv', aq[:, :, lo_:hi_].astype(BF16), st.astype(BF16),
                                preferred_element_type=F32))
        dec = (jnp.einsum('bsk,bsv->bkv', hilo3[:, :, lo_:hi_].astype(BF16), ones, preferred_element_type=F32)
               + jnp.einsum('bsk,bsv->bkv', hilo3[:, :, GROUP_W + lo_:GROUP_W + hi_].astype(BF16), ones,
                            preferred_element_type=F32))
        upd = jnp.einsum('bsk,bsv->bkv', bk[:, :, lo_:hi_].astype(BF16), v3[:, :, lo_:hi_].astype(BF16),
                         preferred_element_type=F32)
        nhst_ref[0, :, hh] = jnp.exp(dec) * st + upd
    o = o + jnp.concatenate(inter, axis=2).reshape(ROWS, GROUP_W)
    o_d = _head_rms(o, e_ref, hng_ref[0]) * _silu(z[:, 3328:3584])
    obcd_ref[...] = jnp.concatenate([o_b, o_c, o_d], axis=1).astype(BF16)


def _sample_pre(layer, x, mod_s, win_b, cos, sin, sgg, sgb, ws_blk, sgbias, cw, cb, cng, cnb, wpw_b, lbnd, hng,
                gmat, eones, state_conv, state_hgrn):
    bs, ts, _ = x.shape
    nseq = ROWS // ts
    nblk = bs // nseq
    n = bs * ts
    kernel = functools.partial(_sample_pre_kernel, layer, ts)
    lsel3 = lambda i: (layer, 0, 0)
    const2 = lambda i: (0, 0)
    const3 = lambda i: (0, 0, 0)
    row_blk = lambda w: pl.BlockSpec((ROWS, w), lambda i: (i, 0))
    out_shape = (
        jax.ShapeDtypeStruct((n, GROUP_W), F32),
        jax.ShapeDtypeStruct((n, GROUP_W), F32),
        jax.ShapeDtypeStruct((n, GROUP_W), F32),
        jax.ShapeDtypeStruct((n, GROUP_W), F32),
        jax.ShapeDtypeStruct((n, 3 * GROUP_W), BF16),
        jax.ShapeDtypeStruct((n, GROUP_W), F32),
        jax.ShapeDtypeStruct((1, bs, CONV_W - 1, GROUP_W), F32),
        jax.ShapeDtypeStruct((1, bs, D_HEADS, D_KDIM, D_KDIM), F32),
    )
    return pl.pallas_call(
        kernel,
        out_shape=out_shape,
        grid=(nblk,),
        in_specs=[
            pl.BlockSpec((nseq, ts, D_MODEL), lambda i: (i, 0, 0)),
            pl.BlockSpec((nseq, 1, 3 * D_MODEL), lambda i: (i, 0, 0)),
            pl.BlockSpec((1, D_MODEL, D_IN), lsel3),
            pl.BlockSpec((ROWS, 128), const2),
            pl.BlockSpec((ROWS, 128), const2),
            pl.BlockSpec((1, 1, GROUP_W), lsel3),
            pl.BlockSpec((1, 1, GROUP_W), lsel3),
            pl.BlockSpec((1, 4, ROWS, ROWS), lambda i: (layer, 0, 0, 0)),
            pl.BlockSpec((1, ROWS, GROUP_W), lsel3),
            pl.BlockSpec((1, CONV_W, GROUP_W), lsel3),
            pl.BlockSpec((1, 1, GROUP_W), lsel3),
            pl.BlockSpec((1, 1, GROUP_W), lsel3),
            pl.BlockSpec((1, 1, GROUP_W), lsel3),
            pl.BlockSpec((1, GROUP_W, GROUP_W), lsel3),
            pl.BlockSpec(lbnd.shape, const2),
            pl.BlockSpec((1, 1, GROUP_W), lsel3),
            pl.BlockSpec(gmat.shape, const3),
            pl.BlockSpec(eones.shape, const2),
            pl.BlockSpec((1, nseq, CONV_W - 1, GROUP_W), lambda i: (layer, i, 0, 0)),
            pl.BlockSpec((1, nseq, D_HEADS, D_KDIM, D_KDIM), lambda i: (layer, i, 0, 0, 0)),
        ],
        out_specs=(
            row_blk(GROUP_W), row_blk(GROUP_W), row_blk(GROUP_W), row_blk(GROUP_W), row_blk(3 * GROUP_W),
            row_blk(GROUP_W),
            pl.BlockSpec((1, nseq, CONV_W - 1, GROUP_W), lambda i: (0, i, 0, 0)),
            pl.BlockSpec((1, nseq, D_HEADS, D_KDIM, D_KDIM), lambda i: (0, i, 0, 0, 0)),
        ),
        scratch_shapes=[pltpu.VMEM((nseq, 40, GROUP_W), F32)],
        compiler_params=pltpu.CompilerParams(dimension_semantics=("arbitrary",), vmem_limit_bytes=VMEM_LIMIT),
        name=f"sample_pre_l{layer}",
    )(x, mod_s, win_b, cos, sin, sgg, sgb, ws_blk, sgbias, cw, cb, cng, cnb, wpw_b, lbnd, hng, gmat, eones,
      state_conv, state_hgrn)


def _sample_attn_kernel(layer, n_pages, page, ts,
                        pt_ref, ck_hbm, cv_hbm, q_ref, kn_ref, vn_ref, ga_ref, lam_ref, ang_ref, e_ref, o_ref,
                        kbuf, vbuf, sem):
    b = pl.program_id(0)
    nb = pl.num_programs(0)

    def page_copies(seq, slot):
        copies = []
        for j in range(n_pages):
            pid = pt_ref[seq, j]
            copies.append(pltpu.make_async_copy(ck_hbm.at[layer, pid], kbuf.at[slot, j], sem.at[0, slot]))
            copies.append(pltpu.make_async_copy(cv_hbm.at[layer, pid], vbuf.at[slot, j], sem.at[1, slot]))
        return copies

    def start_all(copies):
        for n, cp in enumerate(copies):
            cp.start(priority=n % 2)

    @pl.when(b == 0)
    def _():
        for d in range(PAGE_SLOTS - 1):
            start_all(page_copies(d, d))

    @pl.when(b + PAGE_SLOTS - 1 < nb)
    def _():
        start_all(page_copies(b + PAGE_SLOTS - 1, (b + PAGE_SLOTS - 1) % PAGE_SLOTS))

    slot = b % PAGE_SLOTS
    for cp in page_copies(b, slot):
        cp.wait()

    grp = lax.broadcasted_iota(jnp.int32, (1, GROUP_W), 1) >> 5
    head = lax.broadcasted_iota(jnp.int32, (1, GROUP_W), 1) >> 6
    t_q = lax.broadcasted_iota(jnp.int32, (8 * ts, ts), 0) & (ts - 1)
    t_k = lax.broadcasted_iota(jnp.int32, (8 * ts, ts), 1)
    vis = t_k <= t_q
    lam = _lam(lam_ref) + _lam_init(layer)
    q = q_ref[b]
    qexp = jnp.concatenate([jnp.where(grp == j, q, 0.0) for j in range(8)], axis=0).astype(BF16)
    kt_all = jnp.concatenate([kbuf[slot, j] for j in range(n_pages)], axis=1).astype(BF16)
    vt_all = jnp.concatenate([vbuf[slot, j] for j in range(n_pages)], axis=1).astype(BF16)
    s_past = jnp.dot(qexp, kt_all, preferred_element_type=F32)
    s_new = lax.dot_general(qexp, kn_ref[b].astype(BF16), _NT, preferred_element_type=F32)
    s_new = jnp.where(vis, s_new, NEG_BIG)
    m = jnp.maximum(jnp.max(s_past, axis=-1, keepdims=True), jnp.max(s_new, axis=-1, keepdims=True))
    p_past = jnp.exp2(s_past - m)
    p_new = jnp.where(vis, jnp.exp2(s_new - m), 0.0)
    l = jnp.sum(p_past, axis=-1, keepdims=True) + jnp.sum(p_new, axis=-1, keepdims=True)
    o = (lax.dot_general(p_past.astype(BF16), vt_all, _NT, preferred_element_type=F32)
         + jnp.dot(p_new.astype(BF16), vn_ref[b].astype(BF16), preferred_element_type=F32)) / l
    o_a = jnp.zeros((ts, GROUP_W), F32)
    for hh in range(A_HEADS):
        d = o[2 * hh * ts:(2 * hh + 1) * ts] - lam * o[(2 * hh + 1) * ts:(2 * hh + 2) * ts]
        o_a = o_a + jnp.where(head == hh, d, 0.0)
    o_a = _head_rms(o_a, e_ref, ang_ref[0] * (1.0 - _lam_init(layer))) * ga_ref[b]
    o_ref[b] = o_a.astype(BF16)


def _sample_attn(layer, page_table, cache_k, cache_v, q_s, k_new, v_new, ga_s, lam_qk, ang, eones, ts):
    bs, n_pages = page_table.shape
    page = cache_k.shape[3]
    assert bs >= PAGE_SLOTS
    kernel = functools.partial(_sample_attn_kernel, layer, n_pages, page, ts)
    seq_blk = pl.BlockSpec((bs, ts, GROUP_W), lambda b, pt: (0, 0, 0))
    lsel3 = lambda b, pt: (layer, 0, 0)
    grid_spec = pltpu.PrefetchScalarGridSpec(
        num_scalar_prefetch=1,
        grid=(bs,),
        in_specs=[pl.BlockSpec(memory_space=pl.ANY), pl.BlockSpec(memory_space=pl.ANY),
                  seq_blk, seq_blk, seq_blk, seq_blk,
                  pl.BlockSpec((1, 4, A_HALF), lsel3),
                  pl.BlockSpec((1, 1, GROUP_W), lsel3),
                  pl.BlockSpec(eones.shape, lambda b, pt: (0, 0))],
        out_specs=seq_blk,
        scratch_shapes=[pltpu.VMEM((PAGE_SLOTS, n_pages, GROUP_W, page), F32),
                        pltpu.VMEM((PAGE_SLOTS, n_pages, GROUP_W, page), F32),
                        pltpu.SemaphoreType.DMA((2, PAGE_SLOTS))],
    )
    shp3 = (bs, ts, GROUP_W)
    return pl.pallas_call(
        kernel,
        out_shape=jax.ShapeDtypeStruct(shp3, BF16),
        grid_spec=grid_spec,
        compiler_params=pltpu.CompilerParams(dimension_semantics=("arbitrary",), vmem_limit_bytes=VMEM_LIMIT),
        name=f"sample_attn_l{layer}",
    )(page_table, cache_k, cache_v,
      q_s.reshape(shp3), k_new.reshape(shp3), v_new.reshape(shp3), ga_s.reshape(shp3), lam_qk, ang, eones)


def _sample_out_kernel(depth, ts, oa_ref, obcd_ref, x_ref, mod_ref, wout_ref, lng_ref, lnb_ref, y_ref):
    nseq = ROWS // ts
    mixed = (jnp.dot(oa_ref[...], wout_ref[0, 0:GROUP_W, :], preferred_element_type=F32)
             + jnp.dot(obcd_ref[...], wout_ref[0, GROUP_W:, :], preferred_element_type=F32))
    gate = mod_ref[...][:, :, 2 * D_MODEL:]
    y = _alpha(depth) * x_ref[...] + gate * mixed.reshape(nseq, ts, D_MODEL)
    y_ref[...] = _layer_norm(y, lng_ref[0], lnb_ref[0])


def _sample_out(layer, depth, oa, obcd, x, mod_s, wout_b, lng, lnb):
    bs, ts, _ = x.shape
    nseq = ROWS // ts
    lsel3 = lambda i: (layer, 0, 0)
    return pl.pallas_call(
        functools.partial(_sample_out_kernel, depth, ts),
        out_shape=jax.ShapeDtypeStruct(x.shape, F32),
        grid=(bs // nseq,),
        in_specs=[
            pl.BlockSpec((ROWS, GROUP_W), lambda i: (i, 0)),
            pl.BlockSpec((ROWS, 3 * GROUP_W), lambda i: (i, 0)),
            pl.BlockSpec((nseq, ts, D_MODEL), lambda i: (i, 0, 0)),
            pl.BlockSpec((nseq, 1, 3 * D_MODEL), lambda i: (i, 0, 0)),
            pl.BlockSpec((1, D_MODEL, D_MODEL), lsel3),
            pl.BlockSpec((1, 1, D_MODEL), lsel3),
            pl.BlockSpec((1, 1, D_MODEL), lsel3),
        ],
        out_specs=pl.BlockSpec((nseq, ts, D_MODEL), lambda i: (i, 0, 0)),
        compiler_params=pltpu.CompilerParams(dimension_semantics=("arbitrary",), vmem_limit_bytes=VMEM_LIMIT),
        name=f"sample_out_l{layer}",
    )(oa, obcd, x, mod_s, wout_b, lng, lnb)


def _rope_tables(pos):
    half = A_HALF // 2
    inv = ROPE_THETA ** (-jnp.arange(half, dtype=F32) * 2.0 / A_HALF)
    ang = pos.astype(F32)[:, None] * inv[None, :]
    cos = jnp.cos(ang)
    sin = jnp.sin(ang)
    return jnp.tile(jnp.concatenate([cos, cos], -1), (1, 4)), jnp.tile(jnp.concatenate([-sin, sin], -1), (1, 4))


def kernel(x_prompt, x_sample, cache_k, cache_v, state_conv, state_hgrn, page_table, c_prompt, c_sample, w_ada, b_ada, w_in, lam_qk, attn_norm_g, sg_norm_g, sg_norm_b, w_s, b_s, conv_w, conv_b, conv_norm_g, conv_norm_b, w_pw, lower_bounds, hgrn_norm_g, w_out, ln_g, ln_b):
    depth = w_in.shape[0]
    bp, t, _ = x_prompt.shape
    bs, ts, _ = x_sample.shape
    n_pool, page = cache_k.shape[1], cache_k.shape[2]
    past_len = page_table.shape[1] * page
    assert ts == 8 and ROWS % ts == 0 and bs % (ROWS // ts) == 0 and t % ROWS == 0
    nseq = ROWS // ts

    gmat = _segment_matrices()
    eones = _head_ones()
    win_b = w_in.astype(BF16)
    wout_b = w_out.astype(BF16)
    wpw_b = w_pw.astype(BF16)
    row3 = lambda a: a.reshape(depth, 1, a.shape[-1])
    sgg, sgb, cb, cng, cnb = row3(sg_norm_g), row3(sg_norm_b), row3(conv_b), row3(conv_norm_g), row3(conv_norm_b)
    lng, lnb = row3(ln_g), row3(ln_b)
    ang = row3(jnp.tile(attn_norm_g, (1, A_HEADS)))
    hng = row3(jnp.tile(hgrn_norm_g, (1, D_HEADS)))
    sgbias_p = jnp.repeat(jnp.swapaxes(b_s, 1, 2), GROUP_W // 4, axis=2)
    sgbias_s = jnp.tile(sgbias_p[:, :ts], (1, nseq, 1))
    eye = jnp.eye(nseq, dtype=F32)
    ws_blk = jnp.einsum('ab,lgts->lgatbs', eye, w_s[:, :, :ts, :ts]).reshape(depth, 4, ROWS, ROWS)
    cos_p, sin_p = _rope_tables(jnp.arange(t))
    cos_s, sin_s = _rope_tables(past_len + jnp.arange(ts))
    cos_s, sin_s = jnp.tile(cos_s, (nseq, 1)), jnp.tile(sin_s, (nseq, 1))
    ck = jnp.transpose(cache_k, (0, 1, 3, 4, 2)).reshape(depth, n_pool, GROUP_W, page)
    cv = jnp.transpose(cache_v, (0, 1, 3, 4, 2)).reshape(depth, n_pool, GROUP_W, page)

    rows = bp + bs
    pad = (-rows) % 8
    c_all = jnp.concatenate([c_prompt, c_sample, jnp.zeros((pad, D_MODEL), F32)], axis=0)
    mod = _ada(c_all, w_ada, b_ada)

    xp, xs = x_prompt, x_sample
    outs = [[] for _ in range(9)]
    for l in range(depth):
        mod_p = mod[l, :bp].reshape(bp, 1, 3 * D_MODEL)
        mod_s = mod[l, bp:bp + bs].reshape(bs, 1, 3 * D_MODEL)
        krow, vrow, qt, kb, vt, ga, obcd, cst_p, hst_p = _prompt_pre(
            l, xp, mod_p, win_b, cos_p, sin_p, sgg, sgb, w_s, sgbias_p, conv_w, cb, cng, cnb, wpw_b, lower_bounds,
            hng, gmat, eones)
        xp = _prompt_attn(l, depth, qt, kb, vt, ga, obcd, xp, mod_p, wout_b, lam_qk, ang, lng, lnb)

        krow_s, vrow_s, q_s, ga_s, obcd_s, chv_s, cst_s, hst_s = _sample_pre(
            l, xs, mod_s, win_b, cos_s, sin_s, sgg, sgb, ws_blk, sgbias_s, conv_w, cb, cng, cnb, wpw_b,
            lower_bounds, hng, gmat, eones, state_conv, state_hgrn)
        oa_s = _sample_attn(l, page_table, ck, cv, q_s, krow_s, vrow_s, ga_s, lam_qk, ang, eones, ts)
        xs = _sample_out(l, depth, oa_s.reshape(bs * ts, GROUP_W), obcd_s, xs, mod_s, wout_b, lng, lnb)

        outs[0].append(jnp.transpose(krow.reshape(bp, A_HEADS, 2 * A_HALF, t), (0, 3, 1, 2)))
        outs[1].append(jnp.transpose(vrow.reshape(bp, A_HEADS, A_VDIM, t), (0, 3, 1, 2)))
        outs[2].append(krow_s.reshape(bs, ts, A_HEADS, 2 * A_HALF))
        outs[3].append(vrow_s.reshape(bs, ts, A_HEADS, A_VDIM))
        outs[4].append(chv_s.reshape(bs, ts, GROUP_W))
        outs[5].append(cst_p)
        outs[6].append(cst_s[0])
        outs[7].append(hst_p)
        outs[8].append(hst_s[0])
    return (xp, xs) + tuple(jnp.stack(o) for o in outs)
```

```python
import functools
import math

import jax
import jax.numpy as jnp
import numpy as np
from jax import lax
from jax.experimental import pallas as pl
from jax.experimental.pallas import tpu as pltpu

F32 = jnp.float32
BF16 = jnp.bfloat16

D_MODEL = 1024
GROUP_W = 256
A_HEADS = 4
A_HALF = 32
A_VDIM = 64
ROPE_THETA = 10000.0
SG_CHUNK = 128
CONV_W = 31
D_HEADS = 4
D_KDIM = 64
F_FLOOR = 1e-30
EPS = 1e-5
NEG_BIG = -1e30
D_IN = 14 * GROUP_W
SQRT_HALF = 0.7071067811865476
Q_SCALE = (A_HALF ** -0.5) * math.log2(math.e)

ROWS = 128
PRE_TB = 512
ATTN_TQ = 512
ATTN_TK = 512
PAGE_SLOTS = 4
LANE_TILE = 256
SCORE_LEAD = 8
VMEM_LIMIT = 56 * 1024 * 1024

G_CUM8, G_SUF8, G_LVL0, G_CUM, G_SUF = 0, 1, 2, 9, 10

_NT = (((1,), (1,)), ((), ()))
_TN = (((0,), (0,)), ((), ()))


def _segment_matrices():
    t = np.arange(ROWS)[:, None]
    s = np.arange(ROWS)[None, :]
    same8 = (t >> 3) == (s >> 3)
    mats = [same8 & (s <= t), same8 & (s > t)]
    for lvl in range(7):
        mid = ((t >> (lvl + 1)) << (lvl + 1)) + (1 << lvl)
        second = ((t >> lvl) & 1) == 1
        mats.append(np.where(second, (s >= mid) & (s <= t), (s > t) & (s < mid)))
    mats.append(s <= t)
    mats.append(s > t)
    return jnp.asarray(np.stack(mats).astype(np.float32), dtype=BF16)


def _head_ones():
    h = np.arange(GROUP_W) // 64
    return jnp.asarray((h[:, None] == h[None, :]).astype(np.float32), dtype=BF16)


def _silu(x):
    return x * jax.nn.sigmoid(x)


def _gelu(x):
    return 0.5 * x * (1.0 + lax.erf(x * SQRT_HALF))


def _layer_norm(x, g, b):
    xc = x - jnp.mean(x, axis=-1, keepdims=True)
    var = jnp.mean(xc * xc, axis=-1, keepdims=True)
    return xc * lax.rsqrt(var + EPS) * g + b


def _head_rms(x, e_ref, g):
    ms = jnp.dot((x * x).astype(BF16), e_ref[...], preferred_element_type=F32) * (1.0 / 64.0)
    return x * lax.rsqrt(ms + EPS) * g


def _rope(x, cos, sin):
    outs = []
    for half in range(2):
        xh = x[:, 128 * half:128 * half + 128]
        lane = lax.broadcasted_iota(jnp.int32, xh.shape, 1)
        partner = jnp.where((lane & 16) == 0, pltpu.roll(xh, 112, 1), pltpu.roll(xh, 16, 1))
        outs.append(xh * cos + partner * sin)
    return jnp.concatenate(outs, axis=1)


def _lower_bound(lb_ref, layer):
    lb = lb_ref[...]
    e = jnp.exp(lb - jnp.max(lb, axis=0, keepdims=True))
    soft = e / jnp.sum(e, axis=0, keepdims=True)
    acc = jnp.zeros((1, GROUP_W), F32)
    for i in range(1, layer + 1):
        acc = acc + soft[i:i + 1]
    return acc


def _lam(lam_ref):
    lp = lam_ref[0]
    a = jnp.sum(lp[0:1] * lp[1:2], axis=-1, keepdims=True)
    b = jnp.sum(lp[2:3] * lp[3:4], axis=-1, keepdims=True)
    return jnp.exp(a) - jnp.exp(b)


def _lam_init(layer):
    return 0.8 - 0.6 * math.exp(-0.3 * layer)


def _alpha(depth):
    return (2.0 * depth) ** 0.25


def _sg_mix(vn, ws_ref, bias):
    group = lax.broadcasted_iota(jnp.int32, (1, GROUP_W), 1) >> 6
    r = lax.broadcasted_iota(jnp.int32, (ROWS, ROWS), 0)
    c = lax.broadcasted_iota(jnp.int32, (ROWS, ROWS), 1)
    acc = bias
    for g in range(4):
        wm = jnp.where(r >= c, ws_ref[g], 0.0).astype(BF16)
        vm = jnp.where(group == g, vn, 0.0).astype(BF16)
        acc = acc + jnp.dot(wm, vm, preferred_element_type=F32)
    return acc


def _hgrn_gates(dq, df, di, lb):
    f = lb + (1.0 - lb) * jax.nn.sigmoid(df)
    lf = jnp.log(jnp.maximum(f, F_FLOOR))
    hi = lf.astype(BF16)
    lo = (lf - hi.astype(F32)).astype(BF16)
    return _silu(dq), 1.0 - f, di, jnp.concatenate([hi, lo], axis=1)


def _segsums(g_ref, lo, hi, hilo):
    g = g_ref[lo:hi].reshape((hi - lo) * ROWS, ROWS)
    r = jnp.dot(g, hilo, preferred_element_type=F32)
    return r[:, :GROUP_W] + r[:, GROUP_W:]


def _hgrn_intra(qd, kk, vd, seg, e_ref, levels):
    row = lax.broadcasted_iota(jnp.int32, (ROWS, 1), 0)
    head = lax.broadcasted_iota(jnp.int32, (1, GROUP_W), 1) >> 6
    t_idx = lax.broadcasted_iota(jnp.int32, (ROWS, 4 * ROWS), 0)
    s_idx = lax.broadcasted_iota(jnp.int32, (ROWS, 4 * ROWS), 1) & (ROWS - 1)
    att = jnp.zeros((ROWS, 4 * ROWS), F32)
    for i, lvl in enumerate(levels):
        e = jnp.exp(seg[i])
        second = ((row >> lvl) & 1) == 1
        a = jnp.where(second, qd * e, 0.0).astype(BF16)
        b = jnp.where(second, 0.0, kk * e)
        b_heads = jnp.concatenate([jnp.where(head == h, b, 0.0) for h in range(4)], axis=0).astype(BF16)
        s = lax.dot_general(a, b_heads, _NT, preferred_element_type=F32)
        att = att + jnp.where((t_idx >> (lvl + 1)) == (s_idx >> (lvl + 1)), s, 0.0)
    v_stack = jnp.concatenate([jnp.where(head == h, vd, 0.0) for h in range(4)], axis=0).astype(BF16)
    o = jnp.dot(att.astype(BF16), v_stack, preferred_element_type=F32)
    diag = jnp.dot((qd * kk).astype(BF16), e_ref[...], preferred_element_type=F32)
    return o + diag * vd


def _ada_kernel(c_ref, w_ref, b_ref, o_ref):
    c = c_ref[...]
    a = _silu(c).astype(BF16)
    o_ref[0] = jnp.dot(a, w_ref[0].astype(BF16), preferred_element_type=F32) + b_ref[0]


def _ada(c_all, w_ada, b_ada):
    depth = w_ada.shape[0]
    rows = c_all.shape[0]
    return pl.pallas_call(
        _ada_kernel,
        out_shape=jax.ShapeDtypeStruct((depth, rows, 3 * D_MODEL), F32),
        grid=(depth, 3),
        in_specs=[
            pl.BlockSpec((rows, D_MODEL), lambda l, n: (0, 0)),
            pl.BlockSpec((1, D_MODEL, D_MODEL), lambda l, n: (l, 0, n)),
            pl.BlockSpec((1, 1, D_MODEL), lambda l, n: (l, 0, n)),
        ],
        out_specs=pl.BlockSpec((1, rows, D_MODEL), lambda l, n: (l, 0, n)),
        compiler_params=pltpu.CompilerParams(dimension_semantics=("arbitrary", "arbitrary"),
                                             vmem_limit_bytes=VMEM_LIMIT),
        name="adaln",
    )(c_all, w_ada, b_ada.reshape(depth, 1, 3 * D_MODEL))


def _prompt_pre_kernel(layer, tb,
                       x_ref, mod_ref, win_ref, cos_ref, sin_ref, sgg_ref, sgb_ref, ws_ref, sgbias_ref,
                       cw_ref, cb_ref, cng_ref, cnb_ref, wpw_ref, lb_ref, hng_ref, g_ref, e_ref,
                       krow_ref, vrow_ref, qt_ref, kb_ref, vt_ref, ga_ref, obcd_ref, cst_ref, hst_ref,
                       hc_ref, s_ref):
    i = pl.program_id(1)

    @pl.when(i == 0)
    def _():
        hc_ref[0:32, :] = jnp.zeros((32, GROUP_W), F32)
        s_ref[...] = jnp.zeros(s_ref.shape, F32)

    mod = mod_ref[0]
    shift = mod[:, 0:D_MODEL]
    scale = mod[:, D_MODEL:2 * D_MODEL]
    h = (x_ref[0] * (1.0 + scale) + shift).astype(BF16)
    zc = jnp.dot(h, win_ref[0, :, 1792:2560], preferred_element_type=F32)
    zb = jnp.dot(h, win_ref[0, :, 1024:1792], preferred_element_type=F32)
    zd = jnp.dot(h, win_ref[0, :, 2560:3584], preferred_element_type=F32)

    glu = zc[:, 0:GROUP_W] * jax.nn.sigmoid(zc[:, GROUP_W:2 * GROUP_W])
    hc_ref[32:32 + tb, :] = glu
    cst_ref[0] = hc_ref[pl.ds(tb + 2, CONV_W - 1), :]

    lb = _lower_bound(lb_ref, layer)
    pre_d = []
    for c in range(tb // ROWS):
        rs = slice(c * ROWS, (c + 1) * ROWS)
        qd, kk, vd, hilo = _hgrn_gates(zd[rs, 0:256], zd[rs, 256:512], zd[rs, 512:768], lb)
        pre_d.append((qd, kk, vd, hilo, _segsums(g_ref, G_LVL0, G_SUF + 1, hilo)))
    o_b_parts, o_c_parts, o_d_parts = [], [], []
    for c in range(tb // ROWS):
        rs = slice(c * ROWS, (c + 1) * ROWS)
        y = jnp.zeros((ROWS, GROUP_W), F32)
        for j in range(CONV_W):
            y = y + hc_ref[pl.ds(c * ROWS + 2 + j, ROWS), :] * cw_ref[0, j:j + 1, :]
        yn = _silu(_layer_norm(y + cb_ref[0], cng_ref[0], cnb_ref[0]))
        o_c_parts.append(jnp.dot(yn.astype(BF16), wpw_ref[0], preferred_element_type=F32) * _silu(zc[rs, 512:768]))
        vn = _layer_norm(_gelu(zb[rs, 256:512]), sgg_ref[0], sgb_ref[0])
        mixed = _sg_mix(vn, ws_ref.at[0], sgbias_ref[0])
        o_b_parts.append(_gelu(zb[rs, 0:256]) * mixed * _silu(zb[rs, 512:768]))
        qd, kk, vd, hilo, seg = pre_d[c]
        o = _hgrn_intra(qd, kk, vd, [seg[ROWS * n:ROWS * (n + 1)] for n in range(7)], e_ref, range(7))
        aq = qd * jnp.exp(seg[7 * ROWS:8 * ROWS])
        bk = kk * jnp.exp(seg[8 * ROWS:9 * ROWS])
        ones = jnp.ones((ROWS, D_KDIM), BF16)
        inter = []
        for hh in range(D_HEADS):
            lo_, hi_ = 64 * hh, 64 * hh + 64
            st = s_ref[hh]
            inter.append(jnp.dot(aq[:, lo_:hi_].astype(BF16), st.astype(BF16), preferred_element_type=F32))
            dec = (lax.dot_general(hilo[:, lo_:hi_], ones, _TN, preferred_element_type=F32)
                   + lax.dot_general(hilo[:, GROUP_W + lo_:GROUP_W + hi_], ones, _TN, preferred_element_type=F32))
            upd = lax.dot_general(bk[:, lo_:hi_].astype(BF16), vd[:, lo_:hi_].astype(BF16), _TN,
                                  preferred_element_type=F32)
            s_ref[hh] = jnp.exp(dec) * st + upd
        o = o + jnp.concatenate(inter, axis=1)
        o_d_parts.append(_head_rms(o, e_ref, hng_ref[0]) * _silu(zd[rs, 768:1024]))
    za = jnp.dot(h, win_ref[0, :, 0:1024], preferred_element_type=F32)
    cos = cos_ref[...]
    sin = sin_ref[...]
    q = _rope(za[:, 0:256], cos, sin)
    k = _rope(za[:, 256:512], cos, sin)
    v = za[:, 512:768]
    kt = k.T
    vt = v.T
    krow_ref[0] = kt
    vrow_ref[0] = vt
    kb_ref[0] = k.astype(BF16)
    qt_ref[0] = (q * Q_SCALE).T.astype(BF16)
    vt_ref[0] = vt.astype(BF16)
    ga_ref[0] = _silu(za[:, 768:1024])

    hc_ref[0:32, :] = hc_ref[pl.ds(tb, 32), :]
    o_b = jnp.concatenate(o_b_parts, axis=0)
    o_c = jnp.concatenate(o_c_parts, axis=0)
    o_d = jnp.concatenate(o_d_parts, axis=0)
    obcd_ref[0] = jnp.concatenate([o_b, o_c, o_d], axis=1).astype(BF16)
    hst_ref[0] = s_ref[...]


def _prompt_pre(layer, x, mod_p, win_b, cos, sin, sgg, sgb, ws, sgbias, cw, cb, cng, cnb, wpw_b, lbnd, hng,
                gmat, eones):
    b, t, _ = x.shape
    tb = min(PRE_TB, t)
    kernel = functools.partial(_prompt_pre_kernel, layer, tb)
    lsel3 = lambda bb, i: (layer, 0, 0)
    const2 = lambda bb, i: (0, 0)
    const3 = lambda bb, i: (0, 0, 0)
    row_blk = lambda w: pl.BlockSpec((1, tb, w), lambda bb, i: (bb, i, 0))
    col_blk = pl.BlockSpec((1, GROUP_W, tb), lambda bb, i: (bb, 0, i))
    out_shape = (
        jax.ShapeDtypeStruct((b, GROUP_W, t), F32),
        jax.ShapeDtypeStruct((b, GROUP_W, t), F32),
        jax.ShapeDtypeStruct((b, GROUP_W, t), BF16),
        jax.ShapeDtypeStruct((b, t, GROUP_W), BF16),
        jax.ShapeDtypeStruct((b, GROUP_W, t), BF16),
        jax.ShapeDtypeStruct((b, t, GROUP_W), F32),
        jax.ShapeDtypeStruct((b, t, 3 * GROUP_W), BF16),
        jax.ShapeDtypeStruct((b, CONV_W - 1, GROUP_W), F32),
        jax.ShapeDtypeStruct((b, D_HEADS, D_KDIM, D_KDIM), F32),
    )
    return pl.pallas_call(
        kernel,
        out_shape=out_shape,
        grid=(b, t // tb),
        in_specs=[
            row_blk(D_MODEL),
            pl.BlockSpec((1, 1, 3 * D_MODEL), lambda bb, i: (bb, 0, 0)),
            pl.BlockSpec((1, D_MODEL, D_IN), lsel3),
            pl.BlockSpec((tb, 128), lambda bb, i: (i, 0)),
            pl.BlockSpec((tb, 128), lambda bb, i: (i, 0)),
            pl.BlockSpec((1, 1, GROUP_W), lsel3),
            pl.BlockSpec((1, 1, GROUP_W), lsel3),
            pl.BlockSpec((1, 4, ROWS, ROWS), lambda bb, i: (layer, 0, 0, 0)),
            pl.BlockSpec((1, ROWS, GROUP_W), lsel3),
            pl.BlockSpec((1, CONV_W, GROUP_W), lsel3),
            pl.BlockSpec((1, 1, GROUP_W), lsel3),
            pl.BlockSpec((1, 1, GROUP_W), lsel3),
            pl.BlockSpec((1, 1, GROUP_W), lsel3),
            pl.BlockSpec((1, GROUP_W, GROUP_W), lsel3),
            pl.BlockSpec(lbnd.shape, const2),
            pl.BlockSpec((1, 1, GROUP_W), lsel3),
            pl.BlockSpec(gmat.shape, const3),
            pl.BlockSpec(eones.shape, const2),
        ],
        out_specs=(
            col_blk, col_blk, col_blk, row_blk(GROUP_W), col_blk, row_blk(GROUP_W),
            row_blk(3 * GROUP_W),
            pl.BlockSpec((1, CONV_W - 1, GROUP_W), lambda bb, i: (bb, 0, 0)),
            pl.BlockSpec((1, D_HEADS, D_KDIM, D_KDIM), lambda bb, i: (bb, 0, 0, 0)),
        ),
        scratch_shapes=[pltpu.VMEM((32 + tb, GROUP_W), F32), pltpu.VMEM((D_HEADS, D_KDIM, D_KDIM), F32)],
        compiler_params=pltpu.CompilerParams(dimension_semantics=("arbitrary", "arbitrary"),
                                             vmem_limit_bytes=VMEM_LIMIT),
        name=f"prompt_pre_l{layer}",
    )(x, mod_p, win_b, cos, sin, sgg, sgb, ws, sgbias, cw, cb, cng, cnb, wpw_b, lbnd, hng, gmat, eones)


def _prompt_attn_kernel(layer, depth, tq, tk,
                        qt_ref, kb_ref, vt_ref, ga_ref, obcd_ref, x_ref, mod_ref, wout_ref, lam_ref, ang_ref,
                        lng_ref, lnb_ref, y_ref, acc_ref, m_ref, qm_ref, s_ref):
    i = pl.program_id(1)
    qt = qt_ref[0]
    rowg = lax.broadcasted_iota(jnp.int32, (GROUP_W, 1), 0) >> 5
    for j in range(8):
        qm_ref[j] = jnp.where(rowg == j, qt, jnp.zeros_like(qt))
    m_ref[...] = jnp.full(m_ref.shape, NEG_BIG, F32)
    acc_ref[...] = jnp.zeros(acc_ref.shape, F32)
    ones = jnp.ones((16, tk), BF16)
    kpq = tq // tk

    def tile(kt, diag):
        masked = diag is not None
        off = pl.multiple_of(kt * tk, tk)
        kk = kb_ref[0, pl.ds(off, tk), :]
        if masked:
            key = lax.broadcasted_iota(jnp.int32, (tk, tq), 0) + diag * tk
            qry = lax.broadcasted_iota(jnp.int32, (tk, tq), 1)
            visible = key <= qry
        vvs = [jnp.concatenate([vt_ref[0, 64 * hh:64 * hh + 64, pl.ds(off, tk)], ones], axis=0)
               for hh in range(A_HEADS)]

        def scores(j):
            s = jnp.dot(kk, qm_ref[j], preferred_element_type=F32)
            if masked:
                s = jnp.where(visible, s, NEG_BIG)
            s_ref[j] = s
            return jnp.max(s, axis=0, keepdims=True)

        def update(j, cmax):
            m_old = m_ref[j]
            m_new = jnp.maximum(m_old, cmax)
            alpha = jnp.exp2(m_old - m_new)
            m_ref[j] = m_new
            for c in range(tq // LANE_TILE):
                cols = slice(LANE_TILE * c, LANE_TILE * (c + 1))
                p = jnp.exp2(s_ref[j, :, cols] - m_new[:, cols]).astype(BF16)
                acc_ref[j, :, cols] = (alpha[:, cols] * acc_ref[j, :, cols]
                                       + jnp.dot(vvs[j // 2], p, preferred_element_type=F32))

        cmax = {}
        for j in range(8 + SCORE_LEAD):
            if j < 8:
                cmax[j] = scores(j)
            if j >= SCORE_LEAD:
                update(j - SCORE_LEAD, cmax[j - SCORE_LEAD])

    def body(kt, carry):
        tile(kt, None)
        return carry

    lax.fori_loop(0, i * kpq, body, 0)
    for dg in range(kpq):
        tile(i * kpq + dg, dg)

    lam = _lam(lam_ref) + _lam_init(layer)
    heads = []
    for hh in range(A_HEADS):
        a0 = acc_ref[2 * hh]
        a1 = acc_ref[2 * hh + 1]
        d = a0[0:A_VDIM] / a0[A_VDIM:A_VDIM + 1] - lam * (a1[0:A_VDIM] / a1[A_VDIM:A_VDIM + 1])
        ms = jnp.mean(d * d, axis=0, keepdims=True)
        heads.append(d * lax.rsqrt(ms + EPS))
    o_a = jnp.concatenate(heads, axis=0).T
    o_a = o_a * (ang_ref[0] * (1.0 - _lam_init(layer))) * ga_ref[0]
    mixed = (jnp.dot(o_a.astype(BF16), wout_ref[0, 0:GROUP_W, :], preferred_element_type=F32)
             + jnp.dot(obcd_ref[0], wout_ref[0, GROUP_W:, :], preferred_element_type=F32))
    gate = mod_ref[0][:, 2 * D_MODEL:]
    y_ref[0] = _layer_norm(_alpha(depth) * x_ref[0] + gate * mixed, lng_ref[0], lnb_ref[0])


def _prompt_attn(layer, depth, qt, kb, vt, ga, obcd, x, mod_p, wout_b, lam_qk, ang, lng, lnb):
    b, t, _ = x.shape
    tq = min(ATTN_TQ, t)
    tk = min(ATTN_TK, tq)
    kernel = functools.partial(_prompt_attn_kernel, layer, depth, tq, tk)
    lsel3 = lambda bb, i: (layer, 0, 0)
    row_blk = lambda w: pl.BlockSpec((1, tq, w), lambda bb, i: (bb, i, 0))
    return pl.pallas_call(
        kernel,
        out_shape=jax.ShapeDtypeStruct((b, t, D_MODEL), F32),
        grid=(b, t // tq),
        in_specs=[
            pl.BlockSpec((1, GROUP_W, tq), lambda bb, i: (bb, 0, i)),
            pl.BlockSpec((1, t, GROUP_W), lambda bb, i: (bb, 0, 0)),
            pl.BlockSpec((1, GROUP_W, t), lambda bb, i: (bb, 0, 0)),
            row_blk(GROUP_W),
            row_blk(3 * GROUP_W),
            row_blk(D_MODEL),
            pl.BlockSpec((1, 1, 3 * D_MODEL), lambda bb, i: (bb, 0, 0)),
            pl.BlockSpec((1, D_MODEL, D_MODEL), lsel3),
            pl.BlockSpec((1, 4, A_HALF), lsel3),
            pl.BlockSpec((1, 1, GROUP_W), lsel3),
            pl.BlockSpec((1, 1, D_MODEL), lsel3),
            pl.BlockSpec((1, 1, D_MODEL), lsel3),
        ],
        out_specs=row_blk(D_MODEL),
        scratch_shapes=[pltpu.VMEM((8, A_VDIM + 16, tq), F32), pltpu.VMEM((8, 1, tq), F32),
                        pltpu.VMEM((8, GROUP_W, tq), BF16), pltpu.VMEM((8, tk, tq), F32)],
        compiler_params=pltpu.CompilerParams(dimension_semantics=("arbitrary", "arbitrary"),
                                             vmem_limit_bytes=VMEM_LIMIT),
        name=f"prompt_attn_l{layer}",
    )(qt, kb, vt, ga, obcd, x, mod_p, wout_b, lam_qk, ang, lng, lnb)


def _sample_pre_kernel(layer, ts,
                       x_ref, mod_ref, win_ref, cos_ref, sin_ref, sgg_ref, sgb_ref, ws_ref, sgbias_ref,
                       cw_ref, cb_ref, cng_ref, cnb_ref, wpw_ref, lb_ref, hng_ref, g_ref, e_ref, cst_ref, hst_ref,
                       krow_ref, vrow_ref, q_ref, ga_ref, obcd_ref, chv_ref, ncst_ref, nhst_ref,
                       hc_ref):
    nseq = ROWS // ts
    mod = mod_ref[...]
    shift = mod[:, :, 0:D_MODEL]
    scale = mod[:, :, D_MODEL:2 * D_MODEL]
    h = (x_ref[...] * (1.0 + scale) + shift).reshape(ROWS, D_MODEL).astype(BF16)
    z = jnp.dot(h, win_ref[0], preferred_element_type=F32)

    cos = cos_ref[...]
    sin = sin_ref[...]
    k = _rope(z[:, 256:512], cos, sin)
    krow_ref[...] = k
    vrow_ref[...] = z[:, 512:768]
    q_ref[...] = _rope(z[:, 0:256], cos, sin) * Q_SCALE
    ga_ref[...] = _silu(z[:, 768:1024])

    vn = _layer_norm(_gelu(z[:, 1280:1536]), sgg_ref[0], sgb_ref[0])
    chv_ref[...] = vn
    o_b = _gelu(z[:, 1024:1280]) * _sg_mix(vn, ws_ref.at[0], sgbias_ref[0]) * _silu(z[:, 1536:1792])

    a = z[:, 1792:2304]
    glu = a[:, :GROUP_W] * jax.nn.sigmoid(a[:, GROUP_W:])
    hc_ref[:, 0:CONV_W - 1, :] = cst_ref[0]
    hc_ref[:, CONV_W - 1:CONV_W - 1 + ts, :] = glu.reshape(nseq, ts, GROUP_W)
    y = jnp.zeros((nseq, ts, GROUP_W), F32)
    for j in range(CONV_W):
        y = y + hc_ref[:, j:j + ts, :] * cw_ref[0, j:j + 1, :]
    ncst_ref[0] = hc_ref[:, ts:ts + CONV_W - 1, :]
    yn = _silu(_layer_norm(y.reshape(ROWS, GROUP_W) + cb_ref[0], cng_ref[0], cnb_ref[0]))
    o_c = jnp.dot(yn.astype(BF16), wpw_ref[0], preferred_element_type=F32) * _silu(z[:, 2304:2560])

    lb = _lower_bound(lb_ref, layer)
    qd, kk, vd, hilo = _hgrn_gates(z[:, 2560:2816], z[:, 2816:3072], z[:, 3072:3328], lb)
    seg = _segsums(g_ref, G_CUM8, G_LVL0 + 3, hilo)
    o = _hgrn_intra(qd, kk, vd, [seg[ROWS * (2 + n):ROWS * (3 + n)] for n in range(3)], e_ref, range(3))
    aq = (qd * jnp.exp(seg[0:ROWS])).reshape(nseq, ts, GROUP_W)
    bk = (kk * jnp.exp(seg[ROWS:2 * ROWS])).reshape(nseq, ts, GROUP_W)
    v3 = vd.reshape(nseq, ts, GROUP_W)
    hilo3 = hilo.astype(F32).reshape(nseq, ts, 2 * GROUP_W)
    ones = jnp.ones((nseq, ts, D_KDIM), BF16)
    inter = []
    for hh in range(D_HEADS):
        lo_, hi_ = 64 * hh, 64 * hh + 64
        st = hst_ref[0, :, hh]
        inter.append(jnp.einsum('bqk,bkv->bqv', aq[:, :, lo_:hi_].astype(BF16), st.astype(BF16),
                                preferred_element_type=F32))
        dec = (jnp.einsum('bsk,bsv->bkv', hilo3[:, :, lo_:hi_].astype(BF16), ones, preferred_element_type=F32)
               + jnp.einsum('bsk,bsv->bkv', hilo3[:, :, GROUP_W + lo_:GROUP_W + hi_].astype(BF16), ones,
                            preferred_element_type=F32))
        upd = jnp.einsum('bsk,bsv->bkv', bk[:, :, lo_:hi_].astype(BF16), v3[:, :, lo_:hi_].astype(BF16),
                         preferred_element_type=F32)
        nhst_ref[0, :, hh] = jnp.exp(dec) * st + upd
    o = o + jnp.concatenate(inter, axis=2).reshape(ROWS, GROUP_W)
    o_d = _head_rms(o, e_ref, hng_ref[0]) * _silu(z[:, 3328:3584])
    obcd_ref[...] = jnp.concatenate([o_b, o_c, o_d], axis=1).astype(BF16)


def _sample_pre(layer, x, mod_s, win_b, cos, sin, sgg, sgb, ws_blk, sgbias, cw, cb, cng, cnb, wpw_b, lbnd, hng,
                gmat, eones, state_conv, state_hgrn):
    bs, ts, _ = x.shape
    nseq = ROWS // ts
    nblk = bs // nseq
    n = bs * ts
    kernel = functools.partial(_sample_pre_kernel, layer, ts)
    lsel3 = lambda i: (layer, 0, 0)
    const2 = lambda i: (0, 0)
    const3 = lambda i: (0, 0, 0)
    row_blk = lambda w: pl.BlockSpec((ROWS, w), lambda i: (i, 0))
    out_shape = (
        jax.ShapeDtypeStruct((n, GROUP_W), F32),
        jax.ShapeDtypeStruct((n, GROUP_W), F32),
        jax.ShapeDtypeStruct((n, GROUP_W), F32),
        jax.ShapeDtypeStruct((n, GROUP_W), F32),
        jax.ShapeDtypeStruct((n, 3 * GROUP_W), BF16),
        jax.ShapeDtypeStruct((n, GROUP_W), F32),
        jax.ShapeDtypeStruct((1, bs, CONV_W - 1, GROUP_W), F32),
        jax.ShapeDtypeStruct((1, bs, D_HEADS, D_KDIM, D_KDIM), F32),
    )
    return pl.pallas_call(
        kernel,
        out_shape=out_shape,
        grid=(nblk,),
        in_specs=[
            pl.BlockSpec((nseq, ts, D_MODEL), lambda i: (i, 0, 0)),
            pl.BlockSpec((nseq, 1, 3 * D_MODEL), lambda i: (i, 0, 0)),
            pl.BlockSpec((1, D_MODEL, D_IN), lsel3),
            pl.BlockSpec((ROWS, 128), const2),
            pl.BlockSpec((ROWS, 128), const2),
            pl.BlockSpec((1, 1, GROUP_W), lsel3),
            pl.BlockSpec((1, 1, GROUP_W), lsel3),
            pl.BlockSpec((1, 4, ROWS, ROWS), lambda i: (layer, 0, 0, 0)),
            pl.BlockSpec((1, ROWS, GROUP_W), lsel3),
            pl.BlockSpec((1, CONV_W, GROUP_W), lsel3),
            pl.BlockSpec((1, 1, GROUP_W), lsel3),
            pl.BlockSpec((1, 1, GROUP_W), lsel3),
            pl.BlockSpec((1, 1, GROUP_W), lsel3),
            pl.BlockSpec((1, GROUP_W, GROUP_W), lsel3),
            pl.BlockSpec(lbnd.shape, const2),
            pl.BlockSpec((1, 1, GROUP_W), lsel3),
            pl.BlockSpec(gmat.shape, const3),
            pl.BlockSpec(eones.shape, const2),
            pl.BlockSpec((1, nseq, CONV_W - 1, GROUP_W), lambda i: (layer, i, 0, 0)),
            pl.BlockSpec((1, nseq, D_HEADS, D_KDIM, D_KDIM), lambda i: (layer, i, 0, 0, 0)),
        ],
        out_specs=(
            row_blk(GROUP_W), row_blk(GROUP_W), row_blk(GROUP_W), row_blk(GROUP_W), row_blk(3 * GROUP_W),
            row_blk(GROUP_W),
            pl.BlockSpec((1, nseq, CONV_W - 1, GROUP_W), lambda i: (0, i, 0, 0)),
            pl.BlockSpec((1, nseq, D_HEADS, D_KDIM, D_KDIM), lambda i: (0, i, 0, 0, 0)),
        ),
        scratch_shapes=[pltpu.VMEM((nseq, 40, GROUP_W), F32)],
        compiler_params=pltpu.CompilerParams(dimension_semantics=("arbitrary",), vmem_limit_bytes=VMEM_LIMIT),
        name=f"sample_pre_l{layer}",
    )(x, mod_s, win_b, cos, sin, sgg, sgb, ws_blk, sgbias, cw, cb, cng, cnb, wpw_b, lbnd, hng, gmat, eones,
      state_conv, state_hgrn)


def _sample_attn_kernel(layer, n_pages, page, ts,
                        pt_ref, ck_hbm, cv_hbm, q_ref, kn_ref, vn_ref, ga_ref, lam_ref, ang_ref, e_ref, o_ref,
                        kbuf, vbuf, sem):
    b = pl.program_id(0)
    nb = pl.num_programs(0)

    def page_copies(seq, slot):
        copies = []
        for j in range(n_pages):
            pid = pt_ref[seq, j]
            copies.append(pltpu.make_async_copy(ck_hbm.at[layer, pid], kbuf.at[slot, j], sem.at[0, slot]))
            copies.append(pltpu.make_async_copy(cv_hbm.at[layer, pid], vbuf.at[slot, j], sem.at[1, slot]))
        return copies

    def start_all(copies):
        for cp in copies:
            cp.start()

    @pl.when(b == 0)
    def _():
        for d in range(PAGE_SLOTS - 1):
            start_all(page_copies(d, d))

    @pl.when(b + PAGE_SLOTS - 1 < nb)
    def _():
        start_all(page_copies(b + PAGE_SLOTS - 1, (b + PAGE_SLOTS - 1) % PAGE_SLOTS))

    slot = b % PAGE_SLOTS
    for cp in page_copies(b, slot):
        cp.wait()

    grp = lax.broadcasted_iota(jnp.int32, (1, GROUP_W), 1) >> 5
    head = lax.broadcasted_iota(jnp.int32, (1, GROUP_W), 1) >> 6
    t_q = lax.broadcasted_iota(jnp.int32, (8 * ts, ts), 0) & (ts - 1)
    t_k = lax.broadcasted_iota(jnp.int32, (8 * ts, ts), 1)
    vis = t_k <= t_q
    lam = _lam(lam_ref) + _lam_init(layer)
    q = q_ref[b]
    qexp = jnp.concatenate([jnp.where(grp == j, q, 0.0) for j in range(8)], axis=0).astype(BF16)
    kt_all = jnp.concatenate([kbuf[slot, j] for j in range(n_pages)], axis=1).astype(BF16)
    vt_all = jnp.concatenate([vbuf[slot, j] for j in range(n_pages)], axis=1).astype(BF16)
    s_past = jnp.dot(qexp, kt_all, preferred_element_type=F32)
    s_new = lax.dot_general(qexp, kn_ref[b].astype(BF16), _NT, preferred_element_type=F32)
    s_new = jnp.where(vis, s_new, NEG_BIG)
    m = jnp.maximum(jnp.max(s_past, axis=-1, keepdims=True), jnp.max(s_new, axis=-1, keepdims=True))
    p_past = jnp.exp2(s_past - m)
    p_new = jnp.where(vis, jnp.exp2(s_new - m), 0.0)
    l = jnp.sum(p_past, axis=-1, keepdims=True) + jnp.sum(p_new, axis=-1, keepdims=True)
    o = (lax.dot_general(p_past.astype(BF16), vt_all, _NT, preferred_element_type=F32)
         + jnp.dot(p_new.astype(BF16), vn_ref[b].astype(BF16), preferred_element_type=F32)) / l
    o_a = jnp.zeros((ts, GROUP_W), F32)
    for hh in range(A_HEADS):
        d = o[2 * hh * ts:(2 * hh + 1) * ts] - lam * o[(2 * hh + 1) * ts:(2 * hh + 2) * ts]
        o_a = o_a + jnp.where(head == hh, d, 0.0)
    o_a = _head_rms(o_a, e_ref, ang_ref[0] * (1.0 - _lam_init(layer))) * ga_ref[b]
    o_ref[b] = o_a.astype(BF16)


def _sample_attn(layer, page_table, cache_k, cache_v, q_s, k_new, v_new, ga_s, lam_qk, ang, eones, ts):
    bs, n_pages = page_table.shape
    page = cache_k.shape[3]
    assert bs >= PAGE_SLOTS
    kernel = functools.partial(_sample_attn_kernel, layer, n_pages, page, ts)
    seq_blk = pl.BlockSpec((bs, ts, GROUP_W), lambda b, pt: (0, 0, 0))
    lsel3 = lambda b, pt: (layer, 0, 0)
    grid_spec = pltpu.PrefetchScalarGridSpec(
        num_scalar_prefetch=1,
        grid=(bs,),
        in_specs=[pl.BlockSpec(memory_space=pl.ANY), pl.BlockSpec(memory_space=pl.ANY),
                  seq_blk, seq_blk, seq_blk, seq_blk,
                  pl.BlockSpec((1, 4, A_HALF), lsel3),
                  pl.BlockSpec((1, 1, GROUP_W), lsel3),
                  pl.BlockSpec(eones.shape, lambda b, pt: (0, 0))],
        out_specs=seq_blk,
        scratch_shapes=[pltpu.VMEM((PAGE_SLOTS, n_pages, GROUP_W, page), F32),
                        pltpu.VMEM((PAGE_SLOTS, n_pages, GROUP_W, page), F32),
                        pltpu.SemaphoreType.DMA((2, PAGE_SLOTS))],
    )
    shp3 = (bs, ts, GROUP_W)
    return pl.pallas_call(
        kernel,
        out_shape=jax.ShapeDtypeStruct(shp3, BF16),
        grid_spec=grid_spec,
        compiler_params=pltpu.CompilerParams(dimension_semantics=("arbitrary",), vmem_limit_bytes=VMEM_LIMIT),
        name=f"sample_attn_l{layer}",
    )(page_table, cache_k, cache_v,
      q_s.reshape(shp3), k_new.reshape(shp3), v_new.reshape(shp3), ga_s.reshape(shp3), lam_qk, ang, eones)


def _sample_out_kernel(depth, ts, oa_ref, obcd_ref, x_ref, mod_ref, wout_ref, lng_ref, lnb_ref, y_ref):
    nseq = ROWS // ts
    mixed = (jnp.dot(oa_ref[...], wout_ref[0, 0:GROUP_W, :], preferred_element_type=F32)
             + jnp.dot(obcd_ref[...], wout_ref[0, GROUP_W:, :], preferred_element_type=F32))
    gate = mod_ref[...][:, :, 2 * D_MODEL:]
    y = _alpha(depth) * x_ref[...] + gate * mixed.reshape(nseq, ts, D_MODEL)
    y_ref[...] = _layer_norm(y, lng_ref[0], lnb_ref[0])


def _sample_out(layer, depth, oa, obcd, x, mod_s, wout_b, lng, lnb):
    bs, ts, _ = x.shape
    nseq = ROWS // ts
    lsel3 = lambda i: (layer, 0, 0)
    return pl.pallas_call(
        functools.partial(_sample_out_kernel, depth, ts),
        out_shape=jax.ShapeDtypeStruct(x.shape, F32),
        grid=(bs // nseq,),
        in_specs=[
            pl.BlockSpec((ROWS, GROUP_W), lambda i: (i, 0)),
            pl.BlockSpec((ROWS, 3 * GROUP_W), lambda i: (i, 0)),
            pl.BlockSpec((nseq, ts, D_MODEL), lambda i: (i, 0, 0)),
            pl.BlockSpec((nseq, 1, 3 * D_MODEL), lambda i: (i, 0, 0)),
            pl.BlockSpec((1, D_MODEL, D_MODEL), lsel3),
            pl.BlockSpec((1, 1, D_MODEL), lsel3),
            pl.BlockSpec((1, 1, D_MODEL), lsel3),
        ],
        out_specs=pl.BlockSpec((nseq, ts, D_MODEL), lambda i: (i, 0, 0)),
        compiler_params=pltpu.CompilerParams(dimension_semantics=("arbitrary",), vmem_limit_bytes=VMEM_LIMIT),
        name=f"sample_out_l{layer}",
    )(oa, obcd, x, mod_s, wout_b, lng, lnb)


def _rope_tables(pos):
    half = A_HALF // 2
    inv = ROPE_THETA ** (-jnp.arange(half, dtype=F32) * 2.0 / A_HALF)
    ang = pos.astype(F32)[:, None] * inv[None, :]
    cos = jnp.cos(ang)
    sin = jnp.sin(ang)
    return jnp.tile(jnp.concatenate([cos, cos], -1), (1, 4)), jnp.tile(jnp.concatenate([-sin, sin], -1), (1, 4))


def kernel(x_prompt, x_sample, cache_k, cache_v, state_conv, state_hgrn, page_table, c_prompt, c_sample, w_ada, b_ada, w_in, lam_qk, attn_norm_g, sg_norm_g, sg_norm_b, w_s, b_s, conv_w, conv_b, conv_norm_g, conv_norm_b, w_pw, lower_bounds, hgrn_norm_g, w_out, ln_g, ln_b):
    depth = w_in.shape[0]
    bp, t, _ = x_prompt.shape
    bs, ts, _ = x_sample.shape
    n_pool, page = cache_k.shape[1], cache_k.shape[2]
    past_len = page_table.shape[1] * page
    assert ts == 8 and ROWS % ts == 0 and bs % (ROWS // ts) == 0 and t % ROWS == 0
    nseq = ROWS // ts

    gmat = _segment_matrices()
    eones = _head_ones()
    win_b = w_in.astype(BF16)
    wout_b = w_out.astype(BF16)
    wpw_b = w_pw.astype(BF16)
    row3 = lambda a: a.reshape(depth, 1, a.shape[-1])
    sgg, sgb, cb, cng, cnb = row3(sg_norm_g), row3(sg_norm_b), row3(conv_b), row3(conv_norm_g), row3(conv_norm_b)
    lng, lnb = row3(ln_g), row3(ln_b)
    ang = row3(jnp.tile(attn_norm_g, (1, A_HEADS)))
    hng = row3(jnp.tile(hgrn_norm_g, (1, D_HEADS)))
    sgbias_p = jnp.repeat(jnp.swapaxes(b_s, 1, 2), GROUP_W // 4, axis=2)
    sgbias_s = jnp.tile(sgbias_p[:, :ts], (1, nseq, 1))
    eye = jnp.eye(nseq, dtype=F32)
    ws_blk = jnp.einsum('ab,lgts->lgatbs', eye, w_s[:, :, :ts, :ts]).reshape(depth, 4, ROWS, ROWS)
    cos_p, sin_p = _rope_tables(jnp.arange(t))
    cos_s, sin_s = _rope_tables(past_len + jnp.arange(ts))
    cos_s, sin_s = jnp.tile(cos_s, (nseq, 1)), jnp.tile(sin_s, (nseq, 1))
    ck = jnp.transpose(cache_k, (0, 1, 3, 4, 2)).reshape(depth, n_pool, GROUP_W, page)
    cv = jnp.transpose(cache_v, (0, 1, 3, 4, 2)).reshape(depth, n_pool, GROUP_W, page)

    rows = bp + bs
    pad = (-rows) % 8
    c_all = jnp.concatenate([c_prompt, c_sample, jnp.zeros((pad, D_MODEL), F32)], axis=0)
    mod = _ada(c_all, w_ada, b_ada)

    xp, xs = x_prompt, x_sample
    outs = [[] for _ in range(9)]
    for l in range(depth):
        mod_p = mod[l, :bp].reshape(bp, 1, 3 * D_MODEL)
        mod_s = mod[l, bp:bp + bs].reshape(bs, 1, 3 * D_MODEL)
        krow, vrow, qt, kb, vt, ga, obcd, cst_p, hst_p = _prompt_pre(
            l, xp, mod_p, win_b, cos_p, sin_p, sgg, sgb, w_s, sgbias_p, conv_w, cb, cng, cnb, wpw_b, lower_bounds,
            hng, gmat, eones)
        xp = _prompt_attn(l, depth, qt, kb, vt, ga, obcd, xp, mod_p, wout_b, lam_qk, ang, lng, lnb)

        krow_s, vrow_s, q_s, ga_s, obcd_s, chv_s, cst_s, hst_s = _sample_pre(
            l, xs, mod_s, win_b, cos_s, sin_s, sgg, sgb, ws_blk, sgbias_s, conv_w, cb, cng, cnb, wpw_b,
            lower_bounds, hng, gmat, eones, state_conv, state_hgrn)
        oa_s = _sample_attn(l, page_table, ck, cv, q_s, krow_s, vrow_s, ga_s, lam_qk, ang, eones, ts)
        xs = _sample_out(l, depth, oa_s.reshape(bs * ts, GROUP_W), obcd_s, xs, mod_s, wout_b, lng, lnb)

        outs[0].append(jnp.transpose(krow.reshape(bp, A_HEADS, 2 * A_HALF, t), (0, 3, 1, 2)))
        outs[1].append(jnp.transpose(vrow.reshape(bp, A_HEADS, A_VDIM, t), (0, 3, 1, 2)))
        outs[2].append(krow_s.reshape(bs, ts, A_HEADS, 2 * A_HALF))
        outs[3].append(vrow_s.reshape(bs, ts, A_HEADS, A_VDIM))
        outs[4].append(chv_s.reshape(bs, ts, GROUP_W))
        outs[5].append(cst_p)
        outs[6].append(cst_s[0])
        outs[7].append(hst_p)
        outs[8].append(hst_s[0])
    return (xp, xs) + tuple(jnp.stack(o) for o in outs)
```

```python
import functools
import math

import jax
import jax.numpy as jnp
import numpy as np
from jax import lax
from jax.experimental import pallas as pl
from jax.experimental.pallas import tpu as pltpu

F32 = jnp.float32
BF16 = jnp.bfloat16

D_MODEL = 1024
GROUP_W = 256
A_HEADS = 4
A_HALF = 32
A_VDIM = 64
ROPE_THETA = 10000.0
SG_CHUNK = 128
CONV_W = 31
D_HEADS = 4
D_KDIM = 64
F_FLOOR = 1e-30
EPS = 1e-5
NEG_BIG = -1e30
D_IN = 14 * GROUP_W
SQRT_HALF = 0.7071067811865476
Q_SCALE = (A_HALF ** -0.5) * math.log2(math.e)

ROWS = 128
PRE_TB = 512
ATTN_TQ = 512
ATTN_TK = 512
PAGE_SLOTS = 4
LANE_TILE = 256
VMEM_LIMIT = 56 * 1024 * 1024

G_CUM8, G_SUF8, G_LVL0, G_CUM, G_SUF = 0, 1, 2, 9, 10

_NT = (((1,), (1,)), ((), ()))
_TN = (((0,), (0,)), ((), ()))


def _segment_matrices():
    t = np.arange(ROWS)[:, None]
    s = np.arange(ROWS)[None, :]
    same8 = (t >> 3) == (s >> 3)
    mats = [same8 & (s <= t), same8 & (s > t)]
    for lvl in range(7):
        mid = ((t >> (lvl + 1)) << (lvl + 1)) + (1 << lvl)
        second = ((t >> lvl) & 1) == 1
        mats.append(np.where(second, (s >= mid) & (s <= t), (s > t) & (s < mid)))
    mats.append(s <= t)
    mats.append(s > t)
    return jnp.asarray(np.stack(mats).astype(np.float32), dtype=BF16)


def _head_ones():
    h = np.arange(GROUP_W) // 64
    return jnp.asarray((h[:, None] == h[None, :]).astype(np.float32), dtype=BF16)


def _silu(x):
    return x * jax.nn.sigmoid(x)


def _gelu(x):
    return 0.5 * x * (1.0 + lax.erf(x * SQRT_HALF))


def _layer_norm(x, g, b):
    xc = x - jnp.mean(x, axis=-1, keepdims=True)
    var = jnp.mean(xc * xc, axis=-1, keepdims=True)
    return xc * lax.rsqrt(var + EPS) * g + b


def _head_rms(x, e_ref, g):
    ms = jnp.dot((x * x).astype(BF16), e_ref[...], preferred_element_type=F32) * (1.0 / 64.0)
    return x * lax.rsqrt(ms + EPS) * g


def _rope(x, cos, sin):
    outs = []
    for half in range(2):
        xh = x[:, 128 * half:128 * half + 128]
        lane = lax.broadcasted_iota(jnp.int32, xh.shape, 1)
        partner = jnp.where((lane & 16) == 0, pltpu.roll(xh, 112, 1), pltpu.roll(xh, 16, 1))
        outs.append(xh * cos + partner * sin)
    return jnp.concatenate(outs, axis=1)


def _lower_bound(lb_ref, layer):
    lb = lb_ref[...]
    e = jnp.exp(lb - jnp.max(lb, axis=0, keepdims=True))
    soft = e / jnp.sum(e, axis=0, keepdims=True)
    acc = jnp.zeros((1, GROUP_W), F32)
    for i in range(1, layer + 1):
        acc = acc + soft[i:i + 1]
    return acc


def _lam(lam_ref):
    lp = lam_ref[0]
    a = jnp.sum(lp[0:1] * lp[1:2], axis=-1, keepdims=True)
    b = jnp.sum(lp[2:3] * lp[3:4], axis=-1, keepdims=True)
    return jnp.exp(a) - jnp.exp(b)


def _lam_init(layer):
    return 0.8 - 0.6 * math.exp(-0.3 * layer)


def _alpha(depth):
    return (2.0 * depth) ** 0.25


def _sg_mix(vn, ws_ref, bias):
    group = lax.broadcasted_iota(jnp.int32, (1, GROUP_W), 1) >> 6
    r = lax.broadcasted_iota(jnp.int32, (ROWS, ROWS), 0)
    c = lax.broadcasted_iota(jnp.int32, (ROWS, ROWS), 1)
    acc = bias
    for g in range(4):
        wm = jnp.where(r >= c, ws_ref[g], 0.0).astype(BF16)
        vm = jnp.where(group == g, vn, 0.0).astype(BF16)
        acc = acc + jnp.dot(wm, vm, preferred_element_type=F32)
    return acc


def _hgrn_gates(dq, df, di, lb):
    f = lb + (1.0 - lb) * jax.nn.sigmoid(df)
    lf = jnp.log(jnp.maximum(f, F_FLOOR))
    hi = lf.astype(BF16)
    lo = (lf - hi.astype(F32)).astype(BF16)
    return _silu(dq), 1.0 - f, di, jnp.concatenate([hi, lo], axis=1)


def _segsums(g_ref, lo, hi, hilo):
    g = g_ref[lo:hi].reshape((hi - lo) * ROWS, ROWS)
    r = jnp.dot(g, hilo, preferred_element_type=F32)
    return r[:, :GROUP_W] + r[:, GROUP_W:]


def _hgrn_intra(qd, kk, vd, seg, e_ref, levels):
    row = lax.broadcasted_iota(jnp.int32, (ROWS, 1), 0)
    head = lax.broadcasted_iota(jnp.int32, (1, GROUP_W), 1) >> 6
    t_idx = lax.broadcasted_iota(jnp.int32, (ROWS, 4 * ROWS), 0)
    s_idx = lax.broadcasted_iota(jnp.int32, (ROWS, 4 * ROWS), 1) & (ROWS - 1)
    att = jnp.zeros((ROWS, 4 * ROWS), F32)
    for i, lvl in enumerate(levels):
        e = jnp.exp(seg[i])
        second = ((row >> lvl) & 1) == 1
        a = jnp.where(second, qd * e, 0.0).astype(BF16)
        b = jnp.where(second, 0.0, kk * e)
        b_heads = jnp.concatenate([jnp.where(head == h, b, 0.0) for h in range(4)], axis=0).astype(BF16)
        s = lax.dot_general(a, b_heads, _NT, preferred_element_type=F32)
        att = att + jnp.where((t_idx >> (lvl + 1)) == (s_idx >> (lvl + 1)), s, 0.0)
    v_stack = jnp.concatenate([jnp.where(head == h, vd, 0.0) for h in range(4)], axis=0).astype(BF16)
    o = jnp.dot(att.astype(BF16), v_stack, preferred_element_type=F32)
    diag = jnp.dot((qd * kk).astype(BF16), e_ref[...], preferred_element_type=F32)
    return o + diag * vd


def _ada_kernel(c_ref, w_ref, b_ref, o_ref):
    c = c_ref[...]
    a = _silu(c).astype(BF16)
    o_ref[0] = jnp.dot(a, w_ref[0].astype(BF16), preferred_element_type=F32) + b_ref[0]


def _ada(c_all, w_ada, b_ada):
    depth = w_ada.shape[0]
    rows = c_all.shape[0]
    return pl.pallas_call(
        _ada_kernel,
        out_shape=jax.ShapeDtypeStruct((depth, rows, 3 * D_MODEL), F32),
        grid=(depth, 3),
        in_specs=[
            pl.BlockSpec((rows, D_MODEL), lambda l, n: (0, 0)),
            pl.BlockSpec((1, D_MODEL, D_MODEL), lambda l, n: (l, 0, n)),
            pl.BlockSpec((1, 1, D_MODEL), lambda l, n: (l, 0, n)),
        ],
        out_specs=pl.BlockSpec((1, rows, D_MODEL), lambda l, n: (l, 0, n)),
        compiler_params=pltpu.CompilerParams(dimension_semantics=("arbitrary", "arbitrary"),
                                             vmem_limit_bytes=VMEM_LIMIT),
        name="adaln",
    )(c_all, w_ada, b_ada.reshape(depth, 1, 3 * D_MODEL))


def _prompt_pre_kernel(layer, tb,
                       x_ref, mod_ref, win_ref, cos_ref, sin_ref, sgg_ref, sgb_ref, ws_ref, sgbias_ref,
                       cw_ref, cb_ref, cng_ref, cnb_ref, wpw_ref, lb_ref, hng_ref, g_ref, e_ref,
                       krow_ref, vrow_ref, qt_ref, kb_ref, vt_ref, ga_ref, obcd_ref, cst_ref, hst_ref,
                       hc_ref, s_ref):
    i = pl.program_id(1)

    @pl.when(i == 0)
    def _():
        hc_ref[0:32, :] = jnp.zeros((32, GROUP_W), F32)
        s_ref[...] = jnp.zeros(s_ref.shape, F32)

    mod = mod_ref[0]
    shift = mod[:, 0:D_MODEL]
    scale = mod[:, D_MODEL:2 * D_MODEL]
    h = (x_ref[0] * (1.0 + scale) + shift).astype(BF16)
    zc = jnp.dot(h, win_ref[0, :, 1792:2560], preferred_element_type=F32)
    zb = jnp.dot(h, win_ref[0, :, 1024:1792], preferred_element_type=F32)
    zd = jnp.dot(h, win_ref[0, :, 2560:3584], preferred_element_type=F32)

    glu = zc[:, 0:GROUP_W] * jax.nn.sigmoid(zc[:, GROUP_W:2 * GROUP_W])
    hc_ref[32:32 + tb, :] = glu
    cst_ref[0] = hc_ref[pl.ds(tb + 2, CONV_W - 1), :]

    lb = _lower_bound(lb_ref, layer)
    pre_d = []
    for c in range(tb // ROWS):
        rs = slice(c * ROWS, (c + 1) * ROWS)
        qd, kk, vd, hilo = _hgrn_gates(zd[rs, 0:256], zd[rs, 256:512], zd[rs, 512:768], lb)
        pre_d.append((qd, kk, vd, hilo, _segsums(g_ref, G_LVL0, G_SUF + 1, hilo)))
    o_b_parts, o_c_parts, o_d_parts = [], [], []
    for c in range(tb // ROWS):
        rs = slice(c * ROWS, (c + 1) * ROWS)
        y = jnp.zeros((ROWS, GROUP_W), F32)
        for j in range(CONV_W):
            y = y + hc_ref[pl.ds(c * ROWS + 2 + j, ROWS), :] * cw_ref[0, j:j + 1, :]
        yn = _silu(_layer_norm(y + cb_ref[0], cng_ref[0], cnb_ref[0]))
        o_c_parts.append(jnp.dot(yn.astype(BF16), wpw_ref[0], preferred_element_type=F32) * _silu(zc[rs, 512:768]))
        vn = _layer_norm(_gelu(zb[rs, 256:512]), sgg_ref[0], sgb_ref[0])
        mixed = _sg_mix(vn, ws_ref.at[0], sgbias_ref[0])
        o_b_parts.append(_gelu(zb[rs, 0:256]) * mixed * _silu(zb[rs, 512:768]))
        qd, kk, vd, hilo, seg = pre_d[c]
        o = _hgrn_intra(qd, kk, vd, [seg[ROWS * n:ROWS * (n + 1)] for n in range(7)], e_ref, range(7))
        aq = qd * jnp.exp(seg[7 * ROWS:8 * ROWS])
        bk = kk * jnp.exp(seg[8 * ROWS:9 * ROWS])
        ones = jnp.ones((ROWS, D_KDIM), BF16)
        inter = []
        for hh in range(D_HEADS):
            lo_, hi_ = 64 * hh, 64 * hh + 64
            st = s_ref[hh]
            inter.append(jnp.dot(aq[:, lo_:hi_].astype(BF16), st.astype(BF16), preferred_element_type=F32))
            dec = (lax.dot_general(hilo[:, lo_:hi_], ones, _TN, preferred_element_type=F32)
                   + lax.dot_general(hilo[:, GROUP_W + lo_:GROUP_W + hi_], ones, _TN, preferred_element_type=F32))
            upd = lax.dot_general(bk[:, lo_:hi_].astype(BF16), vd[:, lo_:hi_].astype(BF16), _TN,
                                  preferred_element_type=F32)
            s_ref[hh] = jnp.exp(dec) * st + upd
        o = o + jnp.concatenate(inter, axis=1)
        o_d_parts.append(_head_rms(o, e_ref, hng_ref[0]) * _silu(zd[rs, 768:1024]))
    za = jnp.dot(h, win_ref[0, :, 0:1024], preferred_element_type=F32)
    cos = cos_ref[...]
    sin = sin_ref[...]
    q = _rope(za[:, 0:256], cos, sin)
    k = _rope(za[:, 256:512], cos, sin)
    v = za[:, 512:768]
    kt = k.T
    vt = v.T
    krow_ref[0] = kt
    vrow_ref[0] = vt
    kb_ref[0] = k.astype(BF16)
    qt_ref[0] = (q * Q_SCALE).T.astype(BF16)
    vt_ref[0] = vt.astype(BF16)
    ga_ref[0] = _silu(za[:, 768:1024])

    hc_ref[0:32, :] = hc_ref[pl.ds(tb, 32), :]
    o_b = jnp.concatenate(o_b_parts, axis=0)
    o_c = jnp.concatenate(o_c_parts, axis=0)
    o_d = jnp.concatenate(o_d_parts, axis=0)
    obcd_ref[0] = jnp.concatenate([o_b, o_c, o_d], axis=1).astype(BF16)
    hst_ref[0] = s_ref[...]


def _prompt_pre(layer, x, mod_p, win_b, cos, sin, sgg, sgb, ws, sgbias, cw, cb, cng, cnb, wpw_b, lbnd, hng,
                gmat, eones):
    b, t, _ = x.shape
    tb = min(PRE_TB, t)
    kernel = functools.partial(_prompt_pre_kernel, layer, tb)
    lsel3 = lambda bb, i: (layer, 0, 0)
    const2 = lambda bb, i: (0, 0)
    const3 = lambda bb, i: (0, 0, 0)
    row_blk = lambda w: pl.BlockSpec((1, tb, w), lambda bb, i: (bb, i, 0))
    col_blk = pl.BlockSpec((1, GROUP_W, tb), lambda bb, i: (bb, 0, i))
    out_shape = (
        jax.ShapeDtypeStruct((b, GROUP_W, t), F32),
        jax.ShapeDtypeStruct((b, GROUP_W, t), F32),
        jax.ShapeDtypeStruct((b, GROUP_W, t), BF16),
        jax.ShapeDtypeStruct((b, t, GROUP_W), BF16),
        jax.ShapeDtypeStruct((b, GROUP_W, t), BF16),
        jax.ShapeDtypeStruct((b, t, GROUP_W), F32),
        jax.ShapeDtypeStruct((b, t, 3 * GROUP_W), BF16),
        jax.ShapeDtypeStruct((b, CONV_W - 1, GROUP_W), F32),
        jax.ShapeDtypeStruct((b, D_HEADS, D_KDIM, D_KDIM), F32),
    )
    return pl.pallas_call(
        kernel,
        out_shape=out_shape,
        grid=(b, t // tb),
        in_specs=[
            row_blk(D_MODEL),
            pl.BlockSpec((1, 1, 3 * D_MODEL), lambda bb, i: (bb, 0, 0)),
            pl.BlockSpec((1, D_MODEL, D_IN), lsel3),
            pl.BlockSpec((tb, 128), lambda bb, i: (i, 0)),
            pl.BlockSpec((tb, 128), lambda bb, i: (i, 0)),
            pl.BlockSpec((1, 1, GROUP_W), lsel3),
            pl.BlockSpec((1, 1, GROUP_W), lsel3),
            pl.BlockSpec((1, 4, ROWS, ROWS), lambda bb, i: (layer, 0, 0, 0)),
            pl.BlockSpec((1, ROWS, GROUP_W), lsel3),
            pl.BlockSpec((1, CONV_W, GROUP_W), lsel3),
            pl.BlockSpec((1, 1, GROUP_W), lsel3),
            pl.BlockSpec((1, 1, GROUP_W), lsel3),
            pl.BlockSpec((1, 1, GROUP_W), lsel3),
            pl.BlockSpec((1, GROUP_W, GROUP_W), lsel3),
            pl.BlockSpec(lbnd.shape, const2),
            pl.BlockSpec((1, 1, GROUP_W), lsel3),
            pl.BlockSpec(gmat.shape, const3),
            pl.BlockSpec(eones.shape, const2),
        ],
        out_specs=(
            col_blk, col_blk, col_blk, row_blk(GROUP_W), col_blk, row_blk(GROUP_W),
            row_blk(3 * GROUP_W),
            pl.BlockSpec((1, CONV_W - 1, GROUP_W), lambda bb, i: (bb, 0, 0)),
            pl.BlockSpec((1, D_HEADS, D_KDIM, D_KDIM), lambda bb, i: (bb, 0, 0, 0)),
        ),
        scratch_shapes=[pltpu.VMEM((32 + tb, GROUP_W), F32), pltpu.VMEM((D_HEADS, D_KDIM, D_KDIM), F32)],
        compiler_params=pltpu.CompilerParams(dimension_semantics=("arbitrary", "arbitrary"),
                                             vmem_limit_bytes=VMEM_LIMIT),
        name=f"prompt_pre_l{layer}",
    )(x, mod_p, win_b, cos, sin, sgg, sgb, ws, sgbias, cw, cb, cng, cnb, wpw_b, lbnd, hng, gmat, eones)


def _prompt_attn_kernel(layer, depth, tq, tk,
                        qt_ref, kb_ref, vt_ref, ga_ref, obcd_ref, x_ref, mod_ref, wout_ref, lam_ref, ang_ref,
                        lng_ref, lnb_ref, y_ref, acc_ref, m_ref, qm_ref, s_ref):
    i = pl.program_id(1)
    qt = qt_ref[0]
    rowg = lax.broadcasted_iota(jnp.int32, (GROUP_W, 1), 0) >> 5
    for j in range(8):
        qm_ref[j] = jnp.where(rowg == j, qt, jnp.zeros_like(qt))
    m_ref[...] = jnp.full(m_ref.shape, NEG_BIG, F32)
    acc_ref[...] = jnp.zeros(acc_ref.shape, F32)
    ones = jnp.ones((16, tk), BF16)
    kpq = tq // tk

    def tile(kt, diag):
        masked = diag is not None
        off = pl.multiple_of(kt * tk, tk)
        kk = kb_ref[0, pl.ds(off, tk), :]
        if masked:
            key = lax.broadcasted_iota(jnp.int32, (tk, tq), 0) + diag * tk
            qry = lax.broadcasted_iota(jnp.int32, (tk, tq), 1)
            visible = key <= qry
        vvs = [jnp.concatenate([vt_ref[0, 64 * hh:64 * hh + 64, pl.ds(off, tk)], ones], axis=0)
               for hh in range(A_HEADS)]
        cmax = []
        for j in range(8):
            s = jnp.dot(kk, qm_ref[j], preferred_element_type=F32)
            if masked:
                s = jnp.where(visible, s, NEG_BIG)
            s_ref[j] = s
            cmax.append(jnp.max(s, axis=0, keepdims=True))
        for j in range(8):
            m_old = m_ref[j]
            m_new = jnp.maximum(m_old, cmax[j])
            alpha = jnp.exp2(m_old - m_new)
            m_ref[j] = m_new
            for c in range(tq // LANE_TILE):
                cols = slice(LANE_TILE * c, LANE_TILE * (c + 1))
                p = jnp.exp2(s_ref[j, :, cols] - m_new[:, cols]).astype(BF16)
                acc_ref[j, :, cols] = (alpha[:, cols] * acc_ref[j, :, cols]
                                       + jnp.dot(vvs[j // 2], p, preferred_element_type=F32))

    def body(kt, carry):
        tile(kt, None)
        return carry

    lax.fori_loop(0, i * kpq, body, 0)
    for dg in range(kpq):
        tile(i * kpq + dg, dg)

    lam = _lam(lam_ref) + _lam_init(layer)
    heads = []
    for hh in range(A_HEADS):
        a0 = acc_ref[2 * hh]
        a1 = acc_ref[2 * hh + 1]
        d = a0[0:A_VDIM] / a0[A_VDIM:A_VDIM + 1] - lam * (a1[0:A_VDIM] / a1[A_VDIM:A_VDIM + 1])
        ms = jnp.mean(d * d, axis=0, keepdims=True)
        heads.append(d * lax.rsqrt(ms + EPS))
    o_a = jnp.concatenate(heads, axis=0).T
    o_a = o_a * (ang_ref[0] * (1.0 - _lam_init(layer))) * ga_ref[0]
    mixed = (jnp.dot(o_a.astype(BF16), wout_ref[0, 0:GROUP_W, :], preferred_element_type=F32)
             + jnp.dot(obcd_ref[0], wout_ref[0, GROUP_W:, :], preferred_element_type=F32))
    gate = mod_ref[0][:, 2 * D_MODEL:]
    y_ref[0] = _layer_norm(_alpha(depth) * x_ref[0] + gate * mixed, lng_ref[0], lnb_ref[0])


def _prompt_attn(layer, depth, qt, kb, vt, ga, obcd, x, mod_p, wout_b, lam_qk, ang, lng, lnb):
    b, t, _ = x.shape
    tq = min(ATTN_TQ, t)
    tk = min(ATTN_TK, tq)
    kernel = functools.partial(_prompt_attn_kernel, layer, depth, tq, tk)
    lsel3 = lambda bb, i: (layer, 0, 0)
    row_blk = lambda w: pl.BlockSpec((1, tq, w), lambda bb, i: (bb, i, 0))
    return pl.pallas_call(
        kernel,
        out_shape=jax.ShapeDtypeStruct((b, t, D_MODEL), F32),
        grid=(b, t // tq),
        in_specs=[
            pl.BlockSpec((1, GROUP_W, tq), lambda bb, i: (bb, 0, i)),
            pl.BlockSpec((1, t, GROUP_W), lambda bb, i: (bb, 0, 0)),
            pl.BlockSpec((1, GROUP_W, t), lambda bb, i: (bb, 0, 0)),
            row_blk(GROUP_W),
            row_blk(3 * GROUP_W),
            row_blk(D_MODEL),
            pl.BlockSpec((1, 1, 3 * D_MODEL), lambda bb, i: (bb, 0, 0)),
            pl.BlockSpec((1, D_MODEL, D_MODEL), lsel3),
            pl.BlockSpec((1, 4, A_HALF), lsel3),
            pl.BlockSpec((1, 1, GROUP_W), lsel3),
            pl.BlockSpec((1, 1, D_MODEL), lsel3),
            pl.BlockSpec((1, 1, D_MODEL), lsel3),
        ],
        out_specs=row_blk(D_MODEL),
        scratch_shapes=[pltpu.VMEM((8, A_VDIM + 16, tq), F32), pltpu.VMEM((8, 1, tq), F32),
                        pltpu.VMEM((8, GROUP_W, tq), BF16), pltpu.VMEM((8, tk, tq), F32)],
        compiler_params=pltpu.CompilerParams(dimension_semantics=("arbitrary", "arbitrary"),
                                             vmem_limit_bytes=VMEM_LIMIT),
        name=f"prompt_attn_l{layer}",
    )(qt, kb, vt, ga, obcd, x, mod_p, wout_b, lam_qk, ang, lng, lnb)


def _sample_pre_kernel(layer, ts,
                       x_ref, mod_ref, win_ref, cos_ref, sin_ref, sgg_ref, sgb_ref, ws_ref, sgbias_ref,
                       cw_ref, cb_ref, cng_ref, cnb_ref, wpw_ref, lb_ref, hng_ref, g_ref, e_ref, cst_ref, hst_ref,
                       krow_ref, vrow_ref, q_ref, ga_ref, obcd_ref, chv_ref, ncst_ref, nhst_ref,
                       hc_ref):
    nseq = ROWS // ts
    mod = mod_ref[...]
    shift = mod[:, :, 0:D_MODEL]
    scale = mod[:, :, D_MODEL:2 * D_MODEL]
    h = (x_ref[...] * (1.0 + scale) + shift).reshape(ROWS, D_MODEL).astype(BF16)
    z = jnp.dot(h, win_ref[0], preferred_element_type=F32)

    cos = cos_ref[...]
    sin = sin_ref[...]
    k = _rope(z[:, 256:512], cos, sin)
    krow_ref[...] = k
    vrow_ref[...] = z[:, 512:768]
    q_ref[...] = _rope(z[:, 0:256], cos, sin) * Q_SCALE
    ga_ref[...] = _silu(z[:, 768:1024])

    vn = _layer_norm(_gelu(z[:, 1280:1536]), sgg_ref[0], sgb_ref[0])
    chv_ref[...] = vn
    o_b = _gelu(z[:, 1024:1280]) * _sg_mix(vn, ws_ref.at[0], sgbias_ref[0]) * _silu(z[:, 1536:1792])

    a = z[:, 1792:2304]
    glu = a[:, :GROUP_W] * jax.nn.sigmoid(a[:, GROUP_W:])
    hc_ref[:, 0:CONV_W - 1, :] = cst_ref[0]
    hc_ref[:, CONV_W - 1:CONV_W - 1 + ts, :] = glu.reshape(nseq, ts, GROUP_W)
    y = jnp.zeros((nseq, ts, GROUP_W), F32)
    for j in range(CONV_W):
        y = y + hc_ref[:, j:j + ts, :] * cw_ref[0, j:j + 1, :]
    ncst_ref[0] = hc_ref[:, ts:ts + CONV_W - 1, :]
    yn = _silu(_layer_norm(y.reshape(ROWS, GROUP_W) + cb_ref[0], cng_ref[0], cnb_ref[0]))
    o_c = jnp.dot(yn.astype(BF16), wpw_ref[0], preferred_element_type=F32) * _silu(z[:, 2304:2560])

    lb = _lower_bound(lb_ref, layer)
    qd, kk, vd, hilo = _hgrn_gates(z[:, 2560:2816], z[:, 2816:3072], z[:, 3072:3328], lb)
    seg = _segsums(g_ref, G_CUM8, G_LVL0 + 3, hilo)
    o = _hgrn_intra(qd, kk, vd, [seg[ROWS * (2 + n):ROWS * (3 + n)] for n in range(3)], e_ref, range(3))
    aq = (qd * jnp.exp(seg[0:ROWS])).reshape(nseq, ts, GROUP_W)
    bk = (kk * jnp.exp(seg[ROWS:2 * ROWS])).reshape(nseq, ts, GROUP_W)
    v3 = vd.reshape(nseq, ts, GROUP_W)
    hilo3 = hilo.astype(F32).reshape(nseq, ts, 2 * GROUP_W)
    ones = jnp.ones((nseq, ts, D_KDIM), BF16)
    inter = []
    for hh in range(D_HEADS):
        lo_, hi_ = 64 * hh, 64 * hh + 64
        st = hst_ref[0, :, hh]
        inter.append(jnp.einsum('bqk,bkv->bqv', aq[:, :, lo_:hi_].astype(BF16), st.astype(BF16),
                                preferred_element_type=F32))
        dec = (jnp.einsum('bsk,bsv->bkv', hilo3[:, :, lo_:hi_].astype(BF16), ones, preferred_element_type=F32)
               + jnp.einsum('bsk,bsv->bkv', hilo3[:, :, GROUP_W + lo_:GROUP_W + hi_].astype(BF16), ones,
                            preferred_element_type=F32))
        upd = jnp.einsum('bsk,bsv->bkv', bk[:, :, lo_:hi_].astype(BF16), v3[:, :, lo_:hi_].astype(BF16),
                         preferred_element_type=F32)
        nhst_ref[0, :, hh] = jnp.exp(dec) * st + upd
    o = o + jnp.concatenate(inter, axis=2).reshape(ROWS, GROUP_W)
    o_d = _head_rms(o, e_ref, hng_ref[0]) * _silu(z[:, 3328:3584])
    obcd_ref[...] = jnp.concatenate([o_b, o_c, o_d], axis=1).astype(BF16)


def _sample_pre(layer, x, mod_s, win_b, cos, sin, sgg, sgb, ws_blk, sgbias, cw, cb, cng, cnb, wpw_b, lbnd, hng,
                gmat, eones, state_conv, state_hgrn):
    bs, ts, _ = x.shape
    nseq = ROWS // ts
    nblk = bs // nseq
    n = bs * ts
    kernel = functools.partial(_sample_pre_kernel, layer, ts)
    lsel3 = lambda i: (layer, 0, 0)
    const2 = lambda i: (0, 0)
    const3 = lambda i: (0, 0, 0)
    row_blk = lambda w: pl.BlockSpec((ROWS, w), lambda i: (i, 0))
    out_shape = (
        jax.ShapeDtypeStruct((n, GROUP_W), F32),
        jax.ShapeDtypeStruct((n, GROUP_W), F32),
        jax.ShapeDtypeStruct((n, GROUP_W), F32),
        jax.ShapeDtypeStruct((n, GROUP_W), F32),
        jax.ShapeDtypeStruct((n, 3 * GROUP_W), BF16),
        jax.ShapeDtypeStruct((n, GROUP_W), F32),
        jax.ShapeDtypeStruct((1, bs, CONV_W - 1, GROUP_W), F32),
        jax.ShapeDtypeStruct((1, bs, D_HEADS, D_KDIM, D_KDIM), F32),
    )
    return pl.pallas_call(
        kernel,
        out_shape=out_shape,
        grid=(nblk,),
        in_specs=[
            pl.BlockSpec((nseq, ts, D_MODEL), lambda i: (i, 0, 0)),
            pl.BlockSpec((nseq, 1, 3 * D_MODEL), lambda i: (i, 0, 0)),
            pl.BlockSpec((1, D_MODEL, D_IN), lsel3),
            pl.BlockSpec((ROWS, 128), const2),
            pl.BlockSpec((ROWS, 128), const2),
            pl.BlockSpec((1, 1, GROUP_W), lsel3),
            pl.BlockSpec((1, 1, GROUP_W), lsel3),
            pl.BlockSpec((1, 4, ROWS, ROWS), lambda i: (layer, 0, 0, 0)),
            pl.BlockSpec((1, ROWS, GROUP_W), lsel3),
            pl.BlockSpec((1, CONV_W, GROUP_W), lsel3),
            pl.BlockSpec((1, 1, GROUP_W), lsel3),
            pl.BlockSpec((1, 1, GROUP_W), lsel3),
            pl.BlockSpec((1, 1, GROUP_W), lsel3),
            pl.BlockSpec((1, GROUP_W, GROUP_W), lsel3),
            pl.BlockSpec(lbnd.shape, const2),
            pl.BlockSpec((1, 1, GROUP_W), lsel3),
            pl.BlockSpec(gmat.shape, const3),
            pl.BlockSpec(eones.shape, const2),
            pl.BlockSpec((1, nseq, CONV_W - 1, GROUP_W), lambda i: (layer, i, 0, 0)),
            pl.BlockSpec((1, nseq, D_HEADS, D_KDIM, D_KDIM), lambda i: (layer, i, 0, 0, 0)),
        ],
        out_specs=(
            row_blk(GROUP_W), row_blk(GROUP_W), row_blk(GROUP_W), row_blk(GROUP_W), row_blk(3 * GROUP_W),
            row_blk(GROUP_W),
            pl.BlockSpec((1, nseq, CONV_W - 1, GROUP_W), lambda i: (0, i, 0, 0)),
            pl.BlockSpec((1, nseq, D_HEADS, D_KDIM, D_KDIM), lambda i: (0, i, 0, 0, 0)),
        ),
        scratch_shapes=[pltpu.VMEM((nseq, 40, GROUP_W), F32)],
        compiler_params=pltpu.CompilerParams(dimension_semantics=("arbitrary",), vmem_limit_bytes=VMEM_LIMIT),
        name=f"sample_pre_l{layer}",
    )(x, mod_s, win_b, cos, sin, sgg, sgb, ws_blk, sgbias, cw, cb, cng, cnb, wpw_b, lbnd, hng, gmat, eones,
      state_conv, state_hgrn)


def _sample_attn_kernel(layer, n_pages, page, ts,
                        pt_ref, ck_hbm, cv_hbm, q_ref, kn_ref, vn_ref, ga_ref, lam_ref, ang_ref, e_ref, o_ref,
                        kbuf, vbuf, sem):
    b = pl.program_id(0)
    nb = pl.num_programs(0)

    def page_copies(seq, slot):
        copies = []
        for j in range(n_pages):
            pid = pt_ref[seq, j]
            copies.append(pltpu.make_async_copy(ck_hbm.at[layer, pid], kbuf.at[slot, j], sem.at[0, slot]))
            copies.append(pltpu.make_async_copy(cv_hbm.at[layer, pid], vbuf.at[slot, j], sem.at[1, slot]))
        return copies

    def start_all(copies):
        for cp in copies:
            cp.start()

    @pl.when(b == 0)
    def _():
        for d in range(PAGE_SLOTS - 1):
            start_all(page_copies(d, d))

    @pl.when(b + PAGE_SLOTS - 1 < nb)
    def _():
        start_all(page_copies(b + PAGE_SLOTS - 1, (b + PAGE_SLOTS - 1) % PAGE_SLOTS))

    slot = b % PAGE_SLOTS
    for cp in page_copies(b, slot):
        cp.wait()

    grp = lax.broadcasted_iota(jnp.int32, (1, GROUP_W), 1) >> 5
    head = lax.broadcasted_iota(jnp.int32, (1, GROUP_W), 1) >> 6
    t_q = lax.broadcasted_iota(jnp.int32, (8 * ts, ts), 0) & (ts - 1)
    t_k = lax.broadcasted_iota(jnp.int32, (8 * ts, ts), 1)
    vis = t_k <= t_q
    lam = _lam(lam_ref) + _lam_init(layer)
    q = q_ref[b]
    qexp = jnp.concatenate([jnp.where(grp == j, q, 0.0) for j in range(8)], axis=0).astype(BF16)
    kt_all = jnp.concatenate([kbuf[slot, j] for j in range(n_pages)], axis=1).astype(BF16)
    vt_all = jnp.concatenate([vbuf[slot, j] for j in range(n_pages)], axis=1).astype(BF16)
    s_past = jnp.dot(qexp, kt_all, preferred_element_type=F32)
    s_new = lax.dot_general(qexp, kn_ref[b].astype(BF16), _NT, preferred_element_type=F32)
    s_new = jnp.where(vis, s_new, NEG_BIG)
    m = jnp.maximum(jnp.max(s_past, axis=-1, keepdims=True), jnp.max(s_new, axis=-1, keepdims=True))
    p_past = jnp.exp2(s_past - m)
    p_new = jnp.where(vis, jnp.exp2(s_new - m), 0.0)
    l = jnp.sum(p_past, axis=-1, keepdims=True) + jnp.sum(p_new, axis=-1, keepdims=True)
    o = (lax.dot_general(p_past.astype(BF16), vt_all, _NT, preferred_element_type=F32)
         + jnp.dot(p_new.astype(BF16), vn_ref[b].astype(BF16), preferred_element_type=F32)) / l
    o_a = jnp.zeros((ts, GROUP_W), F32)
    for hh in range(A_HEADS):
        d = o[2 * hh * ts:(2 * hh + 1) * ts] - lam * o[(2 * hh + 1) * ts:(2 * hh + 2) * ts]
        o_a = o_a + jnp.where(head == hh, d, 0.0)
    o_a = _head_rms(o_a, e_ref, ang_ref[0] * (1.0 - _lam_init(layer))) * ga_ref[b]
    o_ref[b] = o_a.astype(BF16)


def _sample_attn(layer, page_table, cache_k, cache_v, q_s, k_new, v_new, ga_s, lam_qk, ang, eones, ts):
    bs, n_pages = page_table.shape
    page = cache_k.shape[3]
    assert bs >= PAGE_SLOTS
    kernel = functools.partial(_sample_attn_kernel, layer, n_pages, page, ts)
    seq_blk = pl.BlockSpec((bs, ts, GROUP_W), lambda b, pt: (0, 0, 0))
    lsel3 = lambda b, pt: (layer, 0, 0)
    grid_spec = pltpu.PrefetchScalarGridSpec(
        num_scalar_prefetch=1,
        grid=(bs,),
        in_specs=[pl.BlockSpec(memory_space=pl.ANY), pl.BlockSpec(memory_space=pl.ANY),
                  seq_blk, seq_blk, seq_blk, seq_blk,
                  pl.BlockSpec((1, 4, A_HALF), lsel3),
                  pl.BlockSpec((1, 1, GROUP_W), lsel3),
                  pl.BlockSpec(eones.shape, lambda b, pt: (0, 0))],
        out_specs=seq_blk,
        scratch_shapes=[pltpu.VMEM((PAGE_SLOTS, n_pages, GROUP_W, page), F32),
                        pltpu.VMEM((PAGE_SLOTS, n_pages, GROUP_W, page), F32),
                        pltpu.SemaphoreType.DMA((2, PAGE_SLOTS))],
    )
    shp3 = (bs, ts, GROUP_W)
    return pl.pallas_call(
        kernel,
        out_shape=jax.ShapeDtypeStruct(shp3, BF16),
        grid_spec=grid_spec,
        compiler_params=pltpu.CompilerParams(dimension_semantics=("arbitrary",), vmem_limit_bytes=VMEM_LIMIT),
        name=f"sample_attn_l{layer}",
    )(page_table, cache_k, cache_v,
      q_s.reshape(shp3), k_new.reshape(shp3), v_new.reshape(shp3), ga_s.reshape(shp3), lam_qk, ang, eones)


def _sample_out_kernel(depth, ts, oa_ref, obcd_ref, x_ref, mod_ref, wout_ref, lng_ref, lnb_ref, y_ref):
    nseq = ROWS // ts
    mixed = (jnp.dot(oa_ref[...], wout_ref[0, 0:GROUP_W, :], preferred_element_type=F32)
             + jnp.dot(obcd_ref[...], wout_ref[0, GROUP_W:, :], preferred_element_type=F32))
    gate = mod_ref[...][:, :, 2 * D_MODEL:]
    y = _alpha(depth) * x_ref[...] + gate * mixed.reshape(nseq, ts, D_MODEL)
    y_ref[...] = _layer_norm(y, lng_ref[0], lnb_ref[0])


def _sample_out(layer, depth, oa, obcd, x, mod_s, wout_b, lng, lnb):
    bs, ts, _ = x.shape
    nseq = ROWS // ts
    lsel3 = lambda i: (layer, 0, 0)
    return pl.pallas_call(
        functools.partial(_sample_out_kernel, depth, ts),
        out_shape=jax.ShapeDtypeStruct(x.shape, F32),
        grid=(bs // nseq,),
        in_specs=[
            pl.BlockSpec((ROWS, GROUP_W), lambda i: (i, 0)),
            pl.BlockSpec((ROWS, 3 * GROUP_W), lambda i: (i, 0)),
            pl.BlockSpec((nseq, ts, D_MODEL), lambda i: (i, 0, 0)),
            pl.BlockSpec((nseq, 1, 3 * D_MODEL), lambda i: (i, 0, 0)),
            pl.BlockSpec((1, D_MODEL, D_MODEL), lsel3),
            pl.BlockSpec((1, 1, D_MODEL), lsel3),
            pl.BlockSpec((1, 1, D_MODEL), lsel3),
        ],
        out_specs=pl.BlockSpec((nseq, ts, D_MODEL), lambda i: (i, 0, 0)),
        compiler_params=pltpu.CompilerParams(dimension_semantics=("arbitrary",), vmem_limit_bytes=VMEM_LIMIT),
        name=f"sample_out_l{layer}",
    )(oa, obcd, x, mod_s, wout_b, lng, lnb)


def _rope_tables(pos):
    half = A_HALF // 2
    inv = ROPE_THETA ** (-jnp.arange(half, dtype=F32) * 2.0 / A_HALF)
    ang = pos.astype(F32)[:, None] * inv[None, :]
    cos = jnp.cos(ang)
    sin = jnp.sin(ang)
    return jnp.tile(jnp.concatenate([cos, cos], -1), (1, 4)), jnp.tile(jnp.concatenate([-sin, sin], -1), (1, 4))


def kernel(x_prompt, x_sample, cache_k, cache_v, state_conv, state_hgrn, page_table, c_prompt, c_sample, w_ada, b_ada, w_in, lam_qk, attn_norm_g, sg_norm_g, sg_norm_b, w_s, b_s, conv_w, conv_b, conv_norm_g, conv_norm_b, w_pw, lower_bounds, hgrn_norm_g, w_out, ln_g, ln_b):
    depth = w_in.shape[0]
    bp, t, _ = x_prompt.shape
    bs, ts, _ = x_sample.shape
    n_pool, page = cache_k.shape[1], cache_k.shape[2]
    past_len = page_table.shape[1] * page
    assert ts == 8 and ROWS % ts == 0 and bs % (ROWS // ts) == 0 and t % ROWS == 0
    nseq = ROWS // ts

    gmat = _segment_matrices()
    eones = _head_ones()
    win_b = w_in.astype(BF16)
    wout_b = w_out.astype(BF16)
    wpw_b = w_pw.astype(BF16)
    row3 = lambda a: a.reshape(depth, 1, a.shape[-1])
    sgg, sgb, cb, cng, cnb = row3(sg_norm_g), row3(sg_norm_b), row3(conv_b), row3(conv_norm_g), row3(conv_norm_b)
    lng, lnb = row3(ln_g), row3(ln_b)
    ang = row3(jnp.tile(attn_norm_g, (1, A_HEADS)))
    hng = row3(jnp.tile(hgrn_norm_g, (1, D_HEADS)))
    sgbias_p = jnp.repeat(jnp.swapaxes(b_s, 1, 2), GROUP_W // 4, axis=2)
    sgbias_s = jnp.tile(sgbias_p[:, :ts], (1, nseq, 1))
    eye = jnp.eye(nseq, dtype=F32)
    ws_blk = jnp.einsum('ab,lgts->lgatbs', eye, w_s[:, :, :ts, :ts]).reshape(depth, 4, ROWS, ROWS)
    cos_p, sin_p = _rope_tables(jnp.arange(t))
    cos_s, sin_s = _rope_tables(past_len + jnp.arange(ts))
    cos_s, sin_s = jnp.tile(cos_s, (nseq, 1)), jnp.tile(sin_s, (nseq, 1))
    ck = jnp.transpose(cache_k, (0, 1, 3, 4, 2)).reshape(depth, n_pool, GROUP_W, page)
    cv = jnp.transpose(cache_v, (0, 1, 3, 4, 2)).reshape(depth, n_pool, GROUP_W, page)

    rows = bp + bs
    pad = (-rows) % 8
    c_all = jnp.concatenate([c_prompt, c_sample, jnp.zeros((pad, D_MODEL), F32)], axis=0)
    mod = _ada(c_all, w_ada, b_ada)

    xp, xs = x_prompt, x_sample
    outs = [[] for _ in range(9)]
    for l in range(depth):
        mod_p = mod[l, :bp].reshape(bp, 1, 3 * D_MODEL)
        mod_s = mod[l, bp:bp + bs].reshape(bs, 1, 3 * D_MODEL)
        krow, vrow, qt, kb, vt, ga, obcd, cst_p, hst_p = _prompt_pre(
            l, xp, mod_p, win_b, cos_p, sin_p, sgg, sgb, w_s, sgbias_p, conv_w, cb, cng, cnb, wpw_b, lower_bounds,
            hng, gmat, eones)
        xp = _prompt_attn(l, depth, qt, kb, vt, ga, obcd, xp, mod_p, wout_b, lam_qk, ang, lng, lnb)

        krow_s, vrow_s, q_s, ga_s, obcd_s, chv_s, cst_s, hst_s = _sample_pre(
            l, xs, mod_s, win_b, cos_s, sin_s, sgg, sgb, ws_blk, sgbias_s, conv_w, cb, cng, cnb, wpw_b,
            lower_bounds, hng, gmat, eones, state_conv, state_hgrn)
        oa_s = _sample_attn(l, page_table, ck, cv, q_s, krow_s, vrow_s, ga_s, lam_qk, ang, eones, ts)
        xs = _sample_out(l, depth, oa_s.reshape(bs * ts, GROUP_W), obcd_s, xs, mod_s, wout_b, lng, lnb)

        outs[0].append(jnp.transpose(krow.reshape(bp, A_HEADS, 2 * A_HALF, t), (0, 3, 1, 2)))
        outs[1].append(jnp.transpose(vrow.reshape(bp, A_HEADS, A_VDIM, t), (0, 3, 1, 2)))
        outs[2].append(krow_s.reshape(bs, ts, A_HEADS, 2 * A_HALF))
        outs[3].append(vrow_s.reshape(bs, ts, A_HEADS, A_VDIM))
        outs[4].append(chv_s.reshape(bs, ts, GROUP_W))
        outs[5].append(cst_p)
        outs[6].append(cst_s[0])
        outs[7].append(hst_p)
        outs[8].append(hst_s[0])
    return (xp, xs) + tuple(jnp.stack(o) for o in outs)
```
